```python
import jax, jax.numpy as jnp
from jax import lax
import numpy as np

D_MODEL = 2048
BATCH = 1
SEQ = 8192
DEPTH = 2

RW_HEAD = 64
RW_HEADS = 16
RW_DIM = RW_HEADS * RW_HEAD
W_LORA = 96
A_LORA = 96
V_LORA = 64
G_LORA = 256
GN_EPS = 64e-5
MLA_HEADS = 8
Q_LORA = 512
KV_LORA = 512
QK_NOPE = 128
QK_ROPE = 64
V_HEAD = 128
MLA_DIM = MLA_HEADS * V_HEAD
ROPE_THETA = 10000.0
Q_BLOCK = 128
CHUNK = 128
SG_GROUPS = 8
SG_GROUP_DIM = 128
SG_DIM = SG_GROUPS * SG_GROUP_DIM
D_FF = 5632
CONV_W = 3
RMS_EPS = 1e-6
LN_EPS = 1e-5
N_RW = 3 * RW_DIM + W_LORA + A_LORA + G_LORA
N_DQ = Q_LORA
N_DKV = KV_LORA + QK_ROPE
N_SG = 2 * SG_DIM
N_GATE = 3 * D_MODEL
N_IN = N_RW + N_DQ + N_DKV + N_SG + N_GATE

kernel_name = "hybrid_rwkv7_mla_gmlp_gated_trunk"


def _split(x, sizes):
    idx = [int(i) for i in np.cumsum(sizes)[:-1]]
    return jnp.split(x, idx, axis=-1)


def _rmsnorm(x, g):
    xf = x.astype(jnp.float32)
    y = xf * lax.rsqrt(jnp.mean(xf * xf, axis=-1, keepdims=True) + RMS_EPS)
    return (y * g.astype(jnp.float32)).astype(x.dtype)


def _layernorm(x, g, b):
    xf = x.astype(jnp.float32)
    mu = jnp.mean(xf, axis=-1, keepdims=True)
    var = jnp.mean(jnp.square(xf - mu), axis=-1, keepdims=True)
    y = (xf - mu) * lax.rsqrt(var + LN_EPS) * g.astype(jnp.float32) + b.astype(jnp.float32)
    return y.astype(x.dtype)


def _rope_tables(positions):
    inv = ROPE_THETA ** (-jnp.arange(0, QK_ROPE, 2, dtype=jnp.float32) / QK_ROPE)
    ang = positions.astype(jnp.float32)[..., None] * inv
    return jnp.cos(ang), jnp.sin(ang)


def _rope(x, cos, sin):
    half = x.shape[-1] // 2
    x1, x2 = x[..., :half], x[..., half:]
    cos = cos.astype(x.dtype)
    sin = sin.astype(x.dtype)
    return jnp.concatenate([x1 * cos - x2 * sin, x1 * sin + x2 * cos], axis=-1)


def _rwkv7_scan(r, w, k, v, a, b):
    B_, T, H, N = r.shape

    def step(S, inp):
        r_t, w_t, k_t, v_t, a_t, b_t = inp
        sa = jnp.einsum('bhvk,bhk->bhv', S, a_t)
        S = (S * w_t[:, :, None, :] + sa[..., None] * b_t[:, :, None, :]
             + v_t[..., None] * k_t[:, :, None, :])
        return S, jnp.einsum('bhvk,bhk->bhv', S, r_t)

    S0 = jnp.zeros((B_, H, N, N), jnp.float32)
    xs = tuple(jnp.moveaxis(t, 1, 0) for t in (r, w, k, v, a, b))
    _, ys = lax.scan(step, S0, xs)
    return jnp.moveaxis(ys, 0, 1)


def _rwkv7_time_mix(p_rw, mu, w0, w2, a0, a2, g2, k_k, k_a, r_k, lnx_w, lnx_b, v_first, v_res):
    B_, T, _ = p_rw.shape
    f32 = jnp.float32
    prev = jnp.pad(p_rw, ((0, 0), (1, 0), (0, 0)))[:, :-1]
    p = p_rw + (prev - p_rw) * mu
    r, k, v, xw, xa, xg = _split(p, [RW_DIM, RW_DIM, RW_DIM, W_LORA, A_LORA, G_LORA])
    if v_res is not None:
        v0, v1, v2 = v_res
        v = v + (v_first - v) * jax.nn.sigmoid(v0 + (v @ v1) @ v2)
    w_log = -jax.nn.softplus(-(w0 + jnp.tanh(xw) @ w2).astype(f32)) - 0.5
    decay = jnp.exp(-jnp.exp(w_log))
    a = jax.nn.sigmoid((a0 + xa @ a2).astype(f32))
    g = jax.nn.sigmoid(xg) @ g2

    def heads(t):
        return t.astype(f32).reshape(B_, T, RW_HEADS, RW_HEAD)

    kk = heads(k * k_k)
    kk = kk / jnp.maximum(jnp.sqrt(jnp.sum(kk * kk, axis=-1, keepdims=True)), 1e-12)
    kh = heads(k.astype(f32) * (1.0 + (a - 1.0) * k_a.astype(f32)))
    rh, vh, ah = heads(r), heads(v), heads(a)
    y = _rwkv7_scan(rh, heads(decay), kh, vh, -kk, kk * ah)
    mean = jnp.mean(y, axis=-1, keepdims=True)
    var = jnp.mean(jnp.square(y - mean), axis=-1, keepdims=True)
    y = ((y - mean) * lax.rsqrt(var + GN_EPS)).reshape(B_, T, RW_DIM)
    y = y * lnx_w.astype(f32) + lnx_b.astype(f32)
    bonus = (jnp.sum(rh * kh * r_k.astype(f32), axis=-1, keepdims=True) * vh).reshape(B_, T, RW_DIM)
    out = ((y + bonus) * g.astype(f32)).astype(p_rw.dtype)
    return out, v


def _mla(p_dq, p_dkv, cos, sin, q_norm, kv_norm, w_uq, w_ukv, qn_nope, qn_rope, kn_nope, kn_rope):
    B_, T, _ = p_dq.shape
    q = (_rmsnorm(p_dq, q_norm) @ w_uq).reshape(B_, T, MLA_HEADS, QK_NOPE + QK_ROPE)
    q_nope, q_rope = q[..., :QK_NOPE], q[..., QK_NOPE:]
    c_kv, k_rope = p_dkv[..., :KV_LORA], p_dkv[..., KV_LORA:]
    kv = (_rmsnorm(c_kv, kv_norm) @ w_ukv).reshape(B_, T, MLA_HEADS, QK_NOPE + V_HEAD)
    k_nope, v = kv[..., :QK_NOPE], kv[..., QK_NOPE:]
    q_nope = _rmsnorm(q_nope, qn_nope)
    k_nope = _rmsnorm(k_nope, kn_nope)
    q_rope = _rope(_rmsnorm(q_rope, qn_rope), cos[:, :, None, :], sin[:, :, None, :])
    k_rope = _rope(_rmsnorm(k_rope, kn_rope), cos, sin)
    scale = (QK_NOPE + QK_ROPE) ** -0.5
    k_idx = jnp.arange(T)

    def block(i):
        qs = i * Q_BLOCK
        qn = lax.dynamic_slice_in_dim(q_nope, qs, Q_BLOCK, axis=1)
        qr = lax.dynamic_slice_in_dim(q_rope, qs, Q_BLOCK, axis=1)
        s = (jnp.einsum('bqhd,bkhd->bhqk', qn, k_nope)
             + jnp.einsum('bqhd,bkd->bhqk', qr, k_rope)).astype(jnp.float32) * scale
        q_idx = qs + jnp.arange(Q_BLOCK)
        s = jnp.where(k_idx[None, :] <= q_idx[:, None], s, -jnp.inf)
        pr = jax.nn.softmax(s, axis=-1).astype(v.dtype)
        return jnp.einsum('bhqk,bkhd->bqhd', pr, v)

    o = lax.map(block, jnp.arange(T // Q_BLOCK))
    return jnp.moveaxis(o, 0, 1).reshape(B_, T, MLA_DIM)


def _spatial_gating(p_u, p_v, ln_w, ln_b, ws, bs):
    B_, T, _ = p_u.shape
    u = jax.nn.gelu(p_u)
    v = _layernorm(jax.nn.gelu(p_v), ln_w, ln_b)
    vc = v.reshape(B_, T // CHUNK, CHUNK, SG_GROUPS, SG_GROUP_DIM)
    mask = jnp.tril(jnp.ones((CHUNK, CHUNK), dtype=bool))
    w = jnp.where(mask[None], ws, jnp.zeros_like(ws))
    s = jnp.einsum('gts,bnsgc->bntgc', w, vc) + bs.T[None, None, :, :, None]
    return u * s.reshape(B_, T, SG_DIM)


def _conv_ffn(h, w_up, conv_w, conv_b, w_down):
    up = h @ w_up
    T = up.shape[1]
    padded = jnp.pad(up, ((0, 0), (CONV_W - 1, 0), (0, 0)))
    conv = conv_b + conv_w[0] * padded[:, 0:T]
    for j in range(1, CONV_W):
        conv = conv + conv_w[j] * padded[:, j:j + T]
    gate, val = jnp.split(conv, 2, axis=-1)
    return (jax.nn.silu(gate) * val) @ w_down


def setup_inputs(seed: int = 0) -> dict:
    key = jax.random.key(seed)
    ks = iter(jax.random.split(key, 64))
    f32 = jnp.float32
    D, L, Lv = D_MODEL, DEPTH, DEPTH - 1

    def nrm(shape, scale):
        return jax.random.normal(next(ks), shape, f32) * scale

    def gain(shape):
        return 1.0 + nrm(shape, 0.05)

    return {
        "x": nrm((BATCH, SEQ, D), 1.0),
        "c": nrm((BATCH, D), 1.0),
        "positions": (jnp.arange(SEQ, dtype=jnp.int32)[None, :]
                      + jax.random.randint(next(ks), (BATCH, 1), 0, SEQ, dtype=jnp.int32)),
        "ada_w": nrm((L, D, 6 * D), 0.5 * D ** -0.5),
        "ada_b": nrm((L, 6 * D), 0.01),
        "norm_mix_g": gain((L, D)),
        "norm_ffn_g": gain((L, D)),
        "w_in": nrm((L, D, N_IN), D ** -0.5),
        "rw_mu": jax.random.uniform(next(ks), (L, N_RW), f32),
        "rw_w0": jax.random.uniform(next(ks), (L, RW_DIM), f32, -5.0, -0.5),
        "rw_w2": nrm((L, W_LORA, RW_DIM), W_LORA ** -0.5),
        "rw_a0": nrm((L, RW_DIM), 0.1),
        "rw_a2": nrm((L, A_LORA, RW_DIM), A_LORA ** -0.5),
        "rw_g2": nrm((L, G_LORA, RW_DIM), G_LORA ** -0.5),
        "rw_kk": 0.85 + nrm((L, RW_DIM), 0.05),
        "rw_ka": gain((L, RW_DIM)),
        "rw_rk": nrm((L, RW_HEADS, RW_HEAD), 0.1),
        "rw_lnx_w": gain((L, RW_DIM)),
        "rw_lnx_b": nrm((L, RW_DIM), 0.01),
        "rw_v0": nrm((Lv, RW_DIM), 0.1),
        "rw_v1": nrm((Lv, RW_DIM, V_LORA), RW_DIM ** -0.5),
        "rw_v2": nrm((Lv, V_LORA, RW_DIM), V_LORA ** -0.5),
        "mla_q_norm": gain((L, Q_LORA)),
        "mla_kv_norm": gain((L, KV_LORA)),
        "mla_w_uq": nrm((L, Q_LORA, MLA_HEADS * (QK_NOPE + QK_ROPE)), Q_LORA ** -0.5),
        "mla_w_ukv": nrm((L, KV_LORA, MLA_HEADS * (QK_NOPE + V_HEAD)), KV_LORA ** -0.5),
        "mla_qn_nope": gain((L, QK_NOPE)),
        "mla_qn_rope": gain((L, QK_ROPE)),
        "mla_kn_nope": gain((L, QK_NOPE)),
        "mla_kn_rope": gain((L, QK_ROPE)),
        "sg_ln_w": gain((L, SG_DIM)),
        "sg_ln_b": nrm((L, SG_DIM), 0.01),
        "sg_ws": nrm((L, SG_GROUPS, CHUNK, CHUNK), CHUNK ** -0.5),
        "sg_b": 1.0 + nrm((L, SG_GROUPS, CHUNK), 0.1),
        "w_br_a": nrm((L, RW_DIM, D), RW_DIM ** -0.5),
        "w_br_b": nrm((L, MLA_DIM, D), MLA_DIM ** -0.5),
        "w_br_c": nrm((L, SG_DIM, D), SG_DIM ** -0.5),
        "w_out": nrm((L, D, D), D ** -0.5),
        "ffn_up": nrm((L, D, 2 * D_FF), D ** -0.5),
        "ffn_conv": nrm((L, CONV_W, 2 * D_FF), CONV_W ** -0.5),
        "ffn_conv_b": nrm((L, 2 * D_FF), 0.01),
        "ffn_down": nrm((L, D_FF, D), D_FF ** -0.5),
    }


def reference(x, c, positions, ada_w, ada_b, norm_mix_g, norm_ffn_g, w_in,
              rw_mu, rw_w0, rw_w2, rw_a0, rw_a2, rw_g2, rw_kk, rw_ka, rw_rk, rw_lnx_w, rw_lnx_b,
              rw_v0, rw_v1, rw_v2,
              mla_q_norm, mla_kv_norm, mla_w_uq, mla_w_ukv, mla_qn_nope, mla_qn_rope,
              mla_kn_nope, mla_kn_rope,
              sg_ln_w, sg_ln_b, sg_ws, sg_b,
              w_br_a, w_br_b, w_br_c, w_out,
              ffn_up, ffn_conv, ffn_conv_b, ffn_down):
    B_, T, D = x.shape
    cos, sin = _rope_tables(positions)
    v_first = None
    for l in range(DEPTH):
        mod = jax.nn.silu(c) @ ada_w[l] + ada_b[l]
        sh1, sc1, gt1, sh2, sc2, gt2 = jnp.split(mod[:, None, :], 6, axis=-1)

        h = _rmsnorm(x, norm_mix_g[l]) * (1.0 + sc1) + sh1
        proj = h @ w_in[l]
        p_rw, p_dq, p_dkv, p_u, p_v, p_gate = _split(
            proj, [N_RW, N_DQ, N_DKV, SG_DIM, SG_DIM, N_GATE])
        v_res = None if l == 0 else (rw_v0[l - 1], rw_v1[l - 1], rw_v2[l - 1])
        y_a, v_l = _rwkv7_time_mix(p_rw, rw_mu[l], rw_w0[l], rw_w2[l], rw_a0[l], rw_a2[l],
                                   rw_g2[l], rw_kk[l], rw_ka[l], rw_rk[l], rw_lnx_w[l],
                                   rw_lnx_b[l], v_first, v_res)
        if l == 0:
            v_first = v_l
        y_b = _mla(p_dq, p_dkv, cos, sin, mla_q_norm[l], mla_kv_norm[l], mla_w_uq[l],
                   mla_w_ukv[l], mla_qn_nope[l], mla_qn_rope[l], mla_kn_nope[l], mla_kn_rope[l])
        y_c = _spatial_gating(p_u, p_v, sg_ln_w[l], sg_ln_b[l], sg_ws[l], sg_b[l])
        gates = jax.nn.sigmoid(p_gate.astype(jnp.float32)).astype(x.dtype).reshape(B_, T, 3, D)
        merged = (gates[:, :, 0] * (y_a @ w_br_a[l])
                  + gates[:, :, 1] * (y_b @ w_br_b[l])
                  + gates[:, :, 2] * (y_c @ w_br_c[l]))
        x = x + gt1 * (merged @ w_out[l])

        h = _rmsnorm(x, norm_ffn_g[l]) * (1.0 + sc2) + sh2
        x = x + gt2 * _conv_ffn(h, ffn_up[l], ffn_conv[l], ffn_conv_b[l], ffn_down[l])
    return x
```

```python
import functools

import jax
import jax.numpy as jnp
from jax import lax
from jax.experimental import pallas as pl
from jax.experimental.pallas import tpu as pltpu

F32 = jnp.float32
BF16 = jnp.bfloat16

D_MODEL = 2048
RW_HEAD = 64
RW_HEADS = 16
RW_DIM = RW_HEADS * RW_HEAD
W_LORA = 96
A_LORA = 96
V_LORA = 64
G_LORA = 256
GN_EPS = 64e-5
MLA_HEADS = 8
Q_LORA = 512
KV_LORA = 512
QK_NOPE = 128
QK_ROPE = 64
V_HEAD = 128
MLA_DIM = MLA_HEADS * V_HEAD
ROPE_THETA = 10000.0
CHUNK = 128
SG_GROUPS = 8
SG_GROUP_DIM = 128
SG_DIM = SG_GROUPS * SG_GROUP_DIM
D_FF = 5632
CONV_W = 3
RMS_EPS = 1e-6
LN_EPS = 1e-5

LANES = 128
SUBLANES = 8
LORA_PAD = 128
N_RW = 3 * RW_DIM + W_LORA + A_LORA + G_LORA
N_RW_PAD = 3 * RW_DIM + 2 * LORA_PAD + G_LORA
ROPE_PAD = 128
QK_PAD = QK_NOPE + ROPE_PAD
N_MLA_PAD = Q_LORA + KV_LORA + ROPE_PAD
VMEM_LIMIT = 52 * 1024 * 1024


def _cparams(sem):
    return pltpu.CompilerParams(dimension_semantics=sem, vmem_limit_bytes=VMEM_LIMIT)


def _sigmoid(x):
    return 1.0 / (1.0 + jnp.exp(-x))


def _softplus(x):
    return jnp.maximum(x, 0.0) + jnp.log(1.0 + jnp.exp(-jnp.abs(x)))


def _gelu_tanh(x):
    return 0.5 * x * (1.0 + jnp.tanh(0.7978845608028654 * (x + 0.044715 * (x * x * x))))


def _bdot(a, b):
    return jnp.dot(a.astype(BF16), b, preferred_element_type=F32)


def _group_sum(x, ones_bd):
    hi = x.astype(BF16)
    lo = (x - hi.astype(F32)).astype(BF16)
    return (jnp.dot(hi, ones_bd, preferred_element_type=F32)
            + jnp.dot(lo, ones_bd, preferred_element_type=F32))


def _ada_body(c_ref, w_ref, b_ref, o_ref, *, kc):
    d = c_ref.shape[0]
    tn = o_ref.shape[-1]

    def step(i, acc):
        ck = c_ref[pl.ds(i * kc, kc), :]
        sk = ck * _sigmoid(ck)
        wk = w_ref[0, pl.ds(i * kc, kc), :]
        return acc + jnp.sum(wk * sk, axis=0, keepdims=True)

    acc = lax.fori_loop(0, d // kc, step, jnp.zeros((1, tn), F32))
    o_ref[0] = acc + b_ref[0]


def _ada_all(c, ada_w, ada_b):
    nl, d, n = ada_w.shape
    tn = 1024
    out = pl.pallas_call(
        functools.partial(_ada_body, kc=256),
        grid=(nl, n // tn),
        in_specs=[pl.BlockSpec((d, 1), lambda l, j: (0, 0)),
                  pl.BlockSpec((1, d, tn), lambda l, j: (l, 0, j)),
                  pl.BlockSpec((1, 1, tn), lambda l, j: (l, 0, j))],
        out_specs=pl.BlockSpec((1, 1, tn), lambda l, j: (l, 0, j)),
        out_shape=jax.ShapeDtypeStruct((nl, 1, n), F32),
        compiler_params=_cparams(("arbitrary", "arbitrary")),
        name="ada_mod",
    )(c.reshape(d, 1), ada_w, ada_b.reshape(nl, 1, n))
    return out


def _rope_body(pos_ref, inv_ref, msk_ref, sgn_ref, cos_ref, sin_ref):
    ang = pos_ref[...].astype(F32) * inv_ref[...]
    cos_ref[...] = jnp.cos(ang) * msk_ref[...]
    sin_ref[...] = jnp.sin(ang) * sgn_ref[...]


def _rope_pad(v):
    h = QK_ROPE // 2
    z = jnp.zeros(v.shape[:-1] + (ROPE_PAD // 2 - h,), v.dtype)
    return jnp.concatenate([v[..., :h], z, v[..., h:], z], axis=-1)


def _rope_tables(positions, t):
    tm = min(1024, t)
    inv = ROPE_THETA ** (-jnp.arange(0, QK_ROPE, 2, dtype=F32) / QK_ROPE)
    ones = jnp.ones((QK_ROPE // 2,), F32)
    inv_p = _rope_pad(jnp.concatenate([inv, inv]))[None]
    msk_p = _rope_pad(jnp.concatenate([ones, ones]))[None]
    sgn_p = _rope_pad(jnp.concatenate([-ones, ones]))[None]
    row = pl.BlockSpec((1, ROPE_PAD), lambda i: (0, 0))
    blk = pl.BlockSpec((tm, ROPE_PAD), lambda i: (i, 0))
    return pl.pallas_call(
        _rope_body,
        grid=(t // tm,),
        in_specs=[pl.BlockSpec((tm, 1), lambda i: (i, 0)), row, row, row],
        out_specs=[blk, blk],
        out_shape=[jax.ShapeDtypeStruct((t, ROPE_PAD), F32)] * 2,
        compiler_params=_cparams(("arbitrary",)),
        name="rope_tables",
    )(positions.reshape(t, 1), inv_p, msk_p, sgn_p)


def _norm_mod_body(x_ref, g_ref, sc_ref, sh_ref, o_ref):
    x = x_ref[...]
    y = x * lax.rsqrt(jnp.mean(x * x, axis=-1, keepdims=True) + RMS_EPS) * g_ref[...]
    o_ref[...] = (y * (1.0 + sc_ref[...]) + sh_ref[...]).astype(o_ref.dtype)


def _norm_mod(x, g, sc, sh):
    t, d = x.shape
    tm = min(512, t)
    row = pl.BlockSpec((1, d), lambda i: (0, 0))
    return pl.pallas_call(
        _norm_mod_body,
        grid=(t // tm,),
        in_specs=[pl.BlockSpec((tm, d), lambda i: (i, 0)), row, row, row],
        out_specs=pl.BlockSpec((tm, d), lambda i: (i, 0)),
        out_shape=jax.ShapeDtypeStruct((t, d), BF16),
        compiler_params=_cparams(("arbitrary",)),
        name="norm_mod",
    )(x, g, sc, sh)


def _mm_body(*refs, nk, epi):
    if epi == "residual":
        a_ref, b_ref, res_ref, gt_ref, o_ref = refs[:5]
        rest = refs[5:]
    else:
        a_ref, b_ref, o_ref = refs[:3]
        rest = refs[3:]

    def finish(acc):
        if epi == "sigmoid":
            acc = _sigmoid(acc)
        elif epi == "residual":
            acc = res_ref[...] + gt_ref[...] * acc
        o_ref[...] = acc.astype(o_ref.dtype)

    part = jnp.dot(a_ref[...], b_ref[...], preferred_element_type=F32)
    if nk == 1:
        finish(part)
        return
    acc_ref, = rest
    k = pl.program_id(2)

    @pl.when(k == 0)
    def _():
        acc_ref[...] = part

    @pl.when(k > 0)
    def _():
        acc_ref[...] += part

    @pl.when(k == nk - 1)
    def _():
        finish(acc_ref[...])


def _matmul(a, b, *, out_dtype, tm, tn, tk=None, epi="none", res=None, gate=None, name="matmul"):
    m, kd = a.shape
    n = b.shape[1]
    tm = min(tm, m)
    tn = min(tn, n)
    tk = kd if tk is None else tk
    nk = kd // tk
    assert m % tm == 0 and n % tn == 0 and kd % tk == 0
    in_specs = [pl.BlockSpec((tm, tk), lambda i, j, k: (i, k)),
                pl.BlockSpec((tk, tn), lambda i, j, k: (k, j))]
    args = [a, b]
    if epi == "residual":
        in_specs += [pl.BlockSpec((tm, tn), lambda i, j, k: (i, j)),
                     pl.BlockSpec((1, tn), lambda i, j, k: (0, j))]
        args += [res, gate]
    scratch = [pltpu.VMEM((tm, tn), F32)] if nk > 1 else []
    return pl.pallas_call(
        functools.partial(_mm_body, nk=nk, epi=epi),
        grid=(m // tm, n // tn, nk),
        in_specs=in_specs,
        out_specs=pl.BlockSpec((tm, tn), lambda i, j, k: (i, j)),
        out_shape=jax.ShapeDtypeStruct((m, n), out_dtype),
        scratch_shapes=scratch,
        compiler_params=_cparams(("arbitrary", "arbitrary", "arbitrary")),
        name=name,
    )(*args)


def _rw_prep_body(*refs, has_vres):
    (p_ref, pprev_ref, mu_ref, w0_ref, w2_ref, a0_ref, a2_ref, g2_ref, kk_ref, ka_ref,
     bd_ref) = refs[:11]
    pos = 11
    if has_vres:
        vf_ref, v0_ref, v1_ref, v2_ref = refs[pos:pos + 4]
        pos += 4
    r_out, w_out, k_out, v_out, a_out, b_out, g_out = refs[pos:pos + 7]

    i = pl.program_id(0)
    p = p_ref[...]
    tm = p.shape[0]
    last = pprev_ref[7:8, :] * (i > 0).astype(F32)
    rowid = lax.broadcasted_iota(jnp.int32, (tm, 1), 0)
    prev = jnp.where(rowid == 0, last, pltpu.roll(p, 1, axis=0))
    pm = p + (prev - p) * mu_ref[...]

    o = 3 * RW_DIM
    r = pm[:, 0:RW_DIM]
    k = pm[:, RW_DIM:2 * RW_DIM]
    v = pm[:, 2 * RW_DIM:o]
    xw = pm[:, o:o + LORA_PAD]
    xa = pm[:, o + LORA_PAD:o + 2 * LORA_PAD]
    xg = pm[:, o + 2 * LORA_PAD:]

    if has_vres:
        mix = _sigmoid(v0_ref[...] + _bdot(_bdot(v, v1_ref[...]), v2_ref[...]))
        v = v + (vf_ref[...] - v) * mix

    w_log = -_softplus(-(w0_ref[...] + _bdot(jnp.tanh(xw), w2_ref[...]))) - 0.5
    decay = jnp.exp(-jnp.exp(w_log))
    a = _sigmoid(a0_ref[...] + _bdot(xa, a2_ref[...]))
    g = _bdot(_sigmoid(xg), g2_ref[...])

    kk = k * kk_ref[...]
    nrm = jnp.sqrt(_group_sum(kk * kk, bd_ref[...]))
    kk = kk / jnp.maximum(nrm, 1e-12)
    kh = k * (1.0 + (a - 1.0) * ka_ref[...])

    r_out[...] = r
    w_out[...] = decay
    k_out[...] = kh
    v_out[...] = v
    a_out[...] = -kk
    b_out[...] = kk * a
    g_out[...] = g


def _rw_prep(p_rw, mu, w0, w2, a0, a2, g2, k_k, k_a, ones_bd, vres):
    t = p_rw.shape[0]
    tm = min(256, t)
    has_vres = vres is not None
    row = lambda n: pl.BlockSpec((1, n), lambda i: (0, 0))
    full = lambda a: pl.BlockSpec(a.shape, lambda i: (0, 0))
    blk = pl.BlockSpec((tm, RW_DIM), lambda i: (i, 0))
    in_specs = [pl.BlockSpec((tm, N_RW_PAD), lambda i: (i, 0)),
                pl.BlockSpec((8, N_RW_PAD), lambda i: (jnp.maximum(i * (tm // 8) - 1, 0), 0)),
                row(N_RW_PAD), row(RW_DIM), full(w2), row(RW_DIM), full(a2), full(g2),
                row(RW_DIM), row(RW_DIM), full(ones_bd)]
    args = [p_rw, p_rw, mu, w0, w2, a0, a2, g2, k_k, k_a, ones_bd]
    if has_vres:
        v_first, v0, v1, v2 = vres
        in_specs += [blk, row(RW_DIM), full(v1), full(v2)]
        args += [v_first, v0, v1, v2]
    return pl.pallas_call(
        functools.partial(_rw_prep_body, has_vres=has_vres),
        grid=(t // tm,),
        in_specs=in_specs,
        out_specs=[blk] * 7,
        out_shape=[jax.ShapeDtypeStruct((t, RW_DIM), F32)] * 7,
        compiler_params=_cparams(("arbitrary",)),
        name="rwkv_prep",
    )(*args)


SCAN_T = 128
HEAD_PAIRS = RW_HEADS // 2


def _rw_scan_body(r_ref, w_ref, k_ref, v_ref, a_ref, b_ref, y_ref, s_ref, vt_ref, yt_ref):
    @pl.when(pl.program_id(0) == 0)
    def _():
        s_ref[...] = jnp.zeros_like(s_ref)

    for p in range(HEAD_PAIRS):
        sl = slice(p * LANES, (p + 1) * LANES)
        vt_ref[sl, :] = v_ref[:, sl].T

    lane = lax.broadcasted_iota(jnp.int32, (RW_HEAD, LANES), 1)
    lo = lane < RW_HEAD
    lane2 = lax.broadcasted_iota(jnp.int32, (LANES, SCAN_T), 1)

    def half_sums(x):
        s_lo = jnp.sum(jnp.where(lo, x, 0.0), axis=-1, keepdims=True)
        s_hi = jnp.sum(jnp.where(lo, 0.0, x), axis=-1, keepdims=True)
        return s_lo, s_hi

    def step8(tb, carry):
        t0 = pl.multiple_of(tb * SUBLANES, SUBLANES)
        for p in range(HEAD_PAIRS):
            sl = pl.ds(p * LANES, LANES)
            tile = lambda ref: ref[pl.ds(t0, SUBLANES), sl]
            a8, w8, b8, k8, r8 = tile(a_ref), tile(w_ref), tile(b_ref), tile(k_ref), tile(r_ref)
            s = s_ref[p]
            for j in range(SUBLANES):
                t = t0 + j
                row = lambda x8: x8[j:j + 1, :]
                vcol = jnp.sum(jnp.where(lane2 == t, vt_ref[sl, :], 0.0), axis=-1, keepdims=True)
                vb = jnp.where(lo, vcol[:RW_HEAD], vcol[RW_HEAD:])
                sa_lo, sa_hi = half_sums(s * row(a8))
                sa = jnp.where(lo, sa_lo, sa_hi)
                s = s * row(w8) + sa * row(b8) + vb * row(k8)
                y_lo, y_hi = half_sums(s * row(r8))
                e1 = lane == t
                r0 = p * LANES
                yt_ref[pl.ds(r0, RW_HEAD), :] = jnp.where(e1, y_lo, yt_ref[pl.ds(r0, RW_HEAD), :])
                yt_ref[pl.ds(r0 + RW_HEAD, RW_HEAD), :] = jnp.where(
                    e1, y_hi, yt_ref[pl.ds(r0 + RW_HEAD, RW_HEAD), :])
            s_ref[p] = s
        return carry

    lax.fori_loop(0, SCAN_T // SUBLANES, step8, 0)

    for p in range(HEAD_PAIRS):
        sl = slice(p * LANES, (p + 1) * LANES)
        y_ref[:, sl] = yt_ref[sl, :].T


def _rw_scan(r, w, k, v, a, b):
    t = r.shape[0]
    blk = pl.BlockSpec((SCAN_T, RW_DIM), lambda i: (i, 0))
    return pl.pallas_call(
        _rw_scan_body,
        grid=(t // SCAN_T,),
        in_specs=[blk] * 6,
        out_specs=blk,
        out_shape=jax.ShapeDtypeStruct((t, RW_DIM), F32),
        scratch_shapes=[pltpu.VMEM((HEAD_PAIRS, RW_HEAD, LANES), F32),
                        pltpu.VMEM((RW_DIM, SCAN_T), F32),
                        pltpu.VMEM((RW_DIM, SCAN_T), F32)],
        compiler_params=_cparams(("arbitrary",)),
        name="rwkv_scan",
    )(r, w, k, v, a, b)


def _rw_post_body(y_ref, r_ref, k_ref, v_ref, g_ref, lw_ref, lb_ref, rk_ref, bd_ref, o_ref):
    bd = bd_ref[...]
    y = y_ref[...]
    inv_n = 1.0 / RW_HEAD
    d = y - _group_sum(y, bd) * inv_n
    var = _group_sum(d * d, bd) * inv_n
    yn = d * lax.rsqrt(var + GN_EPS) * lw_ref[...] + lb_ref[...]
    bonus = _group_sum(r_ref[...] * k_ref[...] * rk_ref[...], bd) * v_ref[...]
    o_ref[...] = ((yn + bonus) * g_ref[...]).astype(o_ref.dtype)


def _rw_post(y, r, k, v, g, lnx_w, lnx_b, r_k, ones_bd):
    t = y.shape[0]
    tm = min(256, t)
    blk = pl.BlockSpec((tm, RW_DIM), lambda i: (i, 0))
    row = pl.BlockSpec((1, RW_DIM), lambda i: (0, 0))
    return pl.pallas_call(
        _rw_post_body,
        grid=(t // tm,),
        in_specs=[blk] * 5 + [row] * 3 + [pl.BlockSpec(ones_bd.shape, lambda i: (0, 0))],
        out_specs=blk,
        out_shape=jax.ShapeDtypeStruct((t, RW_DIM), BF16),
        compiler_params=_cparams(("arbitrary",)),
        name="rwkv_post",
    )(y, r, k, v, g, lnx_w, lnx_b, r_k, ones_bd)


def _rms(x, n):
    return x * lax.rsqrt(jnp.sum(x * x, axis=-1, keepdims=True) * (1.0 / n) + RMS_EPS)


def _mla_prep_body(p_ref, cos_ref, sin_ref, qn_ref, kvn_ref, wq_ref, wk_ref, wv_ref,
                   qnn_ref, qnr_ref, knn_ref, knr_ref, q_out, k_out, v_out):
    p = p_ref[...]
    cos = cos_ref[...]
    sin = sin_ref[...]
    scale = (QK_NOPE + QK_ROPE) ** -0.5

    def rope(x):
        return x * cos + pltpu.roll(x, ROPE_PAD // 2, axis=1) * sin

    cq = _rms(p[:, :Q_LORA], Q_LORA) * qn_ref[...]
    ckv = _rms(p[:, Q_LORA:Q_LORA + KV_LORA], KV_LORA) * kvn_ref[...]
    qf = _bdot(cq, wq_ref[...])
    kn = _bdot(ckv, wk_ref[...])
    v_out[...] = _bdot(ckv, wv_ref[...]).astype(v_out.dtype)
    kr = rope(_rms(p[:, Q_LORA + KV_LORA:], QK_ROPE) * knr_ref[...]).astype(k_out.dtype)
    for h in range(MLA_HEADS):
        o = h * QK_PAD
        qn = _rms(qf[:, o:o + QK_NOPE], QK_NOPE) * qnn_ref[...]
        qr = rope(_rms(qf[:, o + QK_NOPE:o + QK_PAD], QK_ROPE) * qnr_ref[...])
        q_out[:, o:o + QK_NOPE] = (qn * scale).astype(q_out.dtype)
        q_out[:, o + QK_NOPE:o + QK_PAD] = (qr * scale).astype(q_out.dtype)
        kh = _rms(kn[:, h * QK_NOPE:(h + 1) * QK_NOPE], QK_NOPE) * knn_ref[...]
        k_out[:, o:o + QK_NOPE] = kh.astype(k_out.dtype)
        k_out[:, o + QK_NOPE:o + QK_PAD] = kr


def _mla_prep(p_mla, cos, sin, q_norm, kv_norm, wq, wk, wv, qnn, qnr, knn, knr):
    t = p_mla.shape[0]
    tm = min(256, t)
    row = lambda a: pl.BlockSpec(a.shape, lambda i: (0, 0))
    blk = lambda n: pl.BlockSpec((tm, n), lambda i: (i, 0))
    return pl.pallas_call(
        _mla_prep_body,
        grid=(t // tm,),
        in_specs=[blk(N_MLA_PAD), blk(ROPE_PAD), blk(ROPE_PAD), row(q_norm), row(kv_norm),
                  row(wq), row(wk), row(wv), row(qnn), row(qnr), row(knn), row(knr)],
        out_specs=[blk(MLA_HEADS * QK_PAD), blk(MLA_HEADS * QK_PAD), blk(MLA_DIM)],
        out_shape=[jax.ShapeDtypeStruct((t, MLA_HEADS * QK_PAD), BF16),
                   jax.ShapeDtypeStruct((t, MLA_HEADS * QK_PAD), BF16),
                   jax.ShapeDtypeStruct((t, MLA_DIM), BF16)],
        compiler_params=_cparams(("arbitrary",)),
        name="mla_prep",
    )(p_mla, cos, sin, q_norm, kv_norm, wq, wk, wv, qnn, qnr, knn, knr)


def _attn_body(q_ref, k_ref, v_ref, o_ref, m_ref, l_ref, acc_ref):
    qi = pl.program_id(1)
    kj = pl.program_id(2)

    @pl.when(kj == 0)
    def _():
        m_ref[...] = jnp.full_like(m_ref, -jnp.inf)
        l_ref[...] = jnp.zeros_like(l_ref)
        acc_ref[...] = jnp.zeros_like(acc_ref)

    def update(masked):
        s = lax.dot_general(q_ref[...], k_ref[...], (((1,), (1,)), ((), ())),
                            preferred_element_type=F32)
        if masked:
            rows = lax.broadcasted_iota(jnp.int32, s.shape, 0)
            cols = lax.broadcasted_iota(jnp.int32, s.shape, 1)
            s = jnp.where(cols <= rows, s, -jnp.inf)
        m_old = m_ref[...]
        m_new = jnp.maximum(m_old, jnp.max(s, axis=-1, keepdims=True))
        alpha = jnp.exp(m_old - m_new)
        pexp = jnp.exp(s - m_new)
        l_ref[...] = alpha * l_ref[...] + jnp.sum(pexp, axis=-1, keepdims=True)
        acc_ref[...] = alpha * acc_ref[...] + jnp.dot(
            pexp.astype(v_ref.dtype), v_ref[...], preferred_element_type=F32)
        m_ref[...] = m_new

    @pl.when(kj < qi)
    def _():
        update(False)

    @pl.when(kj == qi)
    def _():
        update(True)
        o_ref[...] = (acc_ref[...] / l_ref[...]).astype(o_ref.dtype)


def _attention(q, k, v):
    t = q.shape[0]
    tq = min(1024, t)
    nq = t // tq
    return pl.pallas_call(
        _attn_body,
        grid=(MLA_HEADS, nq, nq),
        in_specs=[pl.BlockSpec((tq, QK_PAD), lambda h, i, j: (i, h)),
                  pl.BlockSpec((tq, QK_PAD), lambda h, i, j: (jnp.minimum(j, i), h)),
                  pl.BlockSpec((tq, V_HEAD), lambda h, i, j: (jnp.minimum(j, i), h))],
        out_specs=pl.BlockSpec((tq, V_HEAD), lambda h, i, j: (i, h)),
        out_shape=jax.ShapeDtypeStruct((t, MLA_DIM), BF16),
        scratch_shapes=[pltpu.VMEM((tq, 1), F32), pltpu.VMEM((tq, 1), F32),
                        pltpu.VMEM((tq, V_HEAD), F32)],
        compiler_params=_cparams(("arbitrary", "arbitrary", "arbitrary")),
        name="mla_attention",
    )(q, k, v)


def _sg_body(p_ref, lw_ref, lb_ref, ws_ref, bs_ref, o_ref):
    tm = p_ref.shape[0]
    u = _gelu_tanh(p_ref[:, :SG_DIM])
    gv = _gelu_tanh(p_ref[:, SG_DIM:])
    mu = jnp.mean(gv, axis=-1, keepdims=True)
    d = gv - mu
    var = jnp.mean(d * d, axis=-1, keepdims=True)
    vn = (d * lax.rsqrt(var + LN_EPS) * lw_ref[...] + lb_ref[...]).astype(BF16)
    rows = lax.broadcasted_iota(jnp.int32, (CHUNK, CHUNK), 0)
    cols = lax.broadcasted_iota(jnp.int32, (CHUNK, CHUNK), 1)
    tril = cols <= rows
    for g in range(SG_GROUPS):
        gs = slice(g * SG_GROUP_DIM, (g + 1) * SG_GROUP_DIM)
        wg = jnp.where(tril, ws_ref[g], 0.0).astype(BF16)
        for c in range(tm // CHUNK):
            cs = slice(c * CHUNK, (c + 1) * CHUNK)
            s = jnp.dot(wg, vn[cs, gs], preferred_element_type=F32) + bs_ref[:, gs]
            o_ref[cs, gs] = (u[cs, gs] * s).astype(o_ref.dtype)


def _spatial_gating(p_sg, ln_w, ln_b, ws, bs_full):
    t = p_sg.shape[0]
    tm = min(256, t)
    row = pl.BlockSpec((1, SG_DIM), lambda i: (0, 0))
    return pl.pallas_call(
        _sg_body,
        grid=(t // tm,),
        in_specs=[pl.BlockSpec((tm, 2 * SG_DIM), lambda i: (i, 0)), row, row,
                  pl.BlockSpec(ws.shape, lambda i: (0, 0, 0)),
                  pl.BlockSpec(bs_full.shape, lambda i: (0, 0))],
        out_specs=pl.BlockSpec((tm, SG_DIM), lambda i: (i, 0)),
        out_shape=jax.ShapeDtypeStruct((t, SG_DIM), BF16),
        compiler_params=_cparams(("arbitrary",)),
        name="spatial_gating",
    )(p_sg, ln_w, ln_b, ws, bs_full)


def _merge_body(ya_ref, yb_ref, yc_ref, wa_ref, wb_ref, wc_ref, ga_ref, gb_ref, gc_ref, o_ref):
    m = ga_ref[...].astype(F32) * jnp.dot(ya_ref[...], wa_ref[...], preferred_element_type=F32)
    m += gb_ref[...].astype(F32) * jnp.dot(yb_ref[...], wb_ref[...], preferred_element_type=F32)
    m += gc_ref[...].astype(F32) * jnp.dot(yc_ref[...], wc_ref[...], preferred_element_type=F32)
    o_ref[...] = m.astype(o_ref.dtype)


def _merge(ya, yb, yc, wa, wb, wc, gates):
    t = ya.shape[0]
    d = wa.shape[1]
    tm = min(512, t)
    tn = 1024
    nj = d // tn
    yblk = pl.BlockSpec((tm, ya.shape[1]), lambda i, j: (i, 0))
    wblk = pl.BlockSpec((wa.shape[0], tn), lambda i, j: (0, j))
    gblk = lambda b: pl.BlockSpec((tm, tn), lambda i, j: (i, b * nj + j))
    return pl.pallas_call(
        _merge_body,
        grid=(t // tm, nj),
        in_specs=[yblk, yblk, yblk, wblk, wblk, wblk, gblk(0), gblk(1), gblk(2)],
        out_specs=pl.BlockSpec((tm, tn), lambda i, j: (i, j)),
        out_shape=jax.ShapeDtypeStruct((t, d), BF16),
        compiler_params=_cparams(("arbitrary", "arbitrary")),
        name="branch_merge",
    )(ya, yb, yc, wa, wb, wc, gates, gates, gates)


CONV_CARRY = 8


def _ffn_up_body(h_ref, wg_ref, wv_ref, cwg_ref, cwv_ref, cbg_ref, cbv_ref, o_ref,
                 cg_ref, cv_ref):
    i = pl.program_id(1)

    @pl.when(i == 0)
    def _():
        cg_ref[...] = jnp.zeros_like(cg_ref)
        cv_ref[...] = jnp.zeros_like(cv_ref)

    h = h_ref[...]
    tm = h.shape[0]
    rowid = lax.broadcasted_iota(jnp.int32, (tm, 1), 0)

    def conv(w_ref, cw_ref, cb_ref, carry_ref):
        up = jnp.dot(h, w_ref[...], preferred_element_type=F32)
        c1 = carry_ref[CONV_CARRY - 1:CONV_CARRY, :]
        c2 = carry_ref[CONV_CARRY - 2:CONV_CARRY - 1, :]
        s1 = jnp.where(rowid == 0, c1, pltpu.roll(up, 1, axis=0))
        s2 = jnp.where(rowid == 0, c2, jnp.where(rowid == 1, c1, pltpu.roll(up, 2, axis=0)))
        carry_ref[...] = up[tm - CONV_CARRY:, :]
        return cb_ref[...] + cw_ref[0:1, :] * s2 + cw_ref[1:2, :] * s1 + cw_ref[2:3, :] * up

    gate = conv(wg_ref, cwg_ref, cbg_ref, cg_ref)
    val = conv(wv_ref, cwv_ref, cbv_ref, cv_ref)
    o_ref[...] = (gate * _sigmoid(gate) * val).astype(o_ref.dtype)


def _ffn_up(h, w_up, conv_w, conv_b):
    t, d = h.shape
    tm = min(1024, t)
    tn = 512
    nj = D_FF // tn
    return pl.pallas_call(
        _ffn_up_body,
        grid=(nj, t // tm),
        in_specs=[pl.BlockSpec((tm, d), lambda j, i: (i, 0)),
                  pl.BlockSpec((d, tn), lambda j, i: (0, j)),
                  pl.BlockSpec((d, tn), lambda j, i: (0, nj + j)),
                  pl.BlockSpec((CONV_W, tn), lambda j, i: (0, j)),
                  pl.BlockSpec((CONV_W, tn), lambda j, i: (0, nj + j)),
                  pl.BlockSpec((1, tn), lambda j, i: (0, j)),
                  pl.BlockSpec((1, tn), lambda j, i: (0, nj + j))],
        out_specs=pl.BlockSpec((tm, tn), lambda j, i: (i, j)),
        out_shape=jax.ShapeDtypeStruct((t, D_FF), BF16),
        scratch_shapes=[pltpu.VMEM((CONV_CARRY, tn), F32), pltpu.VMEM((CONV_CARRY, tn), F32)],
        compiler_params=_cparams(("arbitrary", "arbitrary")),
        name="ffn_up_conv",
    )(h, w_up, w_up, conv_w, conv_w, conv_b, conv_b)


def _pad_cols(a, n):
    return jnp.pad(a, ((0, 0), (0, n - a.shape[1])))


def _rw_cols(a):
    o = 3 * RW_DIM
    return jnp.concatenate([a[:, :o], _pad_cols(a[:, o:o + W_LORA], LORA_PAD),
                            _pad_cols(a[:, o + W_LORA:o + W_LORA + A_LORA], LORA_PAD),
                            a[:, o + W_LORA + A_LORA:N_RW]], axis=1)


def _pad_rows(a, n):
    return jnp.pad(a, ((0, n - a.shape[0]), (0, 0)))


def kernel(x, c, positions, ada_w, ada_b, norm_mix_g, norm_ffn_g, w_in, rw_mu, rw_w0, rw_w2, rw_a0, rw_a2, rw_g2, rw_kk, rw_ka, rw_rk, rw_lnx_w, rw_lnx_b, rw_v0, rw_v1, rw_v2, mla_q_norm, mla_kv_norm, mla_w_uq, mla_w_ukv, mla_qn_nope, mla_qn_rope, mla_kn_nope, mla_kn_rope, sg_ln_w, sg_ln_b, sg_ws, sg_b, w_br_a, w_br_b, w_br_c, w_out, ffn_up, ffn_conv, ffn_conv_b, ffn_down):
    b_, t, d = x.shape
    assert b_ == 1 and d == D_MODEL
    depth = w_in.shape[0]
    xs = x.reshape(t, d)

    mod = _ada_all(c, ada_w, ada_b)
    cos, sin = _rope_tables(positions, t)
    head_id = jnp.arange(RW_DIM) // RW_HEAD
    ones_bd = (head_id[:, None] == head_id[None, :]).astype(BF16)
    row = lambda a: a.reshape(1, -1)

    v_first = None
    for l in range(depth):
        sh1, sc1, gt1, sh2, sc2, gt2 = [mod[l, :, i * d:(i + 1) * d] for i in range(6)]

        h = _norm_mod(xs, row(norm_mix_g[l]), sc1, sh1)
        wl = w_in[l]
        o_dq = N_RW
        o_kr = N_RW + Q_LORA + KV_LORA
        o_sg = o_kr + QK_ROPE
        o_gt = o_sg + 2 * SG_DIM
        w_rw = _rw_cols(wl).astype(BF16)
        w_mla = jnp.concatenate([wl[:, o_dq:o_kr], _rope_pad(wl[:, o_kr:o_sg])], axis=1).astype(BF16)
        p_rw = _matmul(h, w_rw, out_dtype=F32, tm=1024, tn=512, name="proj_rw")
        p_mla = _matmul(h, w_mla, out_dtype=F32, tm=1024, tn=N_MLA_PAD, name="proj_mla")
        p_sg = _matmul(h, wl[:, o_sg:o_gt].astype(BF16), out_dtype=F32, tm=1024, tn=1024,
                       name="proj_sg")
        gates = _matmul(h, wl[:, o_gt:].astype(BF16), out_dtype=BF16, tm=1024, tn=1024,
                        epi="sigmoid", name="proj_gates")

        vres = None
        if l > 0:
            vres = (v_first, row(rw_v0[l - 1]), rw_v1[l - 1].astype(BF16),
                    rw_v2[l - 1].astype(BF16))
        r_, w_, k_, v_, a_, b2_, g_ = _rw_prep(
            p_rw, _rw_cols(row(rw_mu[l])), row(rw_w0[l]),
            _pad_rows(rw_w2[l], LORA_PAD).astype(BF16), row(rw_a0[l]),
            _pad_rows(rw_a2[l], LORA_PAD).astype(BF16), rw_g2[l].astype(BF16),
            row(rw_kk[l]), row(rw_ka[l]), ones_bd, vres)
        if l == 0:
            v_first = v_
        y_scan = _rw_scan(r_, w_, k_, v_, a_, b2_)
        y_a = _rw_post(y_scan, r_, k_, v_, g_, row(rw_lnx_w[l]), row(rw_lnx_b[l]),
                       row(rw_rk[l]), ones_bd)

        wq = mla_w_uq[l].reshape(Q_LORA, MLA_HEADS, QK_NOPE + QK_ROPE)
        wq = jnp.concatenate([wq[..., :QK_NOPE], _rope_pad(wq[..., QK_NOPE:])], axis=-1)
        wq = wq.reshape(Q_LORA, MLA_HEADS * QK_PAD).astype(BF16)
        wkv = mla_w_ukv[l].reshape(KV_LORA, MLA_HEADS, QK_NOPE + V_HEAD)
        wk = wkv[..., :QK_NOPE].reshape(KV_LORA, MLA_HEADS * QK_NOPE).astype(BF16)
        wv = wkv[..., QK_NOPE:].reshape(KV_LORA, MLA_DIM).astype(BF16)
        q, k, v = _mla_prep(p_mla, cos, sin, row(mla_q_norm[l]), row(mla_kv_norm[l]), wq, wk, wv,
                            row(mla_qn_nope[l]), row(_rope_pad(mla_qn_rope[l])),
                            row(mla_kn_nope[l]), row(_rope_pad(mla_kn_rope[l])))
        y_b = _attention(q, k, v)

        bs_full = jnp.repeat(sg_b[l].T, SG_GROUP_DIM, axis=1)
        y_c = _spatial_gating(p_sg, row(sg_ln_w[l]), row(sg_ln_b[l]), sg_ws[l], bs_full)

        merged = _merge(y_a, y_b, y_c, w_br_a[l].astype(BF16), w_br_b[l].astype(BF16),
                        w_br_c[l].astype(BF16), gates)
        xs = _matmul(merged, w_out[l].astype(BF16), out_dtype=F32, tm=1024, tn=1024,
                     epi="residual", res=xs, gate=gt1, name="out_proj")

        h = _norm_mod(xs, row(norm_ffn_g[l]), sc2, sh2)
        act = _ffn_up(h, ffn_up[l].astype(BF16), ffn_conv[l], row(ffn_conv_b[l]))
        xs = _matmul(act, ffn_down[l].astype(BF16), out_dtype=F32, tm=1024, tn=1024,
                     tk=D_FF // 2, epi="residual", res=xs, gate=gt2, name="ffn_down")

    return xs.reshape(b_, t, d)
```

```python
import functools

import jax
import jax.numpy as jnp
from jax import lax
from jax.experimental import pallas as pl
from jax.experimental.pallas import tpu as pltpu

F32 = jnp.float32
BF16 = jnp.bfloat16

D_MODEL = 2048
RW_HEAD = 64
RW_HEADS = 16
RW_DIM = RW_HEADS * RW_HEAD
W_LORA = 96
A_LORA = 96
V_LORA = 64
G_LORA = 256
GN_EPS = 64e-5
MLA_HEADS = 8
Q_LORA = 512
KV_LORA = 512
QK_NOPE = 128
QK_ROPE = 64
V_HEAD = 128
MLA_DIM = MLA_HEADS * V_HEAD
ROPE_THETA = 10000.0
CHUNK = 128
SG_GROUPS = 8
SG_GROUP_DIM = 128
SG_DIM = SG_GROUPS * SG_GROUP_DIM
D_FF = 5632
CONV_W = 3
RMS_EPS = 1e-6
LN_EPS = 1e-5

LANES = 128
SUBLANES = 8
LORA_PAD = 128
N_RW = 3 * RW_DIM + W_LORA + A_LORA + G_LORA
N_RW_PAD = 3 * RW_DIM + 2 * LORA_PAD + G_LORA
ROPE_PAD = 128
QK_PAD = QK_NOPE + ROPE_PAD
N_MLA_PAD = Q_LORA + KV_LORA + ROPE_PAD
VMEM_LIMIT = 52 * 1024 * 1024


def _cparams(sem):
    return pltpu.CompilerParams(dimension_semantics=sem, vmem_limit_bytes=VMEM_LIMIT)


def _sigmoid(x):
    return 1.0 / (1.0 + jnp.exp(-x))


def _softplus(x):
    return jnp.maximum(x, 0.0) + jnp.log(1.0 + jnp.exp(-jnp.abs(x)))


def _gelu_tanh(x):
    return 0.5 * x * (1.0 + jnp.tanh(0.7978845608028654 * (x + 0.044715 * (x * x * x))))


def _bdot(a, b):
    return jnp.dot(a.astype(BF16), b, preferred_element_type=F32)


def _group_sum(x, ones_bd):
    hi = x.astype(BF16)
    lo = (x - hi.astype(F32)).astype(BF16)
    return (jnp.dot(hi, ones_bd, preferred_element_type=F32)
            + jnp.dot(lo, ones_bd, preferred_element_type=F32))


def _ada_body(c_ref, w_ref, b_ref, o_ref, *, kc):
    d = c_ref.shape[0]
    tn = o_ref.shape[-1]

    def step(i, acc):
        ck = c_ref[pl.ds(i * kc, kc), :]
        sk = ck * _sigmoid(ck)
        wk = w_ref[0, pl.ds(i * kc, kc), :]
        return acc + jnp.sum(wk * sk, axis=0, keepdims=True)

    acc = lax.fori_loop(0, d // kc, step, jnp.zeros((1, tn), F32))
    o_ref[0] = acc + b_ref[0]


def _ada_all(c, ada_w, ada_b):
    nl, d, n = ada_w.shape
    tn = 1024
    out = pl.pallas_call(
        functools.partial(_ada_body, kc=256),
        grid=(nl, n // tn),
        in_specs=[pl.BlockSpec((d, 1), lambda l, j: (0, 0)),
                  pl.BlockSpec((1, d, tn), lambda l, j: (l, 0, j)),
                  pl.BlockSpec((1, 1, tn), lambda l, j: (l, 0, j))],
        out_specs=pl.BlockSpec((1, 1, tn), lambda l, j: (l, 0, j)),
        out_shape=jax.ShapeDtypeStruct((nl, 1, n), F32),
        compiler_params=_cparams(("arbitrary", "arbitrary")),
        name="ada_mod",
    )(c.reshape(d, 1), ada_w, ada_b.reshape(nl, 1, n))
    return out


def _rope_body(pos_ref, inv_ref, msk_ref, sgn_ref, cos_ref, sin_ref):
    ang = pos_ref[...].astype(F32) * inv_ref[...]
    cos_ref[...] = jnp.cos(ang) * msk_ref[...]
    sin_ref[...] = jnp.sin(ang) * sgn_ref[...]


def _rope_pad(v):
    h = QK_ROPE // 2
    z = jnp.zeros(v.shape[:-1] + (ROPE_PAD // 2 - h,), v.dtype)
    return jnp.concatenate([v[..., :h], z, v[..., h:], z], axis=-1)


def _rope_tables(positions, t):
    tm = min(1024, t)
    inv = ROPE_THETA ** (-jnp.arange(0, QK_ROPE, 2, dtype=F32) / QK_ROPE)
    ones = jnp.ones((QK_ROPE // 2,), F32)
    inv_p = _rope_pad(jnp.concatenate([inv, inv]))[None]
    msk_p = _rope_pad(jnp.concatenate([ones, ones]))[None]
    sgn_p = _rope_pad(jnp.concatenate([-ones, ones]))[None]
    row = pl.BlockSpec((1, ROPE_PAD), lambda i: (0, 0))
    blk = pl.BlockSpec((tm, ROPE_PAD), lambda i: (i, 0))
    return pl.pallas_call(
        _rope_body,
        grid=(t // tm,),
        in_specs=[pl.BlockSpec((tm, 1), lambda i: (i, 0)), row, row, row],
        out_specs=[blk, blk],
        out_shape=[jax.ShapeDtypeStruct((t, ROPE_PAD), F32)] * 2,
        compiler_params=_cparams(("arbitrary",)),
        name="rope_tables",
    )(positions.reshape(t, 1), inv_p, msk_p, sgn_p)


def _norm_mod_body(x_ref, g_ref, sc_ref, sh_ref, o_ref):
    x = x_ref[...]
    y = x * lax.rsqrt(jnp.mean(x * x, axis=-1, keepdims=True) + RMS_EPS) * g_ref[...]
    o_ref[...] = (y * (1.0 + sc_ref[...]) + sh_ref[...]).astype(o_ref.dtype)


def _norm_mod(x, g, sc, sh):
    t, d = x.shape
    tm = min(512, t)
    row = pl.BlockSpec((1, d), lambda i: (0, 0))
    return pl.pallas_call(
        _norm_mod_body,
        grid=(t // tm,),
        in_specs=[pl.BlockSpec((tm, d), lambda i: (i, 0)), row, row, row],
        out_specs=pl.BlockSpec((tm, d), lambda i: (i, 0)),
        out_shape=jax.ShapeDtypeStruct((t, d), BF16),
        compiler_params=_cparams(("arbitrary",)),
        name="norm_mod",
    )(x, g, sc, sh)


def _mm_body(*refs, nk, epi):
    if epi == "residual":
        a_ref, b_ref, res_ref, gt_ref, o_ref = refs[:5]
        rest = refs[5:]
    else:
        a_ref, b_ref, o_ref = refs[:3]
        rest = refs[3:]

    def finish(acc):
        if epi == "sigmoid":
            acc = _sigmoid(acc)
        elif epi == "residual":
            acc = res_ref[...] + gt_ref[...] * acc
        o_ref[...] = acc.astype(o_ref.dtype)

    part = jnp.dot(a_ref[...], b_ref[...], preferred_element_type=F32)
    if nk == 1:
        finish(part)
        return
    acc_ref, = rest
    k = pl.program_id(2)

    @pl.when(k == 0)
    def _():
        acc_ref[...] = part

    @pl.when(k > 0)
    def _():
        acc_ref[...] += part

    @pl.when(k == nk - 1)
    def _():
        finish(acc_ref[...])


def _matmul(a, b, *, out_dtype, tm, tn, tk=None, epi="none", res=None, gate=None, name="matmul"):
    m, kd = a.shape
    n = b.shape[1]
    tm = min(tm, m)
    tn = min(tn, n)
    tk = kd if tk is None else tk
    nk = kd // tk
    assert m % tm == 0 and n % tn == 0 and kd % tk == 0
    in_specs = [pl.BlockSpec((tm, tk), lambda i, j, k: (i, k)),
                pl.BlockSpec((tk, tn), lambda i, j, k: (k, j))]
    args = [a, b]
    if epi == "residual":
        in_specs += [pl.BlockSpec((tm, tn), lambda i, j, k: (i, j)),
                     pl.BlockSpec((1, tn), lambda i, j, k: (0, j))]
        args += [res, gate]
    scratch = [pltpu.VMEM((tm, tn), F32)] if nk > 1 else []
    return pl.pallas_call(
        functools.partial(_mm_body, nk=nk, epi=epi),
        grid=(m // tm, n // tn, nk),
        in_specs=in_specs,
        out_specs=pl.BlockSpec((tm, tn), lambda i, j, k: (i, j)),
        out_shape=jax.ShapeDtypeStruct((m, n), out_dtype),
        scratch_shapes=scratch,
        compiler_params=_cparams(("arbitrary", "arbitrary", "arbitrary")),
        name=name,
    )(*args)


def _rw_prep_body(*refs, has_vres):
    (p_ref, pprev_ref, mu_ref, w0_ref, w2_ref, a0_ref, a2_ref, g2_ref, kk_ref, ka_ref,
     bd_ref) = refs[:11]
    pos = 11
    if has_vres:
        vf_ref, v0_ref, v1_ref, v2_ref = refs[pos:pos + 4]
        pos += 4
    r_out, w_out, k_out, v_out, a_out, b_out, g_out = refs[pos:pos + 7]

    i = pl.program_id(0)
    p = p_ref[...]
    tm = p.shape[0]
    last = pprev_ref[7:8, :] * (i > 0).astype(F32)
    rowid = lax.broadcasted_iota(jnp.int32, (tm, 1), 0)
    prev = jnp.where(rowid == 0, last, pltpu.roll(p, 1, axis=0))
    pm = p + (prev - p) * mu_ref[...]

    o = 3 * RW_DIM
    r = pm[:, 0:RW_DIM]
    k = pm[:, RW_DIM:2 * RW_DIM]
    v = pm[:, 2 * RW_DIM:o]
    xw = pm[:, o:o + LORA_PAD]
    xa = pm[:, o + LORA_PAD:o + 2 * LORA_PAD]
    xg = pm[:, o + 2 * LORA_PAD:]

    if has_vres:
        mix = _sigmoid(v0_ref[...] + _bdot(_bdot(v, v1_ref[...]), v2_ref[...]))
        v = v + (vf_ref[...] - v) * mix

    w_log = -_softplus(-(w0_ref[...] + _bdot(jnp.tanh(xw), w2_ref[...]))) - 0.5
    decay = jnp.exp(-jnp.exp(w_log))
    a = _sigmoid(a0_ref[...] + _bdot(xa, a2_ref[...]))
    g = _bdot(_sigmoid(xg), g2_ref[...])

    kk = k * kk_ref[...]
    nrm = jnp.sqrt(_group_sum(kk * kk, bd_ref[...]))
    kk = kk / jnp.maximum(nrm, 1e-12)
    kh = k * (1.0 + (a - 1.0) * ka_ref[...])

    r_out[...] = r
    w_out[...] = decay
    k_out[...] = kh
    v_out[...] = v
    a_out[...] = -kk
    b_out[...] = kk * a
    g_out[...] = g


def _rw_prep(p_rw, mu, w0, w2, a0, a2, g2, k_k, k_a, ones_bd, vres):
    t = p_rw.shape[0]
    tm = min(256, t)
    has_vres = vres is not None
    row = lambda n: pl.BlockSpec((1, n), lambda i: (0, 0))
    full = lambda a: pl.BlockSpec(a.shape, lambda i: (0, 0))
    blk = pl.BlockSpec((tm, RW_DIM), lambda i: (i, 0))
    in_specs = [pl.BlockSpec((tm, N_RW_PAD), lambda i: (i, 0)),
                pl.BlockSpec((8, N_RW_PAD), lambda i: (jnp.maximum(i * (tm // 8) - 1, 0), 0)),
                row(N_RW_PAD), row(RW_DIM), full(w2), row(RW_DIM), full(a2), full(g2),
                row(RW_DIM), row(RW_DIM), full(ones_bd)]
    args = [p_rw, p_rw, mu, w0, w2, a0, a2, g2, k_k, k_a, ones_bd]
    if has_vres:
        v_first, v0, v1, v2 = vres
        in_specs += [blk, row(RW_DIM), full(v1), full(v2)]
        args += [v_first, v0, v1, v2]
    return pl.pallas_call(
        functools.partial(_rw_prep_body, has_vres=has_vres),
        grid=(t // tm,),
        in_specs=in_specs,
        out_specs=[blk] * 7,
        out_shape=[jax.ShapeDtypeStruct((t, RW_DIM), F32)] * 7,
        compiler_params=_cparams(("arbitrary",)),
        name="rwkv_prep",
    )(*args)


SCAN_T = 128
HEAD_PAIRS = RW_HEADS // 2


def _rw_scan_body(r_ref, w_ref, k_ref, v_ref, a_ref, b_ref, y_ref, s_ref):
    @pl.when(pl.program_id(0) == 0)
    def _():
        s_ref[...] = jnp.zeros_like(s_ref)

    lane = lax.broadcasted_iota(jnp.int32, (RW_HEAD, LANES), 1)
    sub = lax.broadcasted_iota(jnp.int32, (RW_HEAD, LANES), 0)
    lo = lane < RW_HEAD
    lo8 = lax.broadcasted_iota(jnp.int32, (SUBLANES, LANES), 1) < RW_HEAD
    diag2 = jnp.where((lane & (RW_HEAD - 1)) == sub, 1.0, 0.0)
    same_half = (lax.broadcasted_iota(jnp.int32, (LANES, LANES), 0) // RW_HEAD
                 == lax.broadcasted_iota(jnp.int32, (LANES, LANES), 1) // RW_HEAD)
    ones_half = jnp.where(same_half, 1.0, 0.0).astype(BF16)
    row16 = lax.broadcasted_iota(jnp.int32, (2 * SUBLANES, LANES), 0) & (SUBLANES - 1)

    def step8(tb, carry):
        t0 = pl.multiple_of(tb * SUBLANES, SUBLANES)
        a_lo, a_hi, w8, b8, k8, r16, vb, yacc = [], [], [], [], [], [], [], []
        for p in range(HEAD_PAIRS):
            sl = pl.ds(p * LANES, LANES)
            tile = lambda ref: ref[pl.ds(t0, SUBLANES), sl]
            a8, r8, v8 = tile(a_ref), tile(r_ref), tile(v_ref)
            a_lo.append(jnp.where(lo8, a8, 0.0))
            a_hi.append(jnp.where(lo8, 0.0, a8))
            w8.append(tile(w_ref))
            b8.append(tile(b_ref))
            k8.append(tile(k_ref))
            r16.append(jnp.concatenate([jnp.where(lo8, r8, 0.0), jnp.where(lo8, 0.0, r8)], axis=0))
            v_diag = jnp.concatenate([diag2 * v8[j:j + 1, :] for j in range(SUBLANES)], axis=0)
            vb.append(jnp.dot(v_diag.astype(BF16), ones_half, preferred_element_type=F32))
            yacc.append(jnp.zeros((2 * SUBLANES, RW_HEAD), F32))
        for j in range(SUBLANES):
            row = lambda x8: x8[j:j + 1, :]
            for p in range(HEAD_PAIRS):
                s = s_ref[p]
                sa_lo = jnp.sum(s * row(a_lo[p]), axis=-1, keepdims=True)
                sa_hi = jnp.sum(s * row(a_hi[p]), axis=-1, keepdims=True)
                sa = jnp.where(lo, sa_lo, sa_hi)
                s = (s * row(w8[p]) + sa * row(b8[p])
                     + vb[p][j * RW_HEAD:(j + 1) * RW_HEAD, :] * row(k8[p]))
                s_ref[p] = s
                rj = jnp.where(row16 == j, r16[p], 0.0).astype(BF16)
                yacc[p] = yacc[p] + lax.dot_general(
                    rj, s.astype(BF16), (((1,), (1,)), ((), ())), preferred_element_type=F32)
        for p in range(HEAD_PAIRS):
            y_ref[pl.ds(t0, SUBLANES), pl.ds(p * LANES, LANES)] = jnp.concatenate(
                [yacc[p][:SUBLANES], yacc[p][SUBLANES:]], axis=1)
        return carry

    lax.fori_loop(0, SCAN_T // SUBLANES, step8, 0)


def _rw_scan(r, w, k, v, a, b):
    t = r.shape[0]
    blk = pl.BlockSpec((SCAN_T, RW_DIM), lambda i: (i, 0))
    return pl.pallas_call(
        _rw_scan_body,
        grid=(t // SCAN_T,),
        in_specs=[blk] * 6,
        out_specs=blk,
        out_shape=jax.ShapeDtypeStruct((t, RW_DIM), F32),
        scratch_shapes=[pltpu.VMEM((HEAD_PAIRS, RW_HEAD, LANES), F32)],
        compiler_params=_cparams(("arbitrary",)),
        name="rwkv_scan",
    )(r, w, k, v, a, b)


def _rw_post_body(y_ref, r_ref, k_ref, v_ref, g_ref, lw_ref, lb_ref, rk_ref, bd_ref, o_ref):
    bd = bd_ref[...]
    y = y_ref[...]
    inv_n = 1.0 / RW_HEAD
    d = y - _group_sum(y, bd) * inv_n
    var = _group_sum(d * d, bd) * inv_n
    yn = d * lax.rsqrt(var + GN_EPS) * lw_ref[...] + lb_ref[...]
    bonus = _group_sum(r_ref[...] * k_ref[...] * rk_ref[...], bd) * v_ref[...]
    o_ref[...] = ((yn + bonus) * g_ref[...]).astype(o_ref.dtype)


def _rw_post(y, r, k, v, g, lnx_w, lnx_b, r_k, ones_bd):
    t = y.shape[0]
    tm = min(256, t)
    blk = pl.BlockSpec((tm, RW_DIM), lambda i: (i, 0))
    row = pl.BlockSpec((1, RW_DIM), lambda i: (0, 0))
    return pl.pallas_call(
        _rw_post_body,
        grid=(t // tm,),
        in_specs=[blk] * 5 + [row] * 3 + [pl.BlockSpec(ones_bd.shape, lambda i: (0, 0))],
        out_specs=blk,
        out_shape=jax.ShapeDtypeStruct((t, RW_DIM), BF16),
        compiler_params=_cparams(("arbitrary",)),
        name="rwkv_post",
    )(y, r, k, v, g, lnx_w, lnx_b, r_k, ones_bd)


def _rms(x, n):
    return x * lax.rsqrt(jnp.sum(x * x, axis=-1, keepdims=True) * (1.0 / n) + RMS_EPS)


def _mla_prep_body(p_ref, cos_ref, sin_ref, qn_ref, kvn_ref, wq_ref, wk_ref, wv_ref,
                   qnn_ref, qnr_ref, knn_ref, knr_ref, q_out, k_out, v_out):
    p = p_ref[...]
    cos = cos_ref[...]
    sin = sin_ref[...]
    scale = (QK_NOPE + QK_ROPE) ** -0.5

    def rope(x):
        return x * cos + pltpu.roll(x, ROPE_PAD // 2, axis=1) * sin

    cq = _rms(p[:, :Q_LORA], Q_LORA) * qn_ref[...]
    ckv = _rms(p[:, Q_LORA:Q_LORA + KV_LORA], KV_LORA) * kvn_ref[...]
    qf = _bdot(cq, wq_ref[...])
    kn = _bdot(ckv, wk_ref[...])
    v_out[...] = _bdot(ckv, wv_ref[...]).astype(v_out.dtype)
    kr = rope(_rms(p[:, Q_LORA + KV_LORA:], QK_ROPE) * knr_ref[...]).astype(k_out.dtype)
    for h in range(MLA_HEADS):
        o = h * QK_PAD
        qn = _rms(qf[:, o:o + QK_NOPE], QK_NOPE) * qnn_ref[...]
        qr = rope(_rms(qf[:, o + QK_NOPE:o + QK_PAD], QK_ROPE) * qnr_ref[...])
        q_out[:, o:o + QK_NOPE] = (qn * scale).astype(q_out.dtype)
        q_out[:, o + QK_NOPE:o + QK_PAD] = (qr * scale).astype(q_out.dtype)
        kh = _rms(kn[:, h * QK_NOPE:(h + 1) * QK_NOPE], QK_NOPE) * knn_ref[...]
        k_out[:, o:o + QK_NOPE] = kh.astype(k_out.dtype)
        k_out[:, o + QK_NOPE:o + QK_PAD] = kr


def _mla_prep(p_mla, cos, sin, q_norm, kv_norm, wq, wk, wv, qnn, qnr, knn, knr):
    t = p_mla.shape[0]
    tm = min(256, t)
    row = lambda a: pl.BlockSpec(a.shape, lambda i: (0, 0))
    blk = lambda n: pl.BlockSpec((tm, n), lambda i: (i, 0))
    return pl.pallas_call(
        _mla_prep_body,
        grid=(t // tm,),
        in_specs=[blk(N_MLA_PAD), blk(ROPE_PAD), blk(ROPE_PAD), row(q_norm), row(kv_norm),
                  row(wq), row(wk), row(wv), row(qnn), row(qnr), row(knn), row(knr)],
        out_specs=[blk(MLA_HEADS * QK_PAD), blk(MLA_HEADS * QK_PAD), blk(MLA_DIM)],
        out_shape=[jax.ShapeDtypeStruct((t, MLA_HEADS * QK_PAD), BF16),
                   jax.ShapeDtypeStruct((t, MLA_HEADS * QK_PAD), BF16),
                   jax.ShapeDtypeStruct((t, MLA_DIM), BF16)],
        compiler_params=_cparams(("arbitrary",)),
        name="mla_prep",
    )(p_mla, cos, sin, q_norm, kv_norm, wq, wk, wv, qnn, qnr, knn, knr)


def _attn_body(q_ref, k_ref, v_ref, o_ref, m_ref, l_ref, acc_ref):
    qi = pl.program_id(1)
    kj = pl.program_id(2)

    @pl.when(kj == 0)
    def _():
        m_ref[...] = jnp.full_like(m_ref, -jnp.inf)
        l_ref[...] = jnp.zeros_like(l_ref)
        acc_ref[...] = jnp.zeros_like(acc_ref)

    def update(masked):
        s = lax.dot_general(q_ref[...], k_ref[...], (((1,), (1,)), ((), ())),
                            preferred_element_type=F32)
        if masked:
            rows = lax.broadcasted_iota(jnp.int32, s.shape, 0)
            cols = lax.broadcasted_iota(jnp.int32, s.shape, 1)
            s = jnp.where(cols <= rows, s, -jnp.inf)
        m_old = m_ref[...]
        m_new = jnp.maximum(m_old, jnp.max(s, axis=-1, keepdims=True))
        alpha = jnp.exp(m_old - m_new)
        pexp = jnp.exp(s - m_new)
        l_ref[...] = alpha * l_ref[...] + jnp.sum(pexp, axis=-1, keepdims=True)
        acc_ref[...] = alpha * acc_ref[...] + jnp.dot(
            pexp.astype(v_ref.dtype), v_ref[...], preferred_element_type=F32)
        m_ref[...] = m_new

    @pl.when(kj < qi)
    def _():
        update(False)

    @pl.when(kj == qi)
    def _():
        update(True)
        o_ref[...] = (acc_ref[...] / l_ref[...]).astype(o_ref.dtype)


def _attention(q, k, v):
    t = q.shape[0]
    tq = min(1024, t)
    nq = t // tq
    return pl.pallas_call(
        _attn_body,
        grid=(MLA_HEADS, nq, nq),
        in_specs=[pl.BlockSpec((tq, QK_PAD), lambda h, i, j: (i, h)),
                  pl.BlockSpec((tq, QK_PAD), lambda h, i, j: (jnp.minimum(j, i), h)),
                  pl.BlockSpec((tq, V_HEAD), lambda h, i, j: (jnp.minimum(j, i), h))],
        out_specs=pl.BlockSpec((tq, V_HEAD), lambda h, i, j: (i, h)),
        out_shape=jax.ShapeDtypeStruct((t, MLA_DIM), BF16),
        scratch_shapes=[pltpu.VMEM((tq, 1), F32), pltpu.VMEM((tq, 1), F32),
                        pltpu.VMEM((tq, V_HEAD), F32)],
        compiler_params=_cparams(("arbitrary", "arbitrary", "arbitrary")),
        name="mla_attention",
    )(q, k, v)


def _sg_body(p_ref, lw_ref, lb_ref, ws_ref, bs_ref, o_ref):
    tm = p_ref.shape[0]
    u = _gelu_tanh(p_ref[:, :SG_DIM])
    gv = _gelu_tanh(p_ref[:, SG_DIM:])
    mu = jnp.mean(gv, axis=-1, keepdims=True)
    d = gv - mu
    var = jnp.mean(d * d, axis=-1, keepdims=True)
    vn = (d * lax.rsqrt(var + LN_EPS) * lw_ref[...] + lb_ref[...]).astype(BF16)
    rows = lax.broadcasted_iota(jnp.int32, (CHUNK, CHUNK), 0)
    cols = lax.broadcasted_iota(jnp.int32, (CHUNK, CHUNK), 1)
    tril = cols <= rows
    for g in range(SG_GROUPS):
        gs = slice(g * SG_GROUP_DIM, (g + 1) * SG_GROUP_DIM)
        wg = jnp.where(tril, ws_ref[g], 0.0).astype(BF16)
        for c in range(tm // CHUNK):
            cs = slice(c * CHUNK, (c + 1) * CHUNK)
            s = jnp.dot(wg, vn[cs, gs], preferred_element_type=F32) + bs_ref[:, gs]
            o_ref[cs, gs] = (u[cs, gs] * s).astype(o_ref.dtype)


def _spatial_gating(p_sg, ln_w, ln_b, ws, bs_full):
    t = p_sg.shape[0]
    tm = min(256, t)
    row = pl.BlockSpec((1, SG_DIM), lambda i: (0, 0))
    return pl.pallas_call(
        _sg_body,
        grid=(t // tm,),
        in_specs=[pl.BlockSpec((tm, 2 * SG_DIM), lambda i: (i, 0)), row, row,
                  pl.BlockSpec(ws.shape, lambda i: (0, 0, 0)),
                  pl.BlockSpec(bs_full.shape, lambda i: (0, 0))],
        out_specs=pl.BlockSpec((tm, SG_DIM), lambda i: (i, 0)),
        out_shape=jax.ShapeDtypeStruct((t, SG_DIM), BF16),
        compiler_params=_cparams(("arbitrary",)),
        name="spatial_gating",
    )(p_sg, ln_w, ln_b, ws, bs_full)


def _merge_body(ya_ref, yb_ref, yc_ref, wa_ref, wb_ref, wc_ref, ga_ref, gb_ref, gc_ref, o_ref):
    m = ga_ref[...].astype(F32) * jnp.dot(ya_ref[...], wa_ref[...], preferred_element_type=F32)
    m += gb_ref[...].astype(F32) * jnp.dot(yb_ref[...], wb_ref[...], preferred_element_type=F32)
    m += gc_ref[...].astype(F32) * jnp.dot(yc_ref[...], wc_ref[...], preferred_element_type=F32)
    o_ref[...] = m.astype(o_ref.dtype)


def _merge(ya, yb, yc, wa, wb, wc, gates):
    t = ya.shape[0]
    d = wa.shape[1]
    tm = min(512, t)
    tn = 1024
    nj = d // tn
    yblk = pl.BlockSpec((tm, ya.shape[1]), lambda i, j: (i, 0))
    wblk = pl.BlockSpec((wa.shape[0], tn), lambda i, j: (0, j))
    gblk = lambda b: pl.BlockSpec((tm, tn), lambda i, j: (i, b * nj + j))
    return pl.pallas_call(
        _merge_body,
        grid=(t // tm, nj),
        in_specs=[yblk, yblk, yblk, wblk, wblk, wblk, gblk(0), gblk(1), gblk(2)],
        out_specs=pl.BlockSpec((tm, tn), lambda i, j: (i, j)),
        out_shape=jax.ShapeDtypeStruct((t, d), BF16),
        compiler_params=_cparams(("arbitrary", "arbitrary")),
        name="branch_merge",
    )(ya, yb, yc, wa, wb, wc, gates, gates, gates)


CONV_CARRY = 8


def _ffn_up_body(h_ref, wg_ref, wv_ref, cwg_ref, cwv_ref, cbg_ref, cbv_ref, o_ref,
                 cg_ref, cv_ref):
    i = pl.program_id(1)

    @pl.when(i == 0)
    def _():
        cg_ref[...] = jnp.zeros_like(cg_ref)
        cv_ref[...] = jnp.zeros_like(cv_ref)

    h = h_ref[...]
    tm = h.shape[0]
    rowid = lax.broadcasted_iota(jnp.int32, (tm, 1), 0)

    def conv(w_ref, cw_ref, cb_ref, carry_ref):
        up = jnp.dot(h, w_ref[...], preferred_element_type=F32)
        c1 = carry_ref[CONV_CARRY - 1:CONV_CARRY, :]
        c2 = carry_ref[CONV_CARRY - 2:CONV_CARRY - 1, :]
        s1 = jnp.where(rowid == 0, c1, pltpu.roll(up, 1, axis=0))
        s2 = jnp.where(rowid == 0, c2, jnp.where(rowid == 1, c1, pltpu.roll(up, 2, axis=0)))
        carry_ref[...] = up[tm - CONV_CARRY:, :]
        return cb_ref[...] + cw_ref[0:1, :] * s2 + cw_ref[1:2, :] * s1 + cw_ref[2:3, :] * up

    gate = conv(wg_ref, cwg_ref, cbg_ref, cg_ref)
    val = conv(wv_ref, cwv_ref, cbv_ref, cv_ref)
    o_ref[...] = (gate * _sigmoid(gate) * val).astype(o_ref.dtype)


def _ffn_up(h, w_up, conv_w, conv_b):
    t, d = h.shape
    tm = min(1024, t)
    tn = 512
    nj = D_FF // tn
    return pl.pallas_call(
        _ffn_up_body,
        grid=(nj, t // tm),
        in_specs=[pl.BlockSpec((tm, d), lambda j, i: (i, 0)),
                  pl.BlockSpec((d, tn), lambda j, i: (0, j)),
                  pl.BlockSpec((d, tn), lambda j, i: (0, nj + j)),
                  pl.BlockSpec((CONV_W, tn), lambda j, i: (0, j)),
                  pl.BlockSpec((CONV_W, tn), lambda j, i: (0, nj + j)),
                  pl.BlockSpec((1, tn), lambda j, i: (0, j)),
                  pl.BlockSpec((1, tn), lambda j, i: (0, nj + j))],
        out_specs=pl.BlockSpec((tm, tn), lambda j, i: (i, j)),
        out_shape=jax.ShapeDtypeStruct((t, D_FF), BF16),
        scratch_shapes=[pltpu.VMEM((CONV_CARRY, tn), F32), pltpu.VMEM((CONV_CARRY, tn), F32)],
        compiler_params=_cparams(("arbitrary", "arbitrary")),
        name="ffn_up_conv",
    )(h, w_up, w_up, conv_w, conv_w, conv_b, conv_b)


def _pad_cols(a, n):
    return jnp.pad(a, ((0, 0), (0, n - a.shape[1])))


def _rw_cols(a):
    o = 3 * RW_DIM
    return jnp.concatenate([a[:, :o], _pad_cols(a[:, o:o + W_LORA], LORA_PAD),
                            _pad_cols(a[:, o + W_LORA:o + W_LORA + A_LORA], LORA_PAD),
                            a[:, o + W_LORA + A_LORA:N_RW]], axis=1)


def _pad_rows(a, n):
    return jnp.pad(a, ((0, n - a.shape[0]), (0, 0)))


def kernel(x, c, positions, ada_w, ada_b, norm_mix_g, norm_ffn_g, w_in, rw_mu, rw_w0, rw_w2, rw_a0, rw_a2, rw_g2, rw_kk, rw_ka, rw_rk, rw_lnx_w, rw_lnx_b, rw_v0, rw_v1, rw_v2, mla_q_norm, mla_kv_norm, mla_w_uq, mla_w_ukv, mla_qn_nope, mla_qn_rope, mla_kn_nope, mla_kn_rope, sg_ln_w, sg_ln_b, sg_ws, sg_b, w_br_a, w_br_b, w_br_c, w_out, ffn_up, ffn_conv, ffn_conv_b, ffn_down):
    b_, t, d = x.shape
    assert b_ == 1 and d == D_MODEL
    depth = w_in.shape[0]
    xs = x.reshape(t, d)

    mod = _ada_all(c, ada_w, ada_b)
    cos, sin = _rope_tables(positions, t)
    head_id = jnp.arange(RW_DIM) // RW_HEAD
    ones_bd = (head_id[:, None] == head_id[None, :]).astype(BF16)
    row = lambda a: a.reshape(1, -1)

    v_first = None
    for l in range(depth):
        sh1, sc1, gt1, sh2, sc2, gt2 = [mod[l, :, i * d:(i + 1) * d] for i in range(6)]

        h = _norm_mod(xs, row(norm_mix_g[l]), sc1, sh1)
        wl = w_in[l]
        o_dq = N_RW
        o_kr = N_RW + Q_LORA + KV_LORA
        o_sg = o_kr + QK_ROPE
        o_gt = o_sg + 2 * SG_DIM
        w_rw = _rw_cols(wl).astype(BF16)
        w_mla = jnp.concatenate([wl[:, o_dq:o_kr], _rope_pad(wl[:, o_kr:o_sg])], axis=1).astype(BF16)
        p_rw = _matmul(h, w_rw, out_dtype=F32, tm=1024, tn=512, name="proj_rw")
        p_mla = _matmul(h, w_mla, out_dtype=F32, tm=1024, tn=N_MLA_PAD, name="proj_mla")
        p_sg = _matmul(h, wl[:, o_sg:o_gt].astype(BF16), out_dtype=F32, tm=1024, tn=1024,
                       name="proj_sg")
        gates = _matmul(h, wl[:, o_gt:].astype(BF16), out_dtype=BF16, tm=1024, tn=1024,
                        epi="sigmoid", name="proj_gates")

        vres = None
        if l > 0:
            vres = (v_first, row(rw_v0[l - 1]), rw_v1[l - 1].astype(BF16),
                    rw_v2[l - 1].astype(BF16))
        r_, w_, k_, v_, a_, b2_, g_ = _rw_prep(
            p_rw, _rw_cols(row(rw_mu[l])), row(rw_w0[l]),
            _pad_rows(rw_w2[l], LORA_PAD).astype(BF16), row(rw_a0[l]),
            _pad_rows(rw_a2[l], LORA_PAD).astype(BF16), rw_g2[l].astype(BF16),
            row(rw_kk[l]), row(rw_ka[l]), ones_bd, vres)
        if l == 0:
            v_first = v_
        y_scan = _rw_scan(r_, w_, k_, v_, a_, b2_)
        y_a = _rw_post(y_scan, r_, k_, v_, g_, row(rw_lnx_w[l]), row(rw_lnx_b[l]),
                       row(rw_rk[l]), ones_bd)

        wq = mla_w_uq[l].reshape(Q_LORA, MLA_HEADS, QK_NOPE + QK_ROPE)
        wq = jnp.concatenate([wq[..., :QK_NOPE], _rope_pad(wq[..., QK_NOPE:])], axis=-1)
        wq = wq.reshape(Q_LORA, MLA_HEADS * QK_PAD).astype(BF16)
        wkv = mla_w_ukv[l].reshape(KV_LORA, MLA_HEADS, QK_NOPE + V_HEAD)
        wk = wkv[..., :QK_NOPE].reshape(KV_LORA, MLA_HEADS * QK_NOPE).astype(BF16)
        wv = wkv[..., QK_NOPE:].reshape(KV_LORA, MLA_DIM).astype(BF16)
        q, k, v = _mla_prep(p_mla, cos, sin, row(mla_q_norm[l]), row(mla_kv_norm[l]), wq, wk, wv,
                            row(mla_qn_nope[l]), row(_rope_pad(mla_qn_rope[l])),
                            row(mla_kn_nope[l]), row(_rope_pad(mla_kn_rope[l])))
        y_b = _attention(q, k, v)

        bs_full = jnp.repeat(sg_b[l].T, SG_GROUP_DIM, axis=1)
        y_c = _spatial_gating(p_sg, row(sg_ln_w[l]), row(sg_ln_b[l]), sg_ws[l], bs_full)

        merged = _merge(y_a, y_b, y_c, w_br_a[l].astype(BF16), w_br_b[l].astype(BF16),
                        w_br_c[l].astype(BF16), gates)
        xs = _matmul(merged, w_out[l].astype(BF16), out_dtype=F32, tm=1024, tn=1024,
                     epi="residual", res=xs, gate=gt1, name="out_proj")

        h = _norm_mod(xs, row(norm_ffn_g[l]), sc2, sh2)
        act = _ffn_up(h, ffn_up[l].astype(BF16), ffn_conv[l], row(ffn_conv_b[l]))
        xs = _matmul(act, ffn_down[l].astype(BF16), out_dtype=F32, tm=1024, tn=1024,
                     tk=D_FF // 2, epi="residual", res=xs, gate=gt2, name="ffn_down")

    return xs.reshape(b_, t, d)
```

```python
import functools

import jax
import jax.numpy as jnp
from jax import lax
from jax.experimental import pallas as pl
from jax.experimental.pallas import tpu as pltpu

F32 = jnp.float32
BF16 = jnp.bfloat16

D_MODEL = 2048
RW_HEAD = 64
RW_HEADS = 16
RW_DIM = RW_HEADS * RW_HEAD
W_LORA = 96
A_LORA = 96
V_LORA = 64
G_LORA = 256
GN_EPS = 64e-5
MLA_HEADS = 8
Q_LORA = 512
KV_LORA = 512
QK_NOPE = 128
QK_ROPE = 64
V_HEAD = 128
MLA_DIM = MLA_HEADS * V_HEAD
ROPE_THETA = 10000.0
CHUNK = 128
SG_GROUPS = 8
SG_GROUP_DIM = 128
SG_DIM = SG_GROUPS * SG_GROUP_DIM
D_FF = 5632
CONV_W = 3
RMS_EPS = 1e-6
LN_EPS = 1e-5

LANES = 128
SUBLANES = 8
LORA_PAD = 128
N_RW = 3 * RW_DIM + W_LORA + A_LORA + G_LORA
N_RW_PAD = 3 * RW_DIM + 2 * LORA_PAD + G_LORA
ROPE_PAD = 128
QK_PAD = QK_NOPE + ROPE_PAD
N_MLA_PAD = Q_LORA + KV_LORA + ROPE_PAD
VMEM_LIMIT = 52 * 1024 * 1024
LOG2_E = 1.4426950408889634


def _cparams(sem):
    return pltpu.CompilerParams(dimension_semantics=sem, vmem_limit_bytes=VMEM_LIMIT)


def _sigmoid(x):
    return 1.0 / (1.0 + jnp.exp(-x))


def _softplus(x):
    return jnp.maximum(x, 0.0) + jnp.log(1.0 + jnp.exp(-jnp.abs(x)))


def _gelu_tanh(x):
    return 0.5 * x * (1.0 + jnp.tanh(0.7978845608028654 * (x + 0.044715 * (x * x * x))))


def _bdot(a, b):
    return jnp.dot(a.astype(BF16), b, preferred_element_type=F32)


def _group_sum(x, ones_bd):
    hi = x.astype(BF16)
    lo = (x - hi.astype(F32)).astype(BF16)
    return (jnp.dot(hi, ones_bd, preferred_element_type=F32)
            + jnp.dot(lo, ones_bd, preferred_element_type=F32))


def _ada_body(c_ref, w_ref, b_ref, o_ref, *, kc):
    d = c_ref.shape[0]
    tn = o_ref.shape[-1]

    def step(i, acc):
        ck = c_ref[pl.ds(i * kc, kc), :]
        sk = ck * _sigmoid(ck)
        wk = w_ref[0, pl.ds(i * kc, kc), :]
        return acc + jnp.sum(wk * sk, axis=0, keepdims=True)

    acc = lax.fori_loop(0, d // kc, step, jnp.zeros((1, tn), F32))
    o_ref[0] = acc + b_ref[0]


def _ada_all(c, ada_w, ada_b):
    nl, d, n = ada_w.shape
    tn = 1024
    out = pl.pallas_call(
        functools.partial(_ada_body, kc=256),
        grid=(nl, n // tn),
        in_specs=[pl.BlockSpec((d, 1), lambda l, j: (0, 0)),
                  pl.BlockSpec((1, d, tn), lambda l, j: (l, 0, j)),
                  pl.BlockSpec((1, 1, tn), lambda l, j: (l, 0, j))],
        out_specs=pl.BlockSpec((1, 1, tn), lambda l, j: (l, 0, j)),
        out_shape=jax.ShapeDtypeStruct((nl, 1, n), F32),
        compiler_params=_cparams(("arbitrary", "arbitrary")),
        name="ada_mod",
    )(c.reshape(d, 1), ada_w, ada_b.reshape(nl, 1, n))
    return out


def _rope_body(pos_ref, inv_ref, msk_ref, sgn_ref, cos_ref, sin_ref):
    ang = pos_ref[...].astype(F32) * inv_ref[...]
    cos_ref[...] = jnp.cos(ang) * msk_ref[...]
    sin_ref[...] = jnp.sin(ang) * sgn_ref[...]


def _rope_pad(v):
    h = QK_ROPE // 2
    z = jnp.zeros(v.shape[:-1] + (ROPE_PAD // 2 - h,), v.dtype)
    return jnp.concatenate([v[..., :h], z, v[..., h:], z], axis=-1)


def _rope_tables(positions, t):
    tm = min(1024, t)
    inv = ROPE_THETA ** (-jnp.arange(0, QK_ROPE, 2, dtype=F32) / QK_ROPE)
    ones = jnp.ones((QK_ROPE // 2,), F32)
    inv_p = _rope_pad(jnp.concatenate([inv, inv]))[None]
    msk_p = _rope_pad(jnp.concatenate([ones, ones]))[None]
    sgn_p = _rope_pad(jnp.concatenate([-ones, ones]))[None]
    row = pl.BlockSpec((1, ROPE_PAD), lambda i: (0, 0))
    blk = pl.BlockSpec((tm, ROPE_PAD), lambda i: (i, 0))
    return pl.pallas_call(
        _rope_body,
        grid=(t // tm,),
        in_specs=[pl.BlockSpec((tm, 1), lambda i: (i, 0)), row, row, row],
        out_specs=[blk, blk],
        out_shape=[jax.ShapeDtypeStruct((t, ROPE_PAD), F32)] * 2,
        compiler_params=_cparams(("arbitrary",)),
        name="rope_tables",
    )(positions.reshape(t, 1), inv_p, msk_p, sgn_p)


def _norm_mod_body(x_ref, g_ref, sc_ref, sh_ref, o_ref):
    x = x_ref[...]
    y = x * lax.rsqrt(jnp.mean(x * x, axis=-1, keepdims=True) + RMS_EPS) * g_ref[...]
    o_ref[...] = (y * (1.0 + sc_ref[...]) + sh_ref[...]).astype(o_ref.dtype)


def _norm_mod(x, g, sc, sh):
    t, d = x.shape
    tm = min(512, t)
    row = pl.BlockSpec((1, d), lambda i: (0, 0))
    return pl.pallas_call(
        _norm_mod_body,
        grid=(t // tm,),
        in_specs=[pl.BlockSpec((tm, d), lambda i: (i, 0)), row, row, row],
        out_specs=pl.BlockSpec((tm, d), lambda i: (i, 0)),
        out_shape=jax.ShapeDtypeStruct((t, d), BF16),
        compiler_params=_cparams(("arbitrary",)),
        name="norm_mod",
    )(x, g, sc, sh)


def _mm_body(*refs, nk, epi):
    if epi == "residual":
        a_ref, b_ref, res_ref, gt_ref, o_ref = refs[:5]
        rest = refs[5:]
    else:
        a_ref, b_ref, o_ref = refs[:3]
        rest = refs[3:]

    def finish(acc):
        if epi == "sigmoid":
            acc = _sigmoid(acc)
        elif epi == "residual":
            acc = res_ref[...] + gt_ref[...] * acc
        o_ref[...] = acc.astype(o_ref.dtype)

    part = jnp.dot(a_ref[...], b_ref[...], preferred_element_type=F32)
    if nk == 1:
        finish(part)
        return
    acc_ref, = rest
    k = pl.program_id(2)

    @pl.when(k == 0)
    def _():
        acc_ref[...] = part

    @pl.when(k > 0)
    def _():
        acc_ref[...] += part

    @pl.when(k == nk - 1)
    def _():
        finish(acc_ref[...])


def _matmul(a, b, *, out_dtype, tm, tn, tk=None, epi="none", res=None, gate=None, name="matmul"):
    m, kd = a.shape
    n = b.shape[1]
    tm = min(tm, m)
    tn = min(tn, n)
    tk = kd if tk is None else tk
    nk = kd // tk
    assert m % tm == 0 and n % tn == 0 and kd % tk == 0
    in_specs = [pl.BlockSpec((tm, tk), lambda i, j, k: (i, k)),
                pl.BlockSpec((tk, tn), lambda i, j, k: (k, j))]
    args = [a, b]
    if epi == "residual":
        in_specs += [pl.BlockSpec((tm, tn), lambda i, j, k: (i, j)),
                     pl.BlockSpec((1, tn), lambda i, j, k: (0, j))]
        args += [res, gate]
    scratch = [pltpu.VMEM((tm, tn), F32)] if nk > 1 else []
    return pl.pallas_call(
        functools.partial(_mm_body, nk=nk, epi=epi),
        grid=(m // tm, n // tn, nk),
        in_specs=in_specs,
        out_specs=pl.BlockSpec((tm, tn), lambda i, j, k: (i, j)),
        out_shape=jax.ShapeDtypeStruct((m, n), out_dtype),
        scratch_shapes=scratch,
        compiler_params=_cparams(("arbitrary", "arbitrary", "arbitrary")),
        name=name,
    )(*args)


def _rw_prep_body(*refs, has_vres):
    (p_ref, pprev_ref, mu_ref, w0_ref, w2_ref, a0_ref, a2_ref, g2_ref, kk_ref, ka_ref,
     bd_ref) = refs[:11]
    pos = 11
    if has_vres:
        vf_ref, v0_ref, v1_ref, v2_ref = refs[pos:pos + 4]
        pos += 4
    r_out, w_out, k_out, v_out, a_out, b_out, g_out = refs[pos:pos + 7]

    i = pl.program_id(0)
    p = p_ref[...]
    tm = p.shape[0]
    last = pprev_ref[7:8, :] * (i > 0).astype(F32)
    rowid = lax.broadcasted_iota(jnp.int32, (tm, 1), 0)
    prev = jnp.where(rowid == 0, last, pltpu.roll(p, 1, axis=0))
    pm = p + (prev - p) * mu_ref[...]

    o = 3 * RW_DIM
    r = pm[:, 0:RW_DIM]
    k = pm[:, RW_DIM:2 * RW_DIM]
    v = pm[:, 2 * RW_DIM:o]
    xw = pm[:, o:o + LORA_PAD]
    xa = pm[:, o + LORA_PAD:o + 2 * LORA_PAD]
    xg = pm[:, o + 2 * LORA_PAD:]

    if has_vres:
        mix = _sigmoid(v0_ref[...] + _bdot(_bdot(v, v1_ref[...]), v2_ref[...]))
        v = v + (vf_ref[...] - v) * mix

    w_log = -_softplus(-(w0_ref[...] + _bdot(jnp.tanh(xw), w2_ref[...]))) - 0.5
    decay = jnp.exp(-jnp.exp(w_log))
    a = _sigmoid(a0_ref[...] + _bdot(xa, a2_ref[...]))
    g = _bdot(_sigmoid(xg), g2_ref[...])

    kk = k * kk_ref[...]
    nrm = jnp.sqrt(_group_sum(kk * kk, bd_ref[...]))
    kk = kk / jnp.maximum(nrm, 1e-12)
    kh = k * (1.0 + (a - 1.0) * ka_ref[...])

    r_out[...] = r
    w_out[...] = decay
    k_out[...] = kh
    v_out[...] = v
    a_out[...] = -kk
    b_out[...] = kk * a
    g_out[...] = g


def _rw_prep(p_rw, mu, w0, w2, a0, a2, g2, k_k, k_a, ones_bd, vres):
    t = p_rw.shape[0]
    tm = min(256, t)
    has_vres = vres is not None
    row = lambda n: pl.BlockSpec((1, n), lambda i: (0, 0))
    full = lambda a: pl.BlockSpec(a.shape, lambda i: (0, 0))
    blk = pl.BlockSpec((tm, RW_DIM), lambda i: (i, 0))
    in_specs = [pl.BlockSpec((tm, N_RW_PAD), lambda i: (i, 0)),
                pl.BlockSpec((8, N_RW_PAD), lambda i: (jnp.maximum(i * (tm // 8) - 1, 0), 0)),
                row(N_RW_PAD), row(RW_DIM), full(w2), row(RW_DIM), full(a2), full(g2),
                row(RW_DIM), row(RW_DIM), full(ones_bd)]
    args = [p_rw, p_rw, mu, w0, w2, a0, a2, g2, k_k, k_a, ones_bd]
    if has_vres:
        v_first, v0, v1, v2 = vres
        in_specs += [blk, row(RW_DIM), full(v1), full(v2)]
        args += [v_first, v0, v1, v2]
    return pl.pallas_call(
        functools.partial(_rw_prep_body, has_vres=has_vres),
        grid=(t // tm,),
        in_specs=in_specs,
        out_specs=[blk] * 7,
        out_shape=[jax.ShapeDtypeStruct((t, RW_DIM), F32)] * 7,
        compiler_params=_cparams(("arbitrary",)),
        name="rwkv_prep",
    )(*args)


SCAN_T = 128
HEAD_PAIRS = RW_HEADS // 2


def _rw_scan_body(r_ref, w_ref, k_ref, v_ref, a_ref, b_ref, y_ref, s_ref):
    @pl.when(pl.program_id(0) == 0)
    def _():
        s_ref[...] = jnp.zeros_like(s_ref)

    lane = lax.broadcasted_iota(jnp.int32, (RW_HEAD, LANES), 1)
    sub = lax.broadcasted_iota(jnp.int32, (RW_HEAD, LANES), 0)
    lo = lane < RW_HEAD
    lo8 = lax.broadcasted_iota(jnp.int32, (SUBLANES, LANES), 1) < RW_HEAD
    diag2 = jnp.where((lane & (RW_HEAD - 1)) == sub, 1.0, 0.0)
    same_half = (lax.broadcasted_iota(jnp.int32, (LANES, LANES), 0) // RW_HEAD
                 == lax.broadcasted_iota(jnp.int32, (LANES, LANES), 1) // RW_HEAD)
    ones_half = jnp.where(same_half, 1.0, 0.0).astype(BF16)
    row16 = lax.broadcasted_iota(jnp.int32, (2 * SUBLANES, LANES), 0) & (SUBLANES - 1)

    def step8(tb, carry):
        t0 = pl.multiple_of(tb * SUBLANES, SUBLANES)
        a_lo, a_hi, w8, b8, k8, r16, vb, yacc = [], [], [], [], [], [], [], []
        for p in range(HEAD_PAIRS):
            sl = pl.ds(p * LANES, LANES)
            tile = lambda ref: ref[pl.ds(t0, SUBLANES), sl]
            a8, r8, v8 = tile(a_ref), tile(r_ref), tile(v_ref)
            a_lo.append(jnp.where(lo8, a8, 0.0))
            a_hi.append(jnp.where(lo8, 0.0, a8))
            w8.append(tile(w_ref))
            b8.append(tile(b_ref))
            k8.append(tile(k_ref))
            r16.append(jnp.concatenate([jnp.where(lo8, r8, 0.0), jnp.where(lo8, 0.0, r8)], axis=0))
            v_diag = jnp.concatenate([diag2 * v8[j:j + 1, :] for j in range(SUBLANES)], axis=0)
            vb.append(jnp.dot(v_diag.astype(BF16), ones_half, preferred_element_type=F32))
            yacc.append(jnp.zeros((2 * SUBLANES, RW_HEAD), F32))
        for j in range(SUBLANES):
            row = lambda x8: x8[j:j + 1, :]
            for p in range(HEAD_PAIRS):
                s = s_ref[p]
                sa_lo = jnp.sum(s * row(a_lo[p]), axis=-1, keepdims=True)
                sa_hi = jnp.sum(s * row(a_hi[p]), axis=-1, keepdims=True)
                sa = jnp.where(lo, sa_lo, sa_hi)
                s = (s * row(w8[p]) + sa * row(b8[p])
                     + vb[p][j * RW_HEAD:(j + 1) * RW_HEAD, :] * row(k8[p]))
                s_ref[p] = s
                rj = jnp.where(row16 == j, r16[p], 0.0).astype(BF16)
                yacc[p] = yacc[p] + lax.dot_general(
                    rj, s.astype(BF16), (((1,), (1,)), ((), ())), preferred_element_type=F32)
        for p in range(HEAD_PAIRS):
            y_ref[pl.ds(t0, SUBLANES), pl.ds(p * LANES, LANES)] = jnp.concatenate(
                [yacc[p][:SUBLANES], yacc[p][SUBLANES:]], axis=1)
        return carry

    lax.fori_loop(0, SCAN_T // SUBLANES, step8, 0)


def _rw_scan(r, w, k, v, a, b):
    t = r.shape[0]
    blk = pl.BlockSpec((SCAN_T, RW_DIM), lambda i: (i, 0))
    return pl.pallas_call(
        _rw_scan_body,
        grid=(t // SCAN_T,),
        in_specs=[blk] * 6,
        out_specs=blk,
        out_shape=jax.ShapeDtypeStruct((t, RW_DIM), F32),
        scratch_shapes=[pltpu.VMEM((HEAD_PAIRS, RW_HEAD, LANES), F32)],
        compiler_params=_cparams(("arbitrary",)),
        name="rwkv_scan",
    )(r, w, k, v, a, b)


def _rw_post_body(y_ref, r_ref, k_ref, v_ref, g_ref, lw_ref, lb_ref, rk_ref, bd_ref, o_ref):
    bd = bd_ref[...]
    y = y_ref[...]
    inv_n = 1.0 / RW_HEAD
    d = y - _group_sum(y, bd) * inv_n
    var = _group_sum(d * d, bd) * inv_n
    yn = d * lax.rsqrt(var + GN_EPS) * lw_ref[...] + lb_ref[...]
    bonus = _group_sum(r_ref[...] * k_ref[...] * rk_ref[...], bd) * v_ref[...]
    o_ref[...] = ((yn + bonus) * g_ref[...]).astype(o_ref.dtype)


def _rw_post(y, r, k, v, g, lnx_w, lnx_b, r_k, ones_bd):
    t = y.shape[0]
    tm = min(256, t)
    blk = pl.BlockSpec((tm, RW_DIM), lambda i: (i, 0))
    row = pl.BlockSpec((1, RW_DIM), lambda i: (0, 0))
    return pl.pallas_call(
        _rw_post_body,
        grid=(t // tm,),
        in_specs=[blk] * 5 + [row] * 3 + [pl.BlockSpec(ones_bd.shape, lambda i: (0, 0))],
        out_specs=blk,
        out_shape=jax.ShapeDtypeStruct((t, RW_DIM), BF16),
        compiler_params=_cparams(("arbitrary",)),
        name="rwkv_post",
    )(y, r, k, v, g, lnx_w, lnx_b, r_k, ones_bd)


def _rms(x, n):
    return x * lax.rsqrt(jnp.sum(x * x, axis=-1, keepdims=True) * (1.0 / n) + RMS_EPS)


def _mla_prep_body(p_ref, cos_ref, sin_ref, qn_ref, kvn_ref, wq_ref, wk_ref, wv_ref,
                   qnn_ref, qnr_ref, knn_ref, knr_ref, q_out, k_out, v_out):
    p = p_ref[...]
    cos = cos_ref[...]
    sin = sin_ref[...]
    scale = (QK_NOPE + QK_ROPE) ** -0.5 * LOG2_E

    def rope(x):
        return x * cos + pltpu.roll(x, ROPE_PAD // 2, axis=1) * sin

    cq = _rms(p[:, :Q_LORA], Q_LORA) * qn_ref[...]
    ckv = _rms(p[:, Q_LORA:Q_LORA + KV_LORA], KV_LORA) * kvn_ref[...]
    qf = _bdot(cq, wq_ref[...])
    kn = _bdot(ckv, wk_ref[...])
    v_out[...] = _bdot(ckv, wv_ref[...]).astype(v_out.dtype)
    kr = rope(_rms(p[:, Q_LORA + KV_LORA:], QK_ROPE) * knr_ref[...]).astype(k_out.dtype)
    for h in range(MLA_HEADS):
        o = h * QK_PAD
        qn = _rms(qf[:, o:o + QK_NOPE], QK_NOPE) * qnn_ref[...]
        qr = rope(_rms(qf[:, o + QK_NOPE:o + QK_PAD], QK_ROPE) * qnr_ref[...])
        q_out[:, o:o + QK_NOPE] = (qn * scale).astype(q_out.dtype)
        q_out[:, o + QK_NOPE:o + QK_PAD] = (qr * scale).astype(q_out.dtype)
        kh = _rms(kn[:, h * QK_NOPE:(h + 1) * QK_NOPE], QK_NOPE) * knn_ref[...]
        k_out[:, o:o + QK_NOPE] = kh.astype(k_out.dtype)
        k_out[:, o + QK_NOPE:o + QK_PAD] = kr


def _mla_prep(p_mla, cos, sin, q_norm, kv_norm, wq, wk, wv, qnn, qnr, knn, knr):
    t = p_mla.shape[0]
    tm = min(256, t)
    row = lambda a: pl.BlockSpec(a.shape, lambda i: (0, 0))
    blk = lambda n: pl.BlockSpec((tm, n), lambda i: (i, 0))
    return pl.pallas_call(
        _mla_prep_body,
        grid=(t // tm,),
        in_specs=[blk(N_MLA_PAD), blk(ROPE_PAD), blk(ROPE_PAD), row(q_norm), row(kv_norm),
                  row(wq), row(wk), row(wv), row(qnn), row(qnr), row(knn), row(knr)],
        out_specs=[blk(MLA_HEADS * QK_PAD), blk(MLA_HEADS * QK_PAD), blk(MLA_DIM)],
        out_shape=[jax.ShapeDtypeStruct((t, MLA_HEADS * QK_PAD), BF16),
                   jax.ShapeDtypeStruct((t, MLA_HEADS * QK_PAD), BF16),
                   jax.ShapeDtypeStruct((t, MLA_DIM), BF16)],
        compiler_params=_cparams(("arbitrary",)),
        name="mla_prep",
    )(p_mla, cos, sin, q_norm, kv_norm, wq, wk, wv, qnn, qnr, knn, knr)


ATT_Q = 1024
ATT_HALF = ATT_Q // 2
ATT_K = 1024


def _attn_update(q, k, v, m_ref, l_ref, acc_ref, half, col_shift):
    s = lax.dot_general(q, k, (((1,), (1,)), ((), ())), preferred_element_type=F32)
    if col_shift is not None:
        rows = lax.broadcasted_iota(jnp.int32, s.shape, 0)
        cols = lax.broadcasted_iota(jnp.int32, s.shape, 1)
        s = jnp.where(cols <= rows + col_shift, s, -jnp.inf)
    tiles = [s[:, c * LANES:(c + 1) * LANES] for c in range(s.shape[1] // LANES)]
    mx = functools.reduce(jnp.maximum, tiles)
    m_old = m_ref[half]
    m_new = jnp.maximum(m_old, jnp.max(mx, axis=1, keepdims=True))
    alpha = jnp.exp2(m_old - m_new)
    ps = [jnp.exp2(x - m_new) for x in tiles]
    l_ref[half] = alpha * l_ref[half] + functools.reduce(jnp.add, ps)
    p = jnp.concatenate([x.astype(v.dtype) for x in ps], axis=1)
    acc_ref[half] = alpha * acc_ref[half] + jnp.dot(p, v, preferred_element_type=F32)
    m_ref[half] = m_new


def _attn_body(q_ref, k_ref, v_ref, o_ref, m_ref, l_ref, acc_ref):
    qi = pl.program_id(1)
    m_ref[...] = jnp.full_like(m_ref, -jnp.inf)
    l_ref[...] = jnp.zeros_like(l_ref)
    acc_ref[...] = jnp.zeros_like(acc_ref)
    halves = [(0, slice(0, ATT_HALF)), (1, slice(ATT_HALF, ATT_Q))]

    def full_block(j, carry):
        r0 = pl.multiple_of(j * ATT_K, ATT_K)
        k = k_ref[pl.ds(r0, ATT_K), :]
        v = v_ref[pl.ds(r0, ATT_K), :]
        for half, rows in halves:
            _attn_update(q_ref[rows, :], k, v, m_ref, l_ref, acc_ref, half, None)
        return carry

    lax.fori_loop(0, qi * (ATT_Q // ATT_K), full_block, 0)

    d0 = pl.multiple_of(qi * ATT_Q, ATT_Q)
    _attn_update(q_ref[:ATT_HALF, :], k_ref[pl.ds(d0, ATT_HALF), :], v_ref[pl.ds(d0, ATT_HALF), :],
                 m_ref, l_ref, acc_ref, 0, 0)
    _attn_update(q_ref[ATT_HALF:, :], k_ref[pl.ds(d0, ATT_Q), :], v_ref[pl.ds(d0, ATT_Q), :],
                 m_ref, l_ref, acc_ref, 1, ATT_HALF)
    for half, rows in halves:
        denom = jnp.sum(l_ref[half], axis=1, keepdims=True)
        o_ref[rows, :] = (acc_ref[half] / denom).astype(o_ref.dtype)


def _attention(q, k, v):
    t = q.shape[0]
    assert t % ATT_Q == 0 and ATT_Q % ATT_K == 0
    stat = pltpu.VMEM((2, ATT_HALF, LANES), F32)
    return pl.pallas_call(
        _attn_body,
        grid=(MLA_HEADS, t // ATT_Q),
        in_specs=[pl.BlockSpec((ATT_Q, QK_PAD), lambda h, i: (i, h)),
                  pl.BlockSpec((t, QK_PAD), lambda h, i: (0, h)),
                  pl.BlockSpec((t, V_HEAD), lambda h, i: (0, h))],
        out_specs=pl.BlockSpec((ATT_Q, V_HEAD), lambda h, i: (i, h)),
        out_shape=jax.ShapeDtypeStruct((t, MLA_DIM), BF16),
        scratch_shapes=[stat, stat, pltpu.VMEM((2, ATT_HALF, V_HEAD), F32)],
        compiler_params=_cparams(("arbitrary", "arbitrary")),
        name="mla_attention",
    )(q, k, v)


def _sg_body(p_ref, lw_ref, lb_ref, ws_ref, bs_ref, o_ref):
    tm = p_ref.shape[0]
    u = _gelu_tanh(p_ref[:, :SG_DIM])
    gv = _gelu_tanh(p_ref[:, SG_DIM:])
    mu = jnp.mean(gv, axis=-1, keepdims=True)
    d = gv - mu
    var = jnp.mean(d * d, axis=-1, keepdims=True)
    vn = (d * lax.rsqrt(var + LN_EPS) * lw_ref[...] + lb_ref[...]).astype(BF16)
    rows = lax.broadcasted_iota(jnp.int32, (CHUNK, CHUNK), 0)
    cols = lax.broadcasted_iota(jnp.int32, (CHUNK, CHUNK), 1)
    tril = cols <= rows
    for g in range(SG_GROUPS):
        gs = slice(g * SG_GROUP_DIM, (g + 1) * SG_GROUP_DIM)
        wg = jnp.where(tril, ws_ref[g], 0.0).astype(BF16)
        for c in range(tm // CHUNK):
            cs = slice(c * CHUNK, (c + 1) * CHUNK)
            s = jnp.dot(wg, vn[cs, gs], preferred_element_type=F32) + bs_ref[:, gs]
            o_ref[cs, gs] = (u[cs, gs] * s).astype(o_ref.dtype)


def _spatial_gating(p_sg, ln_w, ln_b, ws, bs_full):
    t = p_sg.shape[0]
    tm = min(256, t)
    row = pl.BlockSpec((1, SG_DIM), lambda i: (0, 0))
    return pl.pallas_call(
        _sg_body,
        grid=(t // tm,),
        in_specs=[pl.BlockSpec((tm, 2 * SG_DIM), lambda i: (i, 0)), row, row,
                  pl.BlockSpec(ws.shape, lambda i: (0, 0, 0)),
                  pl.BlockSpec(bs_full.shape, lambda i: (0, 0))],
        out_specs=pl.BlockSpec((tm, SG_DIM), lambda i: (i, 0)),
        out_shape=jax.ShapeDtypeStruct((t, SG_DIM), BF16),
        compiler_params=_cparams(("arbitrary",)),
        name="spatial_gating",
    )(p_sg, ln_w, ln_b, ws, bs_full)


def _merge_body(ya_ref, yb_ref, yc_ref, wa_ref, wb_ref, wc_ref, ga_ref, gb_ref, gc_ref, o_ref):
    m = ga_ref[...].astype(F32) * jnp.dot(ya_ref[...], wa_ref[...], preferred_element_type=F32)
    m += gb_ref[...].astype(F32) * jnp.dot(yb_ref[...], wb_ref[...], preferred_element_type=F32)
    m += gc_ref[...].astype(F32) * jnp.dot(yc_ref[...], wc_ref[...], preferred_element_type=F32)
    o_ref[...] = m.astype(o_ref.dtype)


def _merge(ya, yb, yc, wa, wb, wc, gates):
    t = ya.shape[0]
    d = wa.shape[1]
    tm = min(512, t)
    tn = 1024
    nj = d // tn
    yblk = pl.BlockSpec((tm, ya.shape[1]), lambda i, j: (i, 0))
    wblk = pl.BlockSpec((wa.shape[0], tn), lambda i, j: (0, j))
    gblk = lambda b: pl.BlockSpec((tm, tn), lambda i, j: (i, b * nj + j))
    return pl.pallas_call(
        _merge_body,
        grid=(t // tm, nj),
        in_specs=[yblk, yblk, yblk, wblk, wblk, wblk, gblk(0), gblk(1), gblk(2)],
        out_specs=pl.BlockSpec((tm, tn), lambda i, j: (i, j)),
        out_shape=jax.ShapeDtypeStruct((t, d), BF16),
        compiler_params=_cparams(("arbitrary", "arbitrary")),
        name="branch_merge",
    )(ya, yb, yc, wa, wb, wc, gates, gates, gates)


CONV_CARRY = 8


def _ffn_up_body(h_ref, wg_ref, wv_ref, cwg_ref, cwv_ref, cbg_ref, cbv_ref, o_ref,
                 cg_ref, cv_ref):
    i = pl.program_id(1)

    @pl.when(i == 0)
    def _():
        cg_ref[...] = jnp.zeros_like(cg_ref)
        cv_ref[...] = jnp.zeros_like(cv_ref)

    h = h_ref[...]
    tm = h.shape[0]
    rowid = lax.broadcasted_iota(jnp.int32, (tm, 1), 0)

    def conv(w_ref, cw_ref, cb_ref, carry_ref):
        up = jnp.dot(h, w_ref[...], preferred_element_type=F32)
        c1 = carry_ref[CONV_CARRY - 1:CONV_CARRY, :]
        c2 = carry_ref[CONV_CARRY - 2:CONV_CARRY - 1, :]
        s1 = jnp.where(rowid == 0, c1, pltpu.roll(up, 1, axis=0))
        s2 = jnp.where(rowid == 0, c2, jnp.where(rowid == 1, c1, pltpu.roll(up, 2, axis=0)))
        carry_ref[...] = up[tm - CONV_CARRY:, :]
        return cb_ref[...] + cw_ref[0:1, :] * s2 + cw_ref[1:2, :] * s1 + cw_ref[2:3, :] * up

    gate = conv(wg_ref, cwg_ref, cbg_ref, cg_ref)
    val = conv(wv_ref, cwv_ref, cbv_ref, cv_ref)
    o_ref[...] = (gate * _sigmoid(gate) * val).astype(o_ref.dtype)


def _ffn_up(h, w_up, conv_w, conv_b):
    t, d = h.shape
    tm = min(1024, t)
    tn = 512
    nj = D_FF // tn
    return pl.pallas_call(
        _ffn_up_body,
        grid=(nj, t // tm),
        in_specs=[pl.BlockSpec((tm, d), lambda j, i: (i, 0)),
                  pl.BlockSpec((d, tn), lambda j, i: (0, j)),
                  pl.BlockSpec((d, tn), lambda j, i: (0, nj + j)),
                  pl.BlockSpec((CONV_W, tn), lambda j, i: (0, j)),
                  pl.BlockSpec((CONV_W, tn), lambda j, i: (0, nj + j)),
                  pl.BlockSpec((1, tn), lambda j, i: (0, j)),
                  pl.BlockSpec((1, tn), lambda j, i: (0, nj + j))],
        out_specs=pl.BlockSpec((tm, tn), lambda j, i: (i, j)),
        out_shape=jax.ShapeDtypeStruct((t, D_FF), BF16),
        scratch_shapes=[pltpu.VMEM((CONV_CARRY, tn), F32), pltpu.VMEM((CONV_CARRY, tn), F32)],
        compiler_params=_cparams(("arbitrary", "arbitrary")),
        name="ffn_up_conv",
    )(h, w_up, w_up, conv_w, conv_w, conv_b, conv_b)


def _pad_cols(a, n):
    return jnp.pad(a, ((0, 0), (0, n - a.shape[1])))


def _rw_cols(a):
    o = 3 * RW_DIM
    return jnp.concatenate([a[:, :o], _pad_cols(a[:, o:o + W_LORA], LORA_PAD),
                            _pad_cols(a[:, o + W_LORA:o + W_LORA + A_LORA], LORA_PAD),
                            a[:, o + W_LORA + A_LORA:N_RW]], axis=1)


def _pad_rows(a, n):
    return jnp.pad(a, ((0, n - a.shape[0]), (0, 0)))


def kernel(x, c, positions, ada_w, ada_b, norm_mix_g, norm_ffn_g, w_in, rw_mu, rw_w0, rw_w2, rw_a0, rw_a2, rw_g2, rw_kk, rw_ka, rw_rk, rw_lnx_w, rw_lnx_b, rw_v0, rw_v1, rw_v2, mla_q_norm, mla_kv_norm, mla_w_uq, mla_w_ukv, mla_qn_nope, mla_qn_rope, mla_kn_nope, mla_kn_rope, sg_ln_w, sg_ln_b, sg_ws, sg_b, w_br_a, w_br_b, w_br_c, w_out, ffn_up, ffn_conv, ffn_conv_b, ffn_down):
    b_, t, d = x.shape
    assert b_ == 1 and d == D_MODEL
    depth = w_in.shape[0]
    xs = x.reshape(t, d)

    mod = _ada_all(c, ada_w, ada_b)
    cos, sin = _rope_tables(positions, t)
    head_id = jnp.arange(RW_DIM) // RW_HEAD
    ones_bd = (head_id[:, None] == head_id[None, :]).astype(BF16)
    row = lambda a: a.reshape(1, -1)

    v_first = None
    for l in range(depth):
        sh1, sc1, gt1, sh2, sc2, gt2 = [mod[l, :, i * d:(i + 1) * d] for i in range(6)]

        h = _norm_mod(xs, row(norm_mix_g[l]), sc1, sh1)
        wl = w_in[l]
        o_dq = N_RW
        o_kr = N_RW + Q_LORA + KV_LORA
        o_sg = o_kr + QK_ROPE
        o_gt = o_sg + 2 * SG_DIM
        w_rw = _rw_cols(wl).astype(BF16)
        w_mla = jnp.concatenate([wl[:, o_dq:o_kr], _rope_pad(wl[:, o_kr:o_sg])], axis=1).astype(BF16)
        p_rw = _matmul(h, w_rw, out_dtype=F32, tm=1024, tn=512, name="proj_rw")
        p_mla = _matmul(h, w_mla, out_dtype=F32, tm=1024, tn=N_MLA_PAD, name="proj_mla")
        p_sg = _matmul(h, wl[:, o_sg:o_gt].astype(BF16), out_dtype=F32, tm=1024, tn=1024,
                       name="proj_sg")
        gates = _matmul(h, wl[:, o_gt:].astype(BF16), out_dtype=BF16, tm=1024, tn=1024,
                        epi="sigmoid", name="proj_gates")

        vres = None
        if l > 0:
            vres = (v_first, row(rw_v0[l - 1]), rw_v1[l - 1].astype(BF16),
                    rw_v2[l - 1].astype(BF16))
        r_, w_, k_, v_, a_, b2_, g_ = _rw_prep(
            p_rw, _rw_cols(row(rw_mu[l])), row(rw_w0[l]),
            _pad_rows(rw_w2[l], LORA_PAD).astype(BF16), row(rw_a0[l]),
            _pad_rows(rw_a2[l], LORA_PAD).astype(BF16), rw_g2[l].astype(BF16),
            row(rw_kk[l]), row(rw_ka[l]), ones_bd, vres)
        if l == 0:
            v_first = v_
        y_scan = _rw_scan(r_, w_, k_, v_, a_, b2_)
        y_a = _rw_post(y_scan, r_, k_, v_, g_, row(rw_lnx_w[l]), row(rw_lnx_b[l]),
                       row(rw_rk[l]), ones_bd)

        wq = mla_w_uq[l].reshape(Q_LORA, MLA_HEADS, QK_NOPE + QK_ROPE)
        wq = jnp.concatenate([wq[..., :QK_NOPE], _rope_pad(wq[..., QK_NOPE:])], axis=-1)
        wq = wq.reshape(Q_LORA, MLA_HEADS * QK_PAD).astype(BF16)
        wkv = mla_w_ukv[l].reshape(KV_LORA, MLA_HEADS, QK_NOPE + V_HEAD)
        wk = wkv[..., :QK_NOPE].reshape(KV_LORA, MLA_HEADS * QK_NOPE).astype(BF16)
        wv = wkv[..., QK_NOPE:].reshape(KV_LORA, MLA_DIM).astype(BF16)
        q, k, v = _mla_prep(p_mla, cos, sin, row(mla_q_norm[l]), row(mla_kv_norm[l]), wq, wk, wv,
                            row(mla_qn_nope[l]), row(_rope_pad(mla_qn_rope[l])),
                            row(mla_kn_nope[l]), row(_rope_pad(mla_kn_rope[l])))
        y_b = _attention(q, k, v)

        bs_full = jnp.repeat(sg_b[l].T, SG_GROUP_DIM, axis=1)
        y_c = _spatial_gating(p_sg, row(sg_ln_w[l]), row(sg_ln_b[l]), sg_ws[l], bs_full)

        merged = _merge(y_a, y_b, y_c, w_br_a[l].astype(BF16), w_br_b[l].astype(BF16),
                        w_br_c[l].astype(BF16), gates)
        xs = _matmul(merged, w_out[l].astype(BF16), out_dtype=F32, tm=1024, tn=1024,
                     epi="residual", res=xs, gate=gt1, name="out_proj")

        h = _norm_mod(xs, row(norm_ffn_g[l]), sc2, sh2)
        act = _ffn_up(h, ffn_up[l].astype(BF16), ffn_conv[l], row(ffn_conv_b[l]))
        xs = _matmul(act, ffn_down[l].astype(BF16), out_dtype=F32, tm=1024, tn=1024,
                     tk=D_FF // 2, epi="residual", res=xs, gate=gt2, name="ffn_down")

    return xs.reshape(b_, t, d)
```

```python
import functools

import jax
import jax.numpy as jnp
from jax import lax
from jax.experimental import pallas as pl
from jax.experimental.pallas import tpu as pltpu

F32 = jnp.float32
BF16 = jnp.bfloat16

D_MODEL = 2048
RW_HEAD = 64
RW_HEADS = 16
RW_DIM = RW_HEADS * RW_HEAD
W_LORA = 96
A_LORA = 96
V_LORA = 64
G_LORA = 256
GN_EPS = 64e-5
MLA_HEADS = 8
Q_LORA = 512
KV_LORA = 512
QK_NOPE = 128
QK_ROPE = 64
V_HEAD = 128
MLA_DIM = MLA_HEADS * V_HEAD
ROPE_THETA = 10000.0
CHUNK = 128
SG_GROUPS = 8
SG_GROUP_DIM = 128
SG_DIM = SG_GROUPS * SG_GROUP_DIM
D_FF = 5632
CONV_W = 3
RMS_EPS = 1e-6
LN_EPS = 1e-5

LANES = 128
SUBLANES = 8
LORA_PAD = 128
N_RW = 3 * RW_DIM + W_LORA + A_LORA + G_LORA
N_RW_PAD = 3 * RW_DIM + 2 * LORA_PAD + G_LORA
ROPE_PAD = 128
QK_PAD = QK_NOPE + ROPE_PAD
N_MLA_PAD = Q_LORA + KV_LORA + ROPE_PAD
VMEM_LIMIT = 52 * 1024 * 1024
LOG2_E = 1.4426950408889634


def _cparams(sem):
    return pltpu.CompilerParams(dimension_semantics=sem, vmem_limit_bytes=VMEM_LIMIT)


def _sigmoid(x):
    return 1.0 / (1.0 + jnp.exp(-x))


def _softplus(x):
    return jnp.maximum(x, 0.0) + jnp.log(1.0 + jnp.exp(-jnp.abs(x)))


def _gelu_tanh(x):
    return 0.5 * x * (1.0 + jnp.tanh(0.7978845608028654 * (x + 0.044715 * (x * x * x))))


def _bdot(a, b):
    return jnp.dot(a.astype(BF16), b, preferred_element_type=F32)


def _group_sum(x, ones_bd):
    hi = x.astype(BF16)
    lo = (x - hi.astype(F32)).astype(BF16)
    return (jnp.dot(hi, ones_bd, preferred_element_type=F32)
            + jnp.dot(lo, ones_bd, preferred_element_type=F32))


def _ada_body(c_ref, w_ref, b_ref, o_ref, *, kc):
    d = c_ref.shape[0]
    tn = o_ref.shape[-1]

    def step(i, acc):
        ck = c_ref[pl.ds(i * kc, kc), :]
        sk = ck * _sigmoid(ck)
        wk = w_ref[0, pl.ds(i * kc, kc), :]
        return acc + jnp.sum(wk * sk, axis=0, keepdims=True)

    acc = lax.fori_loop(0, d // kc, step, jnp.zeros((1, tn), F32))
    o_ref[0] = acc + b_ref[0]


def _ada_all(c, ada_w, ada_b):
    nl, d, n = ada_w.shape
    tn = 1024
    out = pl.pallas_call(
        functools.partial(_ada_body, kc=256),
        grid=(nl, n // tn),
        in_specs=[pl.BlockSpec((d, 1), lambda l, j: (0, 0)),
                  pl.BlockSpec((1, d, tn), lambda l, j: (l, 0, j)),
                  pl.BlockSpec((1, 1, tn), lambda l, j: (l, 0, j))],
        out_specs=pl.BlockSpec((1, 1, tn), lambda l, j: (l, 0, j)),
        out_shape=jax.ShapeDtypeStruct((nl, 1, n), F32),
        compiler_params=_cparams(("arbitrary", "arbitrary")),
        name="ada_mod",
    )(c.reshape(d, 1), ada_w, ada_b.reshape(nl, 1, n))
    return out


def _rope_body(pos_ref, inv_ref, msk_ref, sgn_ref, cos_ref, sin_ref):
    ang = pos_ref[...].astype(F32) * inv_ref[...]
    cos_ref[...] = jnp.cos(ang) * msk_ref[...]
    sin_ref[...] = jnp.sin(ang) * sgn_ref[...]


def _rope_pad(v):
    h = QK_ROPE // 2
    z = jnp.zeros(v.shape[:-1] + (ROPE_PAD // 2 - h,), v.dtype)
    return jnp.concatenate([v[..., :h], z, v[..., h:], z], axis=-1)


def _rope_tables(positions, t):
    tm = min(1024, t)
    inv = ROPE_THETA ** (-jnp.arange(0, QK_ROPE, 2, dtype=F32) / QK_ROPE)
    ones = jnp.ones((QK_ROPE // 2,), F32)
    inv_p = _rope_pad(jnp.concatenate([inv, inv]))[None]
    msk_p = _rope_pad(jnp.concatenate([ones, ones]))[None]
    sgn_p = _rope_pad(jnp.concatenate([-ones, ones]))[None]
    row = pl.BlockSpec((1, ROPE_PAD), lambda i: (0, 0))
    blk = pl.BlockSpec((tm, ROPE_PAD), lambda i: (i, 0))
    return pl.pallas_call(
        _rope_body,
        grid=(t // tm,),
        in_specs=[pl.BlockSpec((tm, 1), lambda i: (i, 0)), row, row, row],
        out_specs=[blk, blk],
        out_shape=[jax.ShapeDtypeStruct((t, ROPE_PAD), F32)] * 2,
        compiler_params=_cparams(("arbitrary",)),
        name="rope_tables",
    )(positions.reshape(t, 1), inv_p, msk_p, sgn_p)


def _norm_mod_body(x_ref, g_ref, sc_ref, sh_ref, o_ref):
    x = x_ref[...]
    y = x * lax.rsqrt(jnp.mean(x * x, axis=-1, keepdims=True) + RMS_EPS) * g_ref[...]
    o_ref[...] = (y * (1.0 + sc_ref[...]) + sh_ref[...]).astype(o_ref.dtype)


def _norm_mod(x, g, sc, sh):
    t, d = x.shape
    tm = min(512, t)
    row = pl.BlockSpec((1, d), lambda i: (0, 0))
    return pl.pallas_call(
        _norm_mod_body,
        grid=(t // tm,),
        in_specs=[pl.BlockSpec((tm, d), lambda i: (i, 0)), row, row, row],
        out_specs=pl.BlockSpec((tm, d), lambda i: (i, 0)),
        out_shape=jax.ShapeDtypeStruct((t, d), BF16),
        compiler_params=_cparams(("arbitrary",)),
        name="norm_mod",
    )(x, g, sc, sh)


def _mm_body(*refs, nk, epi):
    if epi == "residual":
        a_ref, b_ref, res_ref, gt_ref, o_ref = refs[:5]
        rest = refs[5:]
    else:
        a_ref, b_ref, o_ref = refs[:3]
        rest = refs[3:]

    def finish(acc):
        if epi == "sigmoid":
            acc = _sigmoid(acc)
        elif epi == "residual":
            acc = res_ref[...] + gt_ref[...] * acc
        o_ref[...] = acc.astype(o_ref.dtype)

    part = jnp.dot(a_ref[...], b_ref[...], preferred_element_type=F32)
    if nk == 1:
        finish(part)
        return
    acc_ref, = rest
    k = pl.program_id(2)

    @pl.when(k == 0)
    def _():
        acc_ref[...] = part

    @pl.when(k > 0)
    def _():
        acc_ref[...] += part

    @pl.when(k == nk - 1)
    def _():
        finish(acc_ref[...])


def _matmul(a, b, *, out_dtype, tm, tn, tk=None, epi="none", res=None, gate=None, name="matmul"):
    m, kd = a.shape
    n = b.shape[1]
    tm = min(tm, m)
    tn = min(tn, n)
    tk = kd if tk is None else tk
    nk = kd // tk
    assert m % tm == 0 and n % tn == 0 and kd % tk == 0
    in_specs = [pl.BlockSpec((tm, tk), lambda i, j, k: (i, k)),
                pl.BlockSpec((tk, tn), lambda i, j, k: (k, j))]
    args = [a, b]
    if epi == "residual":
        in_specs += [pl.BlockSpec((tm, tn), lambda i, j, k: (i, j)),
                     pl.BlockSpec((1, tn), lambda i, j, k: (0, j))]
        args += [res, gate]
    scratch = [pltpu.VMEM((tm, tn), F32)] if nk > 1 else []
    return pl.pallas_call(
        functools.partial(_mm_body, nk=nk, epi=epi),
        grid=(m // tm, n // tn, nk),
        in_specs=in_specs,
        out_specs=pl.BlockSpec((tm, tn), lambda i, j, k: (i, j)),
        out_shape=jax.ShapeDtypeStruct((m, n), out_dtype),
        scratch_shapes=scratch,
        compiler_params=_cparams(("arbitrary", "arbitrary", "arbitrary")),
        name=name,
    )(*args)


def _rw_prep_body(*refs, has_vres):
    (p_ref, pprev_ref, mu_ref, w0_ref, w2_ref, a0_ref, a2_ref, g2_ref, kk_ref, ka_ref,
     bd_ref) = refs[:11]
    pos = 11
    if has_vres:
        vf_ref, v0_ref, v1_ref, v2_ref = refs[pos:pos + 4]
        pos += 4
    r_out, w_out, k_out, v_out, a_out, b_out, g_out = refs[pos:pos + 7]

    i = pl.program_id(0)
    p = p_ref[...]
    tm = p.shape[0]
    last = pprev_ref[7:8, :] * (i > 0).astype(F32)
    rowid = lax.broadcasted_iota(jnp.int32, (tm, 1), 0)
    prev = jnp.where(rowid == 0, last, pltpu.roll(p, 1, axis=0))
    pm = p + (prev - p) * mu_ref[...]

    o = 3 * RW_DIM
    r = pm[:, 0:RW_DIM]
    k = pm[:, RW_DIM:2 * RW_DIM]
    v = pm[:, 2 * RW_DIM:o]
    xw = pm[:, o:o + LORA_PAD]
    xa = pm[:, o + LORA_PAD:o + 2 * LORA_PAD]
    xg = pm[:, o + 2 * LORA_PAD:]

    if has_vres:
        mix = _sigmoid(v0_ref[...] + _bdot(_bdot(v, v1_ref[...]), v2_ref[...]))
        v = v + (vf_ref[...] - v) * mix

    w_log = -_softplus(-(w0_ref[...] + _bdot(jnp.tanh(xw), w2_ref[...]))) - 0.5
    log_decay = -jnp.exp(w_log)
    a = _sigmoid(a0_ref[...] + _bdot(xa, a2_ref[...]))
    g = _bdot(_sigmoid(xg), g2_ref[...])

    kk = k * kk_ref[...]
    nrm = jnp.sqrt(_group_sum(kk * kk, bd_ref[...]))
    kk = kk / jnp.maximum(nrm, 1e-12)
    kh = k * (1.0 + (a - 1.0) * ka_ref[...])

    r_out[...] = r
    w_out[...] = log_decay
    k_out[...] = kh
    v_out[...] = v
    a_out[...] = -kk
    b_out[...] = kk * a
    g_out[...] = g


def _rw_prep(p_rw, mu, w0, w2, a0, a2, g2, k_k, k_a, ones_bd, vres):
    t = p_rw.shape[0]
    tm = min(256, t)
    has_vres = vres is not None
    row = lambda n: pl.BlockSpec((1, n), lambda i: (0, 0))
    full = lambda a: pl.BlockSpec(a.shape, lambda i: (0, 0))
    blk = pl.BlockSpec((tm, RW_DIM), lambda i: (i, 0))
    in_specs = [pl.BlockSpec((tm, N_RW_PAD), lambda i: (i, 0)),
                pl.BlockSpec((8, N_RW_PAD), lambda i: (jnp.maximum(i * (tm // 8) - 1, 0), 0)),
                row(N_RW_PAD), row(RW_DIM), full(w2), row(RW_DIM), full(a2), full(g2),
                row(RW_DIM), row(RW_DIM), full(ones_bd)]
    args = [p_rw, p_rw, mu, w0, w2, a0, a2, g2, k_k, k_a, ones_bd]
    if has_vres:
        v_first, v0, v1, v2 = vres
        in_specs += [blk, row(RW_DIM), full(v1), full(v2)]
        args += [v_first, v0, v1, v2]
    return pl.pallas_call(
        functools.partial(_rw_prep_body, has_vres=has_vres),
        grid=(t // tm,),
        in_specs=in_specs,
        out_specs=[blk] * 7,
        out_shape=[jax.ShapeDtypeStruct((t, RW_DIM), F32)] * 7,
        compiler_params=_cparams(("arbitrary",)),
        name="rwkv_prep",
    )(*args)


SCAN_T = 128
HEAD_PAIRS = RW_HEADS // 2
GROUP = 16
NT_DIMS = (((1,), (1,)), ((), ()))
TN_DIMS = (((0,), (0,)), ((), ()))


def _split_bf16(x):
    hi = x.astype(BF16)
    return hi, (x - hi.astype(F32)).astype(BF16)


def _group_cumsum(x, rowmod):
    sh = 1
    while sh < GROUP:
        x = x + jnp.where(rowmod >= sh, pltpu.roll(x, sh, axis=0), 0.0)
        sh *= 2
    return x


def _group_last(x, rowmod):
    n = x.shape[0]
    x = jnp.where(rowmod == GROUP - 1, x, 0.0)
    sh = 1
    while sh < GROUP:
        x = x + jnp.where(rowmod < GROUP - sh, pltpu.roll(x, n - sh, axis=0), 0.0)
        sh *= 2
    return x


def _rw_chunk_body(r_ref, lw_ref, k_ref, v_ref, a_ref, b_ref, y_ref,
                   s_ref, w4_ref, rt_ref, bhh_ref, bhl_ref, kh_ref, vc_ref, c1c_ref, c1r_ref, p8_ref,
                   z_ref, ys_ref, vv_ref, arb_ref, ark_ref):
    @pl.when(pl.program_id(0) == 0)
    def _():
        s_ref[...] = jnp.zeros_like(s_ref)

    tb = r_ref.shape[0]
    ng = tb // GROUP
    pairs = range(HEAD_PAIRS)
    items = [(p, h) for p in pairs for h in range(2)]
    rowmod = lax.broadcasted_iota(jnp.int32, (tb, LANES), 0) & (GROUP - 1)
    lo = lax.broadcasted_iota(jnp.int32, (tb, LANES), 1) < RW_HEAD
    ti = lax.broadcasted_iota(jnp.int32, (tb, tb), 0)
    si = lax.broadcasted_iota(jnp.int32, (tb, tb), 1)
    same = (ti // GROUP) == (si // GROUP)
    strict = same & (si < ti)
    incl = same & (si <= ti)
    bdot = lambda x, y: jnp.dot(x, y, preferred_element_type=F32)
    ntdot = lambda x, y: lax.dot_general(x, y, NT_DIMS, preferred_element_type=F32)

    pre = []
    for p in pairs:
        sl = slice(p * LANES, (p + 1) * LANES)
        lw = lw_ref[:, sl]
        cs = _group_cumsum(lw, rowmod)
        cse = _group_last(cs, rowmod)
        pend = jnp.exp(cse - cs)
        pinv = jnp.exp(-cs)
        at_hi, at_lo = _split_bf16(a_ref[:, sl] * jnp.exp(cs - lw))
        bh_hi, bh_lo = _split_bf16(b_ref[:, sl] * pend)
        pre.append(dict(
            at_hi=at_hi, at_lo=at_lo, bh_hi=bh_hi, bh_lo=bh_lo,
            rt=(r_ref[:, sl] * jnp.exp(cs)).astype(BF16),
            bt=(b_ref[:, sl] * pinv).astype(BF16), kt=(k_ref[:, sl] * pinv).astype(BF16),
            kh=(k_ref[:, sl] * pend).astype(BF16), v=v_ref[:, sl], p8=jnp.exp(cse)))
    zero = jnp.zeros((tb, LANES), BF16)
    grams = []
    for p in pairs:
        d = pre[p]
        lhs = jnp.concatenate([jnp.where(lo, d["at_hi"], zero), jnp.where(lo, zero, d["at_hi"]),
                               jnp.where(lo, d["rt"], zero), jnp.where(lo, zero, d["rt"])], axis=0)
        grams.append(ntdot(lhs, jnp.concatenate([d["bt"], d["kt"]], axis=0)))
    mask = lambda h, x: jnp.where(lo, x, zero) if h == 0 else jnp.where(lo, zero, x)
    n1, mk, npow, tm1, vh, ath = {}, {}, {}, {}, {}, {}
    for p, h in items:
        g = grams[p]
        n1[p, h] = jnp.where(strict, g[h * tb:(h + 1) * tb, :tb], 0.0)
        mk[p, h] = jnp.where(strict, g[h * tb:(h + 1) * tb, tb:], 0.0)
        arb_ref[p, h] = jnp.where(incl, g[(2 + h) * tb:(3 + h) * tb, :tb], 0.0).astype(BF16)
        ark_ref[p, h] = jnp.where(incl, g[(2 + h) * tb:(3 + h) * tb, tb:], 0.0).astype(BF16)
        tm1[p, h] = n1[p, h]
        npow[p, h] = n1[p, h].astype(BF16)
        vh[p, h] = pre[p]["v"][:, h * RW_HEAD:(h + 1) * RW_HEAD]
        ath[p, h] = mask(h, pre[p]["at_hi"])
    sh = 2
    while sh < GROUP:
        sq = {it: bdot(npow[it], npow[it]) for it in items}
        for it in items:
            npow[it] = sq[it].astype(BF16)
        for it in items:
            tm1[it] = tm1[it] + sq[it] + bdot(tm1[it].astype(BF16), npow[it])
        sh *= 2
    tm1 = {it: tm1[it].astype(BF16) for it in items}
    tmk = {it: (mk[it] + bdot(tm1[it], mk[it].astype(BF16))).astype(BF16) for it in items}
    corr = {it: bdot(tm1[it], ath[it]).astype(BF16) for it in items}
    vt = {it: vh[it].T.astype(BF16) for it in items}
    vh = {it: vh[it].astype(BF16) for it in items}
    c1r = {it: bdot(tmk[it], vh[it]) for it in items}
    c1c = {it: ntdot(vt[it], tmk[it]) for it in items}
    for p, h in items:
        vv_ref[p, h] = vh[p, h]
    for p in pairs:
        d = pre[p]
        for g in range(ng):
            rows = slice(g * GROUP, (g + 1) * GROUP)
            both = lambda f: jnp.concatenate([f(0), f(1)], axis=0)
            w4_ref[p, g] = jnp.concatenate(
                [both(lambda h: ath[p, h][rows]), both(lambda h: mask(h, d["at_lo"])[rows]),
                 both(lambda h: corr[p, h][rows]), both(lambda h: ath[p, h][rows])], axis=1)
            rt_ref[p, g] = both(lambda h: mask(h, d["rt"])[rows])
            bhh_ref[p, g] = both(lambda h: mask(h, d["bh_hi"])[rows])
            bhl_ref[p, g] = both(lambda h: mask(h, d["bh_lo"])[rows])
            kh_ref[p, g] = both(lambda h: mask(h, d["kh"])[rows])
            vc_ref[p, g] = jnp.concatenate([vt[p, 0][:, rows], vt[p, 1][:, rows]], axis=1)
            c1c_ref[p, g] = jnp.concatenate([c1c[p, 0][:, rows], c1c[p, 1][:, rows]], axis=1)
            c1r_ref[p, g] = both(lambda h: c1r[p, h][rows])
            p8_ref[p, g] = d["p8"][g * GROUP:g * GROUP + SUBLANES]

    def group_step(g, carry):
        r0 = pl.multiple_of(g * GROUP, GROUP)
        s0 = [s_ref[p] for p in pairs]
        sp = [_split_bf16(s) for s in s0]
        s4 = [jnp.concatenate([hi, hi, hi, lo_], axis=1) for hi, lo_ in sp]
        zc = [ntdot(s4[p], w4_ref[p, g]) + c1c_ref[p, g] for p in pairs]
        vk = [bdot(vc_ref[p, g], kh_ref[p, g]) for p in pairs]
        zs = [_split_bf16(z) for z in zc]
        for p in pairs:
            z_hi, z_lo = zs[p]
            upd = (bdot(z_hi, bhh_ref[p, g]) + bdot(z_lo, bhh_ref[p, g])
                   + bdot(z_hi, bhl_ref[p, g]) + vk[p])
            s_ref[p] = s0[p] * p8_ref[p, g][0:1, :] + upd
        for p in pairs:
            zr = ntdot(w4_ref[p, g], s4[p]) + c1r_ref[p, g]
            yr = ntdot(rt_ref[p, g], sp[p][0])
            for h in range(2):
                z_ref[p, h, pl.ds(r0, GROUP), :] = zr[h * GROUP:(h + 1) * GROUP].astype(BF16)
                ys_ref[p, h, pl.ds(r0, GROUP), :] = yr[h * GROUP:(h + 1) * GROUP]
        return carry

    lax.fori_loop(0, ng, group_step, 0)

    ys = {it: ys_ref[it] + bdot(arb_ref[it], z_ref[it]) + bdot(ark_ref[it], vv_ref[it])
          for it in items}
    for p in pairs:
        y_ref[:, p * LANES:(p + 1) * LANES] = jnp.concatenate([ys[p, 0], ys[p, 1]], axis=1)


def _rw_scan_chunked(r, lw, k, v, a, b):
    t = r.shape[0]
    ng = SCAN_T // GROUP
    g2 = 2 * GROUP
    blk = pl.BlockSpec((SCAN_T, RW_DIM), lambda i: (i, 0))
    per_head = lambda n, dt: pltpu.VMEM((HEAD_PAIRS, 2, SCAN_T, n), dt)
    per_group = lambda rows, n, dt: pltpu.VMEM((HEAD_PAIRS, ng, rows, n), dt)
    return pl.pallas_call(
        _rw_chunk_body,
        grid=(t // SCAN_T,),
        in_specs=[blk] * 6,
        out_specs=blk,
        out_shape=jax.ShapeDtypeStruct((t, RW_DIM), F32),
        scratch_shapes=[pltpu.VMEM((HEAD_PAIRS, RW_HEAD, LANES), F32),
                        per_group(g2, 4 * LANES, BF16), per_group(g2, LANES, BF16),
                        per_group(g2, LANES, BF16), per_group(g2, LANES, BF16),
                        per_group(g2, LANES, BF16), per_group(RW_HEAD, g2, BF16),
                        per_group(RW_HEAD, g2, F32), per_group(g2, RW_HEAD, F32),
                        per_group(SUBLANES, LANES, F32),
                        per_head(RW_HEAD, BF16), per_head(RW_HEAD, F32), per_head(RW_HEAD, BF16),
                        per_head(SCAN_T, BF16), per_head(SCAN_T, BF16)],
        compiler_params=_cparams(("arbitrary",)),
        name="rwkv_scan",
    )(r, lw, k, v, a, b)


def _rw_post_body(y_ref, r_ref, k_ref, v_ref, g_ref, lw_ref, lb_ref, rk_ref, bd_ref, o_ref):
    bd = bd_ref[...]
    y = y_ref[...]
    inv_n = 1.0 / RW_HEAD
    d = y - _group_sum(y, bd) * inv_n
    var = _group_sum(d * d, bd) * inv_n
    yn = d * lax.rsqrt(var + GN_EPS) * lw_ref[...] + lb_ref[...]
    bonus = _group_sum(r_ref[...] * k_ref[...] * rk_ref[...], bd) * v_ref[...]
    o_ref[...] = ((yn + bonus) * g_ref[...]).astype(o_ref.dtype)


def _rw_post(y, r, k, v, g, lnx_w, lnx_b, r_k, ones_bd):
    t = y.shape[0]
    tm = min(256, t)
    blk = pl.BlockSpec((tm, RW_DIM), lambda i: (i, 0))
    row = pl.BlockSpec((1, RW_DIM), lambda i: (0, 0))
    return pl.pallas_call(
        _rw_post_body,
        grid=(t // tm,),
        in_specs=[blk] * 5 + [row] * 3 + [pl.BlockSpec(ones_bd.shape, lambda i: (0, 0))],
        out_specs=blk,
        out_shape=jax.ShapeDtypeStruct((t, RW_DIM), BF16),
        compiler_params=_cparams(("arbitrary",)),
        name="rwkv_post",
    )(y, r, k, v, g, lnx_w, lnx_b, r_k, ones_bd)


def _rms(x, n):
    return x * lax.rsqrt(jnp.sum(x * x, axis=-1, keepdims=True) * (1.0 / n) + RMS_EPS)


def _mla_prep_body(p_ref, cos_ref, sin_ref, qn_ref, kvn_ref, wq_ref, wk_ref, wv_ref,
                   qnn_ref, qnr_ref, knn_ref, knr_ref, q_out, k_out, v_out):
    p = p_ref[...]
    cos = cos_ref[...]
    sin = sin_ref[...]
    scale = (QK_NOPE + QK_ROPE) ** -0.5 * LOG2_E

    def rope(x):
        return x * cos + pltpu.roll(x, ROPE_PAD // 2, axis=1) * sin

    cq = _rms(p[:, :Q_LORA], Q_LORA) * qn_ref[...]
    ckv = _rms(p[:, Q_LORA:Q_LORA + KV_LORA], KV_LORA) * kvn_ref[...]
    qf = _bdot(cq, wq_ref[...])
    kn = _bdot(ckv, wk_ref[...])
    v_out[...] = _bdot(ckv, wv_ref[...]).astype(v_out.dtype)
    kr = rope(_rms(p[:, Q_LORA + KV_LORA:], QK_ROPE) * knr_ref[...]).astype(k_out.dtype)
    for h in range(MLA_HEADS):
        o = h * QK_PAD
        qn = _rms(qf[:, o:o + QK_NOPE], QK_NOPE) * qnn_ref[...]
        qr = rope(_rms(qf[:, o + QK_NOPE:o + QK_PAD], QK_ROPE) * qnr_ref[...])
        q_out[:, o:o + QK_NOPE] = (qn * scale).astype(q_out.dtype)
        q_out[:, o + QK_NOPE:o + QK_PAD] = (qr * scale).astype(q_out.dtype)
        kh = _rms(kn[:, h * QK_NOPE:(h + 1) * QK_NOPE], QK_NOPE) * knn_ref[...]
        k_out[:, o:o + QK_NOPE] = kh.astype(k_out.dtype)
        k_out[:, o + QK_NOPE:o + QK_PAD] = kr


def _mla_prep(p_mla, cos, sin, q_norm, kv_norm, wq, wk, wv, qnn, qnr, knn, knr):
    t = p_mla.shape[0]
    tm = min(256, t)
    row = lambda a: pl.BlockSpec(a.shape, lambda i: (0, 0))
    blk = lambda n: pl.BlockSpec((tm, n), lambda i: (i, 0))
    return pl.pallas_call(
        _mla_prep_body,
        grid=(t // tm,),
        in_specs=[blk(N_MLA_PAD), blk(ROPE_PAD), blk(ROPE_PAD), row(q_norm), row(kv_norm),
                  row(wq), row(wk), row(wv), row(qnn), row(qnr), row(knn), row(knr)],
        out_specs=[blk(MLA_HEADS * QK_PAD), blk(MLA_HEADS * QK_PAD), blk(MLA_DIM)],
        out_shape=[jax.ShapeDtypeStruct((t, MLA_HEADS * QK_PAD), BF16),
                   jax.ShapeDtypeStruct((t, MLA_HEADS * QK_PAD), BF16),
                   jax.ShapeDtypeStruct((t, MLA_DIM), BF16)],
        compiler_params=_cparams(("arbitrary",)),
        name="mla_prep",
    )(p_mla, cos, sin, q_norm, kv_norm, wq, wk, wv, qnn, qnr, knn, knr)


ATT_Q = 1024
ATT_HALF = ATT_Q // 2
ATT_K = 1024


def _attn_update(q, k, v, m_ref, l_ref, acc_ref, half, col_shift):
    s = lax.dot_general(q, k, (((1,), (1,)), ((), ())), preferred_element_type=F32)
    if col_shift is not None:
        rows = lax.broadcasted_iota(jnp.int32, s.shape, 0)
        cols = lax.broadcasted_iota(jnp.int32, s.shape, 1)
        s = jnp.where(cols <= rows + col_shift, s, -jnp.inf)
    tiles = [s[:, c * LANES:(c + 1) * LANES] for c in range(s.shape[1] // LANES)]
    mx = functools.reduce(jnp.maximum, tiles)
    m_old = m_ref[half]
    m_new = jnp.maximum(m_old, jnp.max(mx, axis=1, keepdims=True))
    alpha = jnp.exp2(m_old - m_new)
    ps = [jnp.exp2(x - m_new) for x in tiles]
    l_ref[half] = alpha * l_ref[half] + functools.reduce(jnp.add, ps)
    p = jnp.concatenate([x.astype(v.dtype) for x in ps], axis=1)
    acc_ref[half] = alpha * acc_ref[half] + jnp.dot(p, v, preferred_element_type=F32)
    m_ref[half] = m_new


def _attn_body(q_ref, k_ref, v_ref, o_ref, m_ref, l_ref, acc_ref):
    qi = pl.program_id(1)
    m_ref[...] = jnp.full_like(m_ref, -jnp.inf)
    l_ref[...] = jnp.zeros_like(l_ref)
    acc_ref[...] = jnp.zeros_like(acc_ref)
    halves = [(0, slice(0, ATT_HALF)), (1, slice(ATT_HALF, ATT_Q))]

    def full_block(j, carry):
        r0 = pl.multiple_of(j * ATT_K, ATT_K)
        k = k_ref[pl.ds(r0, ATT_K), :]
        v = v_ref[pl.ds(r0, ATT_K), :]
        for half, rows in halves:
            _attn_update(q_ref[rows, :], k, v, m_ref, l_ref, acc_ref, half, None)
        return carry

    lax.fori_loop(0, qi * (ATT_Q // ATT_K), full_block, 0)

    d0 = pl.multiple_of(qi * ATT_Q, ATT_Q)
    _attn_update(q_ref[:ATT_HALF, :], k_ref[pl.ds(d0, ATT_HALF), :], v_ref[pl.ds(d0, ATT_HALF), :],
                 m_ref, l_ref, acc_ref, 0, 0)
    _attn_update(q_ref[ATT_HALF:, :], k_ref[pl.ds(d0, ATT_Q), :], v_ref[pl.ds(d0, ATT_Q), :],
                 m_ref, l_ref, acc_ref, 1, ATT_HALF)
    for half, rows in halves:
        denom = jnp.sum(l_ref[half], axis=1, keepdims=True)
        o_ref[rows, :] = (acc_ref[half] / denom).astype(o_ref.dtype)


def _attention(q, k, v):
    t = q.shape[0]
    assert t % ATT_Q == 0 and ATT_Q % ATT_K == 0
    stat = pltpu.VMEM((2, ATT_HALF, LANES), F32)
    return pl.pallas_call(
        _attn_body,
        grid=(MLA_HEADS, t // ATT_Q),
        in_specs=[pl.BlockSpec((ATT_Q, QK_PAD), lambda h, i: (i, h)),
                  pl.BlockSpec((t, QK_PAD), lambda h, i: (0, h)),
                  pl.BlockSpec((t, V_HEAD), lambda h, i: (0, h))],
        out_specs=pl.BlockSpec((ATT_Q, V_HEAD), lambda h, i: (i, h)),
        out_shape=jax.ShapeDtypeStruct((t, MLA_DIM), BF16),
        scratch_shapes=[stat, stat, pltpu.VMEM((2, ATT_HALF, V_HEAD), F32)],
        compiler_params=_cparams(("arbitrary", "arbitrary")),
        name="mla_attention",
    )(q, k, v)


def _sg_body(p_ref, lw_ref, lb_ref, ws_ref, bs_ref, o_ref):
    tm = p_ref.shape[0]
    u = _gelu_tanh(p_ref[:, :SG_DIM])
    gv = _gelu_tanh(p_ref[:, SG_DIM:])
    mu = jnp.mean(gv, axis=-1, keepdims=True)
    d = gv - mu
    var = jnp.mean(d * d, axis=-1, keepdims=True)
    vn = (d * lax.rsqrt(var + LN_EPS) * lw_ref[...] + lb_ref[...]).astype(BF16)
    rows = lax.broadcasted_iota(jnp.int32, (CHUNK, CHUNK), 0)
    cols = lax.broadcasted_iota(jnp.int32, (CHUNK, CHUNK), 1)
    tril = cols <= rows
    for g in range(SG_GROUPS):
        gs = slice(g * SG_GROUP_DIM, (g + 1) * SG_GROUP_DIM)
        wg = jnp.where(tril, ws_ref[g], 0.0).astype(BF16)
        for c in range(tm // CHUNK):
            cs = slice(c * CHUNK, (c + 1) * CHUNK)
            s = jnp.dot(wg, vn[cs, gs], preferred_element_type=F32) + bs_ref[:, gs]
            o_ref[cs, gs] = (u[cs, gs] * s).astype(o_ref.dtype)


def _spatial_gating(p_sg, ln_w, ln_b, ws, bs_full):
    t = p_sg.shape[0]
    tm = min(256, t)
    row = pl.BlockSpec((1, SG_DIM), lambda i: (0, 0))
    return pl.pallas_call(
        _sg_body,
        grid=(t // tm,),
        in_specs=[pl.BlockSpec((tm, 2 * SG_DIM), lambda i: (i, 0)), row, row,
                  pl.BlockSpec(ws.shape, lambda i: (0, 0, 0)),
                  pl.BlockSpec(bs_full.shape, lambda i: (0, 0))],
        out_specs=pl.BlockSpec((tm, SG_DIM), lambda i: (i, 0)),
        out_shape=jax.ShapeDtypeStruct((t, SG_DIM), BF16),
        compiler_params=_cparams(("arbitrary",)),
        name="spatial_gating",
    )(p_sg, ln_w, ln_b, ws, bs_full)


def _merge_body(ya_ref, yb_ref, yc_ref, wa_ref, wb_ref, wc_ref, ga_ref, gb_ref, gc_ref, o_ref):
    m = ga_ref[...].astype(F32) * jnp.dot(ya_ref[...], wa_ref[...], preferred_element_type=F32)
    m += gb_ref[...].astype(F32) * jnp.dot(yb_ref[...], wb_ref[...], preferred_element_type=F32)
    m += gc_ref[...].astype(F32) * jnp.dot(yc_ref[...], wc_ref[...], preferred_element_type=F32)
    o_ref[...] = m.astype(o_ref.dtype)


def _merge(ya, yb, yc, wa, wb, wc, gates):
    t = ya.shape[0]
    d = wa.shape[1]
    tm = min(512, t)
    tn = 1024
    nj = d // tn
    yblk = pl.BlockSpec((tm, ya.shape[1]), lambda i, j: (i, 0))
    wblk = pl.BlockSpec((wa.shape[0], tn), lambda i, j: (0, j))
    gblk = lambda b: pl.BlockSpec((tm, tn), lambda i, j: (i, b * nj + j))
    return pl.pallas_call(
        _merge_body,
        grid=(t // tm, nj),
        in_specs=[yblk, yblk, yblk, wblk, wblk, wblk, gblk(0), gblk(1), gblk(2)],
        out_specs=pl.BlockSpec((tm, tn), lambda i, j: (i, j)),
        out_shape=jax.ShapeDtypeStruct((t, d), BF16),
        compiler_params=_cparams(("arbitrary", "arbitrary")),
        name="branch_merge",
    )(ya, yb, yc, wa, wb, wc, gates, gates, gates)


CONV_CARRY = 8


def _ffn_up_body(h_ref, wg_ref, wv_ref, cwg_ref, cwv_ref, cbg_ref, cbv_ref, o_ref,
                 cg_ref, cv_ref):
    i = pl.program_id(1)

    @pl.when(i == 0)
    def _():
        cg_ref[...] = jnp.zeros_like(cg_ref)
        cv_ref[...] = jnp.zeros_like(cv_ref)

    h = h_ref[...]
    tm = h.shape[0]
    rowid = lax.broadcasted_iota(jnp.int32, (tm, 1), 0)

    def conv(w_ref, cw_ref, cb_ref, carry_ref):
        up = jnp.dot(h, w_ref[...], preferred_element_type=F32)
        c1 = carry_ref[CONV_CARRY - 1:CONV_CARRY, :]
        c2 = carry_ref[CONV_CARRY - 2:CONV_CARRY - 1, :]
        s1 = jnp.where(rowid == 0, c1, pltpu.roll(up, 1, axis=0))
        s2 = jnp.where(rowid == 0, c2, jnp.where(rowid == 1, c1, pltpu.roll(up, 2, axis=0)))
        carry_ref[...] = up[tm - CONV_CARRY:, :]
        return cb_ref[...] + cw_ref[0:1, :] * s2 + cw_ref[1:2, :] * s1 + cw_ref[2:3, :] * up

    gate = conv(wg_ref, cwg_ref, cbg_ref, cg_ref)
    val = conv(wv_ref, cwv_ref, cbv_ref, cv_ref)
    o_ref[...] = (gate * _sigmoid(gate) * val).astype(o_ref.dtype)


def _ffn_up(h, w_up, conv_w, conv_b):
    t, d = h.shape
    tm = min(1024, t)
    tn = 512
    nj = D_FF // tn
    return pl.pallas_call(
        _ffn_up_body,
        grid=(nj, t // tm),
        in_specs=[pl.BlockSpec((tm, d), lambda j, i: (i, 0)),
                  pl.BlockSpec((d, tn), lambda j, i: (0, j)),
                  pl.BlockSpec((d, tn), lambda j, i: (0, nj + j)),
                  pl.BlockSpec((CONV_W, tn), lambda j, i: (0, j)),
                  pl.BlockSpec((CONV_W, tn), lambda j, i: (0, nj + j)),
                  pl.BlockSpec((1, tn), lambda j, i: (0, j)),
                  pl.BlockSpec((1, tn), lambda j, i: (0, nj + j))],
        out_specs=pl.BlockSpec((tm, tn), lambda j, i: (i, j)),
        out_shape=jax.ShapeDtypeStruct((t, D_FF), BF16),
        scratch_shapes=[pltpu.VMEM((CONV_CARRY, tn), F32), pltpu.VMEM((CONV_CARRY, tn), F32)],
        compiler_params=_cparams(("arbitrary", "arbitrary")),
        name="ffn_up_conv",
    )(h, w_up, w_up, conv_w, conv_w, conv_b, conv_b)


def _pad_cols(a, n):
    return jnp.pad(a, ((0, 0), (0, n - a.shape[1])))


def _rw_cols(a):
    o = 3 * RW_DIM
    return jnp.concatenate([a[:, :o], _pad_cols(a[:, o:o + W_LORA], LORA_PAD),
                            _pad_cols(a[:, o + W_LORA:o + W_LORA + A_LORA], LORA_PAD),
                            a[:, o + W_LORA + A_LORA:N_RW]], axis=1)


def _pad_rows(a, n):
    return jnp.pad(a, ((0, n - a.shape[0]), (0, 0)))


def kernel(x, c, positions, ada_w, ada_b, norm_mix_g, norm_ffn_g, w_in, rw_mu, rw_w0, rw_w2, rw_a0, rw_a2, rw_g2, rw_kk, rw_ka, rw_rk, rw_lnx_w, rw_lnx_b, rw_v0, rw_v1, rw_v2, mla_q_norm, mla_kv_norm, mla_w_uq, mla_w_ukv, mla_qn_nope, mla_qn_rope, mla_kn_nope, mla_kn_rope, sg_ln_w, sg_ln_b, sg_ws, sg_b, w_br_a, w_br_b, w_br_c, w_out, ffn_up, ffn_conv, ffn_conv_b, ffn_down):
    b_, t, d = x.shape
    assert b_ == 1 and d == D_MODEL
    depth = w_in.shape[0]
    xs = x.reshape(t, d)

    mod = _ada_all(c, ada_w, ada_b)
    cos, sin = _rope_tables(positions, t)
    head_id = jnp.arange(RW_DIM) // RW_HEAD
    ones_bd = (head_id[:, None] == head_id[None, :]).astype(BF16)
    row = lambda a: a.reshape(1, -1)

    v_first = None
    for l in range(depth):
        sh1, sc1, gt1, sh2, sc2, gt2 = [mod[l, :, i * d:(i + 1) * d] for i in range(6)]

        h = _norm_mod(xs, row(norm_mix_g[l]), sc1, sh1)
        wl = w_in[l]
        o_dq = N_RW
        o_kr = N_RW + Q_LORA + KV_LORA
        o_sg = o_kr + QK_ROPE
        o_gt = o_sg + 2 * SG_DIM
        w_rw = _rw_cols(wl).astype(BF16)
        w_mla = jnp.concatenate([wl[:, o_dq:o_kr], _rope_pad(wl[:, o_kr:o_sg])], axis=1).astype(BF16)
        p_rw = _matmul(h, w_rw, out_dtype=F32, tm=1024, tn=512, name="proj_rw")
        p_mla = _matmul(h, w_mla, out_dtype=F32, tm=1024, tn=N_MLA_PAD, name="proj_mla")
        p_sg = _matmul(h, wl[:, o_sg:o_gt].astype(BF16), out_dtype=F32, tm=1024, tn=1024,
                       name="proj_sg")
        gates = _matmul(h, wl[:, o_gt:].astype(BF16), out_dtype=BF16, tm=1024, tn=1024,
                        epi="sigmoid", name="proj_gates")

        vres = None
        if l > 0:
            vres = (v_first, row(rw_v0[l - 1]), rw_v1[l - 1].astype(BF16),
                    rw_v2[l - 1].astype(BF16))
        r_, w_, k_, v_, a_, b2_, g_ = _rw_prep(
            p_rw, _rw_cols(row(rw_mu[l])), row(rw_w0[l]),
            _pad_rows(rw_w2[l], LORA_PAD).astype(BF16), row(rw_a0[l]),
            _pad_rows(rw_a2[l], LORA_PAD).astype(BF16), rw_g2[l].astype(BF16),
            row(rw_kk[l]), row(rw_ka[l]), ones_bd, vres)
        if l == 0:
            v_first = v_
        y_scan = _rw_scan_chunked(r_, w_, k_, v_, a_, b2_)
        y_a = _rw_post(y_scan, r_, k_, v_, g_, row(rw_lnx_w[l]), row(rw_lnx_b[l]),
                       row(rw_rk[l]), ones_bd)

        wq = mla_w_uq[l].reshape(Q_LORA, MLA_HEADS, QK_NOPE + QK_ROPE)
        wq = jnp.concatenate([wq[..., :QK_NOPE], _rope_pad(wq[..., QK_NOPE:])], axis=-1)
        wq = wq.reshape(Q_LORA, MLA_HEADS * QK_PAD).astype(BF16)
        wkv = mla_w_ukv[l].reshape(KV_LORA, MLA_HEADS, QK_NOPE + V_HEAD)
        wk = wkv[..., :QK_NOPE].reshape(KV_LORA, MLA_HEADS * QK_NOPE).astype(BF16)
        wv = wkv[..., QK_NOPE:].reshape(KV_LORA, MLA_DIM).astype(BF16)
        q, k, v = _mla_prep(p_mla, cos, sin, row(mla_q_norm[l]), row(mla_kv_norm[l]), wq, wk, wv,
                            row(mla_qn_nope[l]), row(_rope_pad(mla_qn_rope[l])),
                            row(mla_kn_nope[l]), row(_rope_pad(mla_kn_rope[l])))
        y_b = _attention(q, k, v)

        bs_full = jnp.repeat(sg_b[l].T, SG_GROUP_DIM, axis=1)
        y_c = _spatial_gating(p_sg, row(sg_ln_w[l]), row(sg_ln_b[l]), sg_ws[l], bs_full)

        merged = _merge(y_a, y_b, y_c, w_br_a[l].astype(BF16), w_br_b[l].astype(BF16),
                        w_br_c[l].astype(BF16), gates)
        xs = _matmul(merged, w_out[l].astype(BF16), out_dtype=F32, tm=1024, tn=1024,
                     epi="residual", res=xs, gate=gt1, name="out_proj")

        h = _norm_mod(xs, row(norm_ffn_g[l]), sc2, sh2)
        act = _ffn_up(h, ffn_up[l].astype(BF16), ffn_conv[l], row(ffn_conv_b[l]))
        xs = _matmul(act, ffn_down[l].astype(BF16), out_dtype=F32, tm=1024, tn=1024,
                     tk=D_FF // 2, epi="residual", res=xs, gate=gt2, name="ffn_down")

    return xs.reshape(b_, t, d)
```

```python
import functools

import jax
import jax.numpy as jnp
from jax import lax
from jax.experimental import pallas as pl
from jax.experimental.pallas import tpu as pltpu

F32 = jnp.float32
BF16 = jnp.bfloat16

D_MODEL = 2048
RW_HEAD = 64
RW_HEADS = 16
RW_DIM = RW_HEADS * RW_HEAD
W_LORA = 96
A_LORA = 96
V_LORA = 64
G_LORA = 256
GN_EPS = 64e-5
MLA_HEADS = 8
Q_LORA = 512
KV_LORA = 512
QK_NOPE = 128
QK_ROPE = 64
V_HEAD = 128
MLA_DIM = MLA_HEADS * V_HEAD
ROPE_THETA = 10000.0
CHUNK = 128
SG_GROUPS = 8
SG_GROUP_DIM = 128
SG_DIM = SG_GROUPS * SG_GROUP_DIM
D_FF = 5632
CONV_W = 3
RMS_EPS = 1e-6
LN_EPS = 1e-5

LANES = 128
SUBLANES = 8
LORA_PAD = 128
N_RW = 3 * RW_DIM + W_LORA + A_LORA + G_LORA
N_RW_PAD = 3 * RW_DIM + 2 * LORA_PAD + G_LORA
ROPE_PAD = 128
QK_PAD = QK_NOPE + ROPE_PAD
N_MLA_PAD = Q_LORA + KV_LORA + ROPE_PAD
VMEM_LIMIT = 52 * 1024 * 1024
LOG2_E = 1.4426950408889634


def _cparams(sem):
    return pltpu.CompilerParams(dimension_semantics=sem, vmem_limit_bytes=VMEM_LIMIT)


def _sigmoid(x):
    return 1.0 / (1.0 + jnp.exp(-x))


def _softplus(x):
    return jnp.maximum(x, 0.0) + jnp.log(1.0 + jnp.exp(-jnp.abs(x)))


def _gelu_tanh(x):
    return 0.5 * x * (1.0 + jnp.tanh(0.7978845608028654 * (x + 0.044715 * (x * x * x))))


def _bdot(a, b):
    return jnp.dot(a.astype(BF16), b, preferred_element_type=F32)


def _bdot16(a, w):
    return jnp.dot(a, w.astype(BF16), preferred_element_type=F32)


def _group_sum(x, ones_bd):
    hi = x.astype(BF16)
    lo = (x - hi.astype(F32)).astype(BF16)
    return (jnp.dot(hi, ones_bd, preferred_element_type=F32)
            + jnp.dot(lo, ones_bd, preferred_element_type=F32))


def _ada_body(c_ref, w_ref, b_ref, o_ref, *, kc):
    d = c_ref.shape[0]
    tn = o_ref.shape[-1]

    def step(i, acc):
        ck = c_ref[pl.ds(i * kc, kc), :]
        sk = ck * _sigmoid(ck)
        wk = w_ref[0, pl.ds(i * kc, kc), :]
        return acc + jnp.sum(wk * sk, axis=0, keepdims=True)

    acc = lax.fori_loop(0, d // kc, step, jnp.zeros((1, tn), F32))
    o_ref[0] = acc + b_ref[0]


def _ada_all(c, ada_w, ada_b):
    nl, d, n = ada_w.shape
    tn = 1024
    out = pl.pallas_call(
        functools.partial(_ada_body, kc=256),
        grid=(nl, n // tn),
        in_specs=[pl.BlockSpec((d, 1), lambda l, j: (0, 0)),
                  pl.BlockSpec((1, d, tn), lambda l, j: (l, 0, j)),
                  pl.BlockSpec((1, 1, tn), lambda l, j: (l, 0, j))],
        out_specs=pl.BlockSpec((1, 1, tn), lambda l, j: (l, 0, j)),
        out_shape=jax.ShapeDtypeStruct((nl, 1, n), F32),
        compiler_params=_cparams(("arbitrary", "arbitrary")),
        name="ada_mod",
    )(c.reshape(d, 1), ada_w, ada_b.reshape(nl, 1, n))
    return out


def _rope_body(pos_ref, inv_ref, msk_ref, sgn_ref, cos_ref, sin_ref):
    ang = pos_ref[...].astype(F32) * inv_ref[...]
    cos_ref[...] = jnp.cos(ang) * msk_ref[...]
    sin_ref[...] = jnp.sin(ang) * sgn_ref[...]


def _rope_pad(v):
    h = QK_ROPE // 2
    z = jnp.zeros(v.shape[:-1] + (ROPE_PAD // 2 - h,), v.dtype)
    return jnp.concatenate([v[..., :h], z, v[..., h:], z], axis=-1)


def _rope_tables(positions, t):
    tm = min(1024, t)
    inv = ROPE_THETA ** (-jnp.arange(0, QK_ROPE, 2, dtype=F32) / QK_ROPE)
    ones = jnp.ones((QK_ROPE // 2,), F32)
    inv_p = _rope_pad(jnp.concatenate([inv, inv]))[None]
    msk_p = _rope_pad(jnp.concatenate([ones, ones]))[None]
    sgn_p = _rope_pad(jnp.concatenate([-ones, ones]))[None]
    row = pl.BlockSpec((1, ROPE_PAD), lambda i: (0, 0))
    blk = pl.BlockSpec((tm, ROPE_PAD), lambda i: (i, 0))
    return pl.pallas_call(
        _rope_body,
        grid=(t // tm,),
        in_specs=[pl.BlockSpec((tm, 1), lambda i: (i, 0)), row, row, row],
        out_specs=[blk, blk],
        out_shape=[jax.ShapeDtypeStruct((t, ROPE_PAD), F32)] * 2,
        compiler_params=_cparams(("arbitrary",)),
        name="rope_tables",
    )(positions.reshape(t, 1), inv_p, msk_p, sgn_p)


def _norm_mod_body(x_ref, g_ref, sc_ref, sh_ref, o_ref):
    x = x_ref[...]
    y = x * lax.rsqrt(jnp.mean(x * x, axis=-1, keepdims=True) + RMS_EPS) * g_ref[...]
    o_ref[...] = (y * (1.0 + sc_ref[...]) + sh_ref[...]).astype(o_ref.dtype)


def _norm_mod(x, g, sc, sh):
    t, d = x.shape
    tm = min(512, t)
    row = pl.BlockSpec((1, d), lambda i: (0, 0))
    return pl.pallas_call(
        _norm_mod_body,
        grid=(t // tm,),
        in_specs=[pl.BlockSpec((tm, d), lambda i: (i, 0)), row, row, row],
        out_specs=pl.BlockSpec((tm, d), lambda i: (i, 0)),
        out_shape=jax.ShapeDtypeStruct((t, d), BF16),
        compiler_params=_cparams(("arbitrary",)),
        name="norm_mod",
    )(x, g, sc, sh)


def _mm_body(*refs, nk, epi):
    if epi == "residual":
        a_ref, b_ref, res_ref, gt_ref, o_ref = refs[:5]
        rest = refs[5:]
    else:
        a_ref, b_ref, o_ref = refs[:3]
        rest = refs[3:]

    def finish(acc):
        if epi == "sigmoid":
            acc = _sigmoid(acc)
        elif epi == "residual":
            acc = res_ref[...] + gt_ref[...] * acc
        o_ref[...] = acc.astype(o_ref.dtype)

    part = jnp.dot(a_ref[...], b_ref[...].astype(BF16), preferred_element_type=F32)
    if nk == 1:
        finish(part)
        return
    acc_ref, = rest
    k = pl.program_id(2)

    @pl.when(k == 0)
    def _():
        acc_ref[...] = part

    @pl.when(k > 0)
    def _():
        acc_ref[...] += part

    @pl.when(k == nk - 1)
    def _():
        finish(acc_ref[...])


def _matmul(a, b, *, out_dtype, tm, tn, tk=None, epi="none", res=None, gate=None, name="matmul",
            layer=None, col0=0, n=None):
    m, kd = a.shape
    n = b.shape[-1] if n is None else n
    tm = min(tm, m)
    tn = min(tn, n)
    tk = kd if tk is None else tk
    nk = kd // tk
    assert m % tm == 0 and n % tn == 0 and kd % tk == 0 and col0 % tn == 0
    if layer is None:
        b_spec = pl.BlockSpec((tk, tn), lambda i, j, k: (k, j))
    else:
        b_spec = pl.BlockSpec((None, tk, tn), lambda i, j, k: (layer, k, col0 // tn + j))
    in_specs = [pl.BlockSpec((tm, tk), lambda i, j, k: (i, k)), b_spec]
    args = [a, b]
    if epi == "residual":
        in_specs += [pl.BlockSpec((tm, tn), lambda i, j, k: (i, j)),
                     pl.BlockSpec((1, tn), lambda i, j, k: (0, j))]
        args += [res, gate]
    scratch = [pltpu.VMEM((tm, tn), F32)] if nk > 1 else []
    return pl.pallas_call(
        functools.partial(_mm_body, nk=nk, epi=epi),
        grid=(m // tm, n // tn, nk),
        in_specs=in_specs,
        out_specs=pl.BlockSpec((tm, tn), lambda i, j, k: (i, j)),
        out_shape=jax.ShapeDtypeStruct((m, n), out_dtype),
        scratch_shapes=scratch,
        compiler_params=_cparams(("arbitrary", "arbitrary", "arbitrary")),
        name=name,
    )(*args)


def _rw_prep_body(*refs, has_vres):
    (p_ref, pprev_ref, mu_ref, w0_ref, w2_ref, a0_ref, a2_ref, g2_ref, kk_ref, ka_ref,
     bd_ref) = refs[:11]
    pos = 11
    if has_vres:
        vf_ref, v0_ref, v1_ref, v2_ref = refs[pos:pos + 4]
        pos += 4
    r_out, w_out, k_out, v_out, a_out, b_out, g_out = refs[pos:pos + 7]

    i = pl.program_id(0)
    p = p_ref[...]
    tm = p.shape[0]
    last = pprev_ref[7:8, :] * (i > 0).astype(F32)
    rowid = lax.broadcasted_iota(jnp.int32, (tm, 1), 0)
    prev = jnp.where(rowid == 0, last, pltpu.roll(p, 1, axis=0))
    pm = p + (prev - p) * mu_ref[...]

    o = 3 * RW_DIM
    r = pm[:, 0:RW_DIM]
    k = pm[:, RW_DIM:2 * RW_DIM]
    v = pm[:, 2 * RW_DIM:o]
    xw = pm[:, o:o + LORA_PAD]
    xa = pm[:, o + LORA_PAD:o + 2 * LORA_PAD]
    xg = pm[:, o + 2 * LORA_PAD:]

    if has_vres:
        mix = _sigmoid(v0_ref[...] + _bdot(_bdot(v, v1_ref[...]), v2_ref[...]))
        v = v + (vf_ref[...] - v) * mix

    w_log = -_softplus(-(w0_ref[...] + _bdot(jnp.tanh(xw), w2_ref[...]))) - 0.5
    log_decay = -jnp.exp(w_log)
    a = _sigmoid(a0_ref[...] + _bdot(xa, a2_ref[...]))
    g = _bdot(_sigmoid(xg), g2_ref[...])

    kk = k * kk_ref[...]
    nrm = jnp.sqrt(_group_sum(kk * kk, bd_ref[...]))
    kk = kk / jnp.maximum(nrm, 1e-12)
    kh = k * (1.0 + (a - 1.0) * ka_ref[...])

    r_out[...] = r
    w_out[...] = log_decay
    k_out[...] = kh
    v_out[...] = v
    a_out[...] = -kk
    b_out[...] = kk * a
    g_out[...] = g


def _rw_prep(p_rw, mu, w0, w2, a0, a2, g2, k_k, k_a, ones_bd, vres):
    t = p_rw.shape[0]
    tm = min(256, t)
    has_vres = vres is not None
    row = lambda n: pl.BlockSpec((1, n), lambda i: (0, 0))
    full = lambda a: pl.BlockSpec(a.shape, lambda i: (0, 0))
    blk = pl.BlockSpec((tm, RW_DIM), lambda i: (i, 0))
    in_specs = [pl.BlockSpec((tm, N_RW_PAD), lambda i: (i, 0)),
                pl.BlockSpec((8, N_RW_PAD), lambda i: (jnp.maximum(i * (tm // 8) - 1, 0), 0)),
                row(N_RW_PAD), row(RW_DIM), full(w2), row(RW_DIM), full(a2), full(g2),
                row(RW_DIM), row(RW_DIM), full(ones_bd)]
    args = [p_rw, p_rw, mu, w0, w2, a0, a2, g2, k_k, k_a, ones_bd]
    if has_vres:
        v_first, v0, v1, v2 = vres
        in_specs += [blk, row(RW_DIM), full(v1), full(v2)]
        args += [v_first, v0, v1, v2]
    return pl.pallas_call(
        functools.partial(_rw_prep_body, has_vres=has_vres),
        grid=(t // tm,),
        in_specs=in_specs,
        out_specs=[blk] * 7,
        out_shape=[jax.ShapeDtypeStruct((t, RW_DIM), F32)] * 7,
        compiler_params=_cparams(("arbitrary",)),
        name="rwkv_prep",
    )(*args)


SCAN_T = 128
HEAD_PAIRS = RW_HEADS // 2
GROUP = 32
NT_DIMS = (((1,), (1,)), ((), ()))
TN_DIMS = (((0,), (0,)), ((), ()))


def _split_bf16(x):
    hi = x.astype(BF16)
    return hi, (x - hi.astype(F32)).astype(BF16)


def _group_cumsum(x, rowmod):
    sh = 1
    while sh < GROUP:
        x = x + jnp.where(rowmod >= sh, pltpu.roll(x, sh, axis=0), 0.0)
        sh *= 2
    return x


def _group_last(x, rowmod):
    n = x.shape[0]
    x = jnp.where(rowmod == GROUP - 1, x, 0.0)
    sh = 1
    while sh < GROUP:
        x = x + jnp.where(rowmod < GROUP - sh, pltpu.roll(x, n - sh, axis=0), 0.0)
        sh *= 2
    return x


def _rw_chunk_body(r_ref, lw_ref, k_ref, v_ref, a_ref, b_ref, y_ref,
                   s_ref, w4_ref, rt_ref, bhh_ref, bhl_ref, kh_ref, vc_ref, c1c_ref, c1r_ref, p8_ref,
                   z_ref, ys_ref, vv_ref, arb_ref, ark_ref):
    @pl.when(pl.program_id(0) == 0)
    def _():
        s_ref[...] = jnp.zeros_like(s_ref)

    tb = r_ref.shape[0]
    ng = tb // GROUP
    pairs = range(HEAD_PAIRS)
    items = [(p, h) for p in pairs for h in range(2)]
    rowmod = lax.broadcasted_iota(jnp.int32, (tb, LANES), 0) & (GROUP - 1)
    lo = lax.broadcasted_iota(jnp.int32, (tb, LANES), 1) < RW_HEAD
    ti = lax.broadcasted_iota(jnp.int32, (tb, tb), 0)
    si = lax.broadcasted_iota(jnp.int32, (tb, tb), 1)
    same = (ti // GROUP) == (si // GROUP)
    strict = same & (si < ti)
    incl = same & (si <= ti)
    bdot = lambda x, y: jnp.dot(x, y, preferred_element_type=F32)
    ntdot = lambda x, y: lax.dot_general(x, y, NT_DIMS, preferred_element_type=F32)

    pre = []
    for p in pairs:
        sl = slice(p * LANES, (p + 1) * LANES)
        lw = lw_ref[:, sl]
        cs = _group_cumsum(lw, rowmod)
        cse = _group_last(cs, rowmod)
        pend = jnp.exp(cse - cs)
        pinv = jnp.exp(-cs)
        at_hi, at_lo = _split_bf16(a_ref[:, sl] * jnp.exp(cs - lw))
        bh_hi, bh_lo = _split_bf16(b_ref[:, sl] * pend)
        pre.append(dict(
            at_hi=at_hi, at_lo=at_lo, bh_hi=bh_hi, bh_lo=bh_lo,
            rt=(r_ref[:, sl] * jnp.exp(cs)).astype(BF16),
            bt=(b_ref[:, sl] * pinv).astype(BF16), kt=(k_ref[:, sl] * pinv).astype(BF16),
            kh=(k_ref[:, sl] * pend).astype(BF16), v=v_ref[:, sl], p8=jnp.exp(cse)))
    zero = jnp.zeros((tb, LANES), BF16)
    grams = []
    for p in pairs:
        d = pre[p]
        lhs = jnp.concatenate([jnp.where(lo, d["at_hi"], zero), jnp.where(lo, zero, d["at_hi"]),
                               jnp.where(lo, d["rt"], zero), jnp.where(lo, zero, d["rt"])], axis=0)
        grams.append(ntdot(lhs, jnp.concatenate([d["bt"], d["kt"]], axis=0)))
    mask = lambda h, x: jnp.where(lo, x, zero) if h == 0 else jnp.where(lo, zero, x)
    n1, mk, npow, tm1, vh, ath = {}, {}, {}, {}, {}, {}
    for p, h in items:
        g = grams[p]
        n1[p, h] = jnp.where(strict, g[h * tb:(h + 1) * tb, :tb], 0.0)
        mk[p, h] = jnp.where(strict, g[h * tb:(h + 1) * tb, tb:], 0.0)
        arb_ref[p, h] = jnp.where(incl, g[(2 + h) * tb:(3 + h) * tb, :tb], 0.0).astype(BF16)
        ark_ref[p, h] = jnp.where(incl, g[(2 + h) * tb:(3 + h) * tb, tb:], 0.0).astype(BF16)
        tm1[p, h] = n1[p, h]
        npow[p, h] = n1[p, h].astype(BF16)
        vh[p, h] = pre[p]["v"][:, h * RW_HEAD:(h + 1) * RW_HEAD]
        ath[p, h] = mask(h, pre[p]["at_hi"])
    sh = 2
    while sh < GROUP:
        sq = {it: bdot(npow[it], npow[it]) for it in items}
        for it in items:
            npow[it] = sq[it].astype(BF16)
        for it in items:
            tm1[it] = tm1[it] + sq[it] + bdot(tm1[it].astype(BF16), npow[it])
        sh *= 2
    tm1 = {it: tm1[it].astype(BF16) for it in items}
    tmk = {it: (mk[it] + bdot(tm1[it], mk[it].astype(BF16))).astype(BF16) for it in items}
    corr = {it: bdot(tm1[it], ath[it]).astype(BF16) for it in items}
    vt = {it: vh[it].T.astype(BF16) for it in items}
    vh = {it: vh[it].astype(BF16) for it in items}
    c1r = {it: bdot(tmk[it], vh[it]) for it in items}
    c1c = {it: ntdot(vt[it], tmk[it]) for it in items}
    for p, h in items:
        vv_ref[p, h] = vh[p, h]
    for p in pairs:
        d = pre[p]
        for g in range(ng):
            rows = slice(g * GROUP, (g + 1) * GROUP)
            both = lambda f: jnp.concatenate([f(0), f(1)], axis=0)
            w4_ref[p, g] = jnp.concatenate(
                [both(lambda h: ath[p, h][rows]), both(lambda h: mask(h, d["at_lo"])[rows]),
                 both(lambda h: corr[p, h][rows]), both(lambda h: ath[p, h][rows])], axis=1)
            rt_ref[p, g] = both(lambda h: mask(h, d["rt"])[rows])
            bhh_ref[p, g] = both(lambda h: mask(h, d["bh_hi"])[rows])
            bhl_ref[p, g] = both(lambda h: mask(h, d["bh_lo"])[rows])
            kh_ref[p, g] = both(lambda h: mask(h, d["kh"])[rows])
            vc_ref[p, g] = jnp.concatenate([vt[p, 0][:, rows], vt[p, 1][:, rows]], axis=1)
            c1c_ref[p, g] = jnp.concatenate([c1c[p, 0][:, rows], c1c[p, 1][:, rows]], axis=1)
            c1r_ref[p, g] = both(lambda h: c1r[p, h][rows])
            p8_ref[p, g] = d["p8"][g * GROUP:g * GROUP + SUBLANES]

    def group_step(g, carry):
        r0 = pl.multiple_of(g * GROUP, GROUP)
        s0 = [s_ref[p] for p in pairs]
        sp = [_split_bf16(s) for s in s0]
        s4 = [jnp.concatenate([hi, hi, hi, lo_], axis=1) for hi, lo_ in sp]
        zc = [ntdot(s4[p], w4_ref[p, g]) + c1c_ref[p, g] for p in pairs]
        vk = [bdot(vc_ref[p, g], kh_ref[p, g]) for p in pairs]
        zs = [_split_bf16(z) for z in zc]
        for p in pairs:
            z_hi, z_lo = zs[p]
            upd = (bdot(z_hi, bhh_ref[p, g]) + bdot(z_lo, bhh_ref[p, g])
                   + bdot(z_hi, bhl_ref[p, g]) + vk[p])
            s_ref[p] = s0[p] * p8_ref[p, g][0:1, :] + upd
        for p in pairs:
            zr = ntdot(w4_ref[p, g], s4[p]) + c1r_ref[p, g]
            yr = ntdot(rt_ref[p, g], sp[p][0])
            for h in range(2):
                z_ref[p, h, pl.ds(r0, GROUP), :] = zr[h * GROUP:(h + 1) * GROUP].astype(BF16)
                ys_ref[p, h, pl.ds(r0, GROUP), :] = yr[h * GROUP:(h + 1) * GROUP]
        return carry

    lax.fori_loop(0, ng, group_step, 0)

    ys = {it: ys_ref[it] + bdot(arb_ref[it], z_ref[it]) + bdot(ark_ref[it], vv_ref[it])
          for it in items}
    for p in pairs:
        y_ref[:, p * LANES:(p + 1) * LANES] = jnp.concatenate([ys[p, 0], ys[p, 1]], axis=1)


def _rw_scan_chunked(r, lw, k, v, a, b):
    t = r.shape[0]
    ng = SCAN_T // GROUP
    g2 = 2 * GROUP
    blk = pl.BlockSpec((SCAN_T, RW_DIM), lambda i: (i, 0))
    per_head = lambda n, dt: pltpu.VMEM((HEAD_PAIRS, 2, SCAN_T, n), dt)
    per_group = lambda rows, n, dt: pltpu.VMEM((HEAD_PAIRS, ng, rows, n), dt)
    return pl.pallas_call(
        _rw_chunk_body,
        grid=(t // SCAN_T,),
        in_specs=[blk] * 6,
        out_specs=blk,
        out_shape=jax.ShapeDtypeStruct((t, RW_DIM), F32),
        scratch_shapes=[pltpu.VMEM((HEAD_PAIRS, RW_HEAD, LANES), F32),
                        per_group(g2, 4 * LANES, BF16), per_group(g2, LANES, BF16),
                        per_group(g2, LANES, BF16), per_group(g2, LANES, BF16),
                        per_group(g2, LANES, BF16), per_group(RW_HEAD, g2, BF16),
                        per_group(RW_HEAD, g2, F32), per_group(g2, RW_HEAD, F32),
                        per_group(SUBLANES, LANES, F32),
                        per_head(RW_HEAD, BF16), per_head(RW_HEAD, F32), per_head(RW_HEAD, BF16),
                        per_head(SCAN_T, BF16), per_head(SCAN_T, BF16)],
        compiler_params=_cparams(("arbitrary",)),
        name="rwkv_scan",
    )(r, lw, k, v, a, b)


def _rw_post_body(y_ref, r_ref, k_ref, v_ref, g_ref, lw_ref, lb_ref, rk_ref, bd_ref, o_ref):
    bd = bd_ref[...]
    y = y_ref[...]
    inv_n = 1.0 / RW_HEAD
    d = y - _group_sum(y, bd) * inv_n
    var = _group_sum(d * d, bd) * inv_n
    yn = d * lax.rsqrt(var + GN_EPS) * lw_ref[...] + lb_ref[...]
    bonus = _group_sum(r_ref[...] * k_ref[...] * rk_ref[...], bd) * v_ref[...]
    o_ref[...] = ((yn + bonus) * g_ref[...]).astype(o_ref.dtype)


def _rw_post(y, r, k, v, g, lnx_w, lnx_b, r_k, ones_bd):
    t = y.shape[0]
    tm = min(256, t)
    blk = pl.BlockSpec((tm, RW_DIM), lambda i: (i, 0))
    row = pl.BlockSpec((1, RW_DIM), lambda i: (0, 0))
    return pl.pallas_call(
        _rw_post_body,
        grid=(t // tm,),
        in_specs=[blk] * 5 + [row] * 3 + [pl.BlockSpec(ones_bd.shape, lambda i: (0, 0))],
        out_specs=blk,
        out_shape=jax.ShapeDtypeStruct((t, RW_DIM), BF16),
        compiler_params=_cparams(("arbitrary",)),
        name="rwkv_post",
    )(y, r, k, v, g, lnx_w, lnx_b, r_k, ones_bd)


def _rms(x, n):
    return x * lax.rsqrt(jnp.sum(x * x, axis=-1, keepdims=True) * (1.0 / n) + RMS_EPS)


def _mla_prep_body(p_ref, cos_ref, sin_ref, qn_ref, kvn_ref, wq_ref, wk_ref, wv_ref,
                   qnn_ref, qnr_ref, knn_ref, knr_ref, q_out, k_out, v_out):
    p = p_ref[...]
    cos = cos_ref[...]
    sin = sin_ref[...]
    scale = (QK_NOPE + QK_ROPE) ** -0.5 * LOG2_E

    def rope(x):
        return x * cos + pltpu.roll(x, ROPE_PAD // 2, axis=1) * sin

    cq = _rms(p[:, :Q_LORA], Q_LORA) * qn_ref[...]
    ckv = _rms(p[:, Q_LORA:Q_LORA + KV_LORA], KV_LORA) * kvn_ref[...]
    qf = _bdot(cq, wq_ref[...])
    kn = _bdot(ckv, wk_ref[...])
    v_out[...] = _bdot(ckv, wv_ref[...]).astype(v_out.dtype)
    kr = rope(_rms(p[:, Q_LORA + KV_LORA:], QK_ROPE) * knr_ref[...]).astype(k_out.dtype)
    for h in range(MLA_HEADS):
        o = h * QK_PAD
        qn = _rms(qf[:, o:o + QK_NOPE], QK_NOPE) * qnn_ref[...]
        qr = rope(_rms(qf[:, o + QK_NOPE:o + QK_PAD], QK_ROPE) * qnr_ref[...])
        q_out[:, o:o + QK_NOPE] = (qn * scale).astype(q_out.dtype)
        q_out[:, o + QK_NOPE:o + QK_PAD] = (qr * scale).astype(q_out.dtype)
        kh = _rms(kn[:, h * QK_NOPE:(h + 1) * QK_NOPE], QK_NOPE) * knn_ref[...]
        k_out[:, o:o + QK_NOPE] = kh.astype(k_out.dtype)
        k_out[:, o + QK_NOPE:o + QK_PAD] = kr


def _mla_prep(p_mla, cos, sin, q_norm, kv_norm, wq, wk, wv, qnn, qnr, knn, knr):
    t = p_mla.shape[0]
    tm = min(256, t)
    row = lambda a: pl.BlockSpec(a.shape, lambda i: (0, 0))
    blk = lambda n: pl.BlockSpec((tm, n), lambda i: (i, 0))
    return pl.pallas_call(
        _mla_prep_body,
        grid=(t // tm,),
        in_specs=[blk(N_MLA_PAD), blk(ROPE_PAD), blk(ROPE_PAD), row(q_norm), row(kv_norm),
                  row(wq), row(wk), row(wv), row(qnn), row(qnr), row(knn), row(knr)],
        out_specs=[blk(MLA_HEADS * QK_PAD), blk(MLA_HEADS * QK_PAD), blk(MLA_DIM)],
        out_shape=[jax.ShapeDtypeStruct((t, MLA_HEADS * QK_PAD), BF16),
                   jax.ShapeDtypeStruct((t, MLA_HEADS * QK_PAD), BF16),
                   jax.ShapeDtypeStruct((t, MLA_DIM), BF16)],
        compiler_params=_cparams(("arbitrary",)),
        name="mla_prep",
    )(p_mla, cos, sin, q_norm, kv_norm, wq, wk, wv, qnn, qnr, knn, knr)


ATT_Q = 1024
ATT_HALF = ATT_Q // 2
ATT_K = 1024


def _attn_update(q, k, v, m_ref, l_ref, acc_ref, half, col_shift):
    s = lax.dot_general(q, k, (((1,), (1,)), ((), ())), preferred_element_type=F32)
    if col_shift is not None:
        rows = lax.broadcasted_iota(jnp.int32, s.shape, 0)
        cols = lax.broadcasted_iota(jnp.int32, s.shape, 1)
        s = jnp.where(cols <= rows + col_shift, s, -jnp.inf)
    tiles = [s[:, c * LANES:(c + 1) * LANES] for c in range(s.shape[1] // LANES)]
    mx = functools.reduce(jnp.maximum, tiles)
    m_old = m_ref[half]
    m_new = jnp.maximum(m_old, jnp.max(mx, axis=1, keepdims=True))
    alpha = jnp.exp2(m_old - m_new)
    ps = [jnp.exp2(x - m_new) for x in tiles]
    l_ref[half] = alpha * l_ref[half] + functools.reduce(jnp.add, ps)
    p = jnp.concatenate([x.astype(v.dtype) for x in ps], axis=1)
    acc_ref[half] = alpha * acc_ref[half] + jnp.dot(p, v, preferred_element_type=F32)
    m_ref[half] = m_new


def _attn_body(q_ref, k_ref, v_ref, o_ref, m_ref, l_ref, acc_ref):
    qi = pl.program_id(1)
    m_ref[...] = jnp.full_like(m_ref, -jnp.inf)
    l_ref[...] = jnp.zeros_like(l_ref)
    acc_ref[...] = jnp.zeros_like(acc_ref)
    halves = [(0, slice(0, ATT_HALF)), (1, slice(ATT_HALF, ATT_Q))]

    def full_block(j, carry):
        r0 = pl.multiple_of(j * ATT_K, ATT_K)
        k = k_ref[pl.ds(r0, ATT_K), :]
        v = v_ref[pl.ds(r0, ATT_K), :]
        for half, rows in halves:
            _attn_update(q_ref[rows, :], k, v, m_ref, l_ref, acc_ref, half, None)
        return carry

    lax.fori_loop(0, qi * (ATT_Q // ATT_K), full_block, 0)

    d0 = pl.multiple_of(qi * ATT_Q, ATT_Q)
    _attn_update(q_ref[:ATT_HALF, :], k_ref[pl.ds(d0, ATT_HALF), :], v_ref[pl.ds(d0, ATT_HALF), :],
                 m_ref, l_ref, acc_ref, 0, 0)
    _attn_update(q_ref[ATT_HALF:, :], k_ref[pl.ds(d0, ATT_Q), :], v_ref[pl.ds(d0, ATT_Q), :],
                 m_ref, l_ref, acc_ref, 1, ATT_HALF)
    for half, rows in halves:
        denom = jnp.sum(l_ref[half], axis=1, keepdims=True)
        o_ref[rows, :] = (acc_ref[half] / denom).astype(o_ref.dtype)


def _attention(q, k, v):
    t = q.shape[0]
    assert t % ATT_Q == 0 and ATT_Q % ATT_K == 0
    stat = pltpu.VMEM((2, ATT_HALF, LANES), F32)
    return pl.pallas_call(
        _attn_body,
        grid=(MLA_HEADS, t // ATT_Q),
        in_specs=[pl.BlockSpec((ATT_Q, QK_PAD), lambda h, i: (i, h)),
                  pl.BlockSpec((t, QK_PAD), lambda h, i: (0, h)),
                  pl.BlockSpec((t, V_HEAD), lambda h, i: (0, h))],
        out_specs=pl.BlockSpec((ATT_Q, V_HEAD), lambda h, i: (i, h)),
        out_shape=jax.ShapeDtypeStruct((t, MLA_DIM), BF16),
        scratch_shapes=[stat, stat, pltpu.VMEM((2, ATT_HALF, V_HEAD), F32)],
        compiler_params=_cparams(("arbitrary", "arbitrary")),
        name="mla_attention",
    )(q, k, v)


def _sg_body(p_ref, lw_ref, lb_ref, ws_ref, bs_ref, o_ref):
    tm = p_ref.shape[0]
    u = _gelu_tanh(p_ref[:, :SG_DIM])
    gv = _gelu_tanh(p_ref[:, SG_DIM:])
    mu = jnp.mean(gv, axis=-1, keepdims=True)
    d = gv - mu
    var = jnp.mean(d * d, axis=-1, keepdims=True)
    vn = (d * lax.rsqrt(var + LN_EPS) * lw_ref[...] + lb_ref[...]).astype(BF16)
    rows = lax.broadcasted_iota(jnp.int32, (CHUNK, CHUNK), 0)
    cols = lax.broadcasted_iota(jnp.int32, (CHUNK, CHUNK), 1)
    tril = cols <= rows
    for g in range(SG_GROUPS):
        gs = slice(g * SG_GROUP_DIM, (g + 1) * SG_GROUP_DIM)
        wg = jnp.where(tril, ws_ref[g], 0.0).astype(BF16)
        for c in range(tm // CHUNK):
            cs = slice(c * CHUNK, (c + 1) * CHUNK)
            s = jnp.dot(wg, vn[cs, gs], preferred_element_type=F32) + bs_ref[:, gs]
            o_ref[cs, gs] = (u[cs, gs] * s).astype(o_ref.dtype)


def _spatial_gating(p_sg, ln_w, ln_b, ws, bs_full):
    t = p_sg.shape[0]
    tm = min(256, t)
    row = pl.BlockSpec((1, SG_DIM), lambda i: (0, 0))
    return pl.pallas_call(
        _sg_body,
        grid=(t // tm,),
        in_specs=[pl.BlockSpec((tm, 2 * SG_DIM), lambda i: (i, 0)), row, row,
                  pl.BlockSpec(ws.shape, lambda i: (0, 0, 0)),
                  pl.BlockSpec(bs_full.shape, lambda i: (0, 0))],
        out_specs=pl.BlockSpec((tm, SG_DIM), lambda i: (i, 0)),
        out_shape=jax.ShapeDtypeStruct((t, SG_DIM), BF16),
        compiler_params=_cparams(("arbitrary",)),
        name="spatial_gating",
    )(p_sg, ln_w, ln_b, ws, bs_full)


def _merge_body(ya_ref, yb_ref, yc_ref, wa_ref, wb_ref, wc_ref, ga_ref, gb_ref, gc_ref, o_ref):
    m = ga_ref[...].astype(F32) * _bdot16(ya_ref[...], wa_ref[...])
    m += gb_ref[...].astype(F32) * _bdot16(yb_ref[...], wb_ref[...])
    m += gc_ref[...].astype(F32) * _bdot16(yc_ref[...], wc_ref[...])
    o_ref[...] = m.astype(o_ref.dtype)


def _merge(ya, yb, yc, wa, wb, wc, layer, gates):
    t = ya.shape[0]
    d = wa.shape[-1]
    tm = min(512, t)
    tn = 512
    nj = d // tn
    yblk = pl.BlockSpec((tm, ya.shape[1]), lambda i, j: (i, 0))
    wblk = pl.BlockSpec((None, wa.shape[1], tn), lambda i, j: (layer, 0, j))
    gblk = lambda b: pl.BlockSpec((tm, tn), lambda i, j: (i, b * nj + j))
    return pl.pallas_call(
        _merge_body,
        grid=(t // tm, nj),
        in_specs=[yblk, yblk, yblk, wblk, wblk, wblk, gblk(0), gblk(1), gblk(2)],
        out_specs=pl.BlockSpec((tm, tn), lambda i, j: (i, j)),
        out_shape=jax.ShapeDtypeStruct((t, d), BF16),
        compiler_params=_cparams(("arbitrary", "arbitrary")),
        name="branch_merge",
    )(ya, yb, yc, wa, wb, wc, gates, gates, gates)


CONV_CARRY = 8


def _ffn_up_body(h_ref, wg_ref, wv_ref, cwg_ref, cwv_ref, cbg_ref, cbv_ref, o_ref,
                 cg_ref, cv_ref, wg16_ref, wv16_ref):
    i = pl.program_id(1)

    @pl.when(i == 0)
    def _():
        cg_ref[...] = jnp.zeros_like(cg_ref)
        cv_ref[...] = jnp.zeros_like(cv_ref)
        wg16_ref[...] = wg_ref[...].astype(BF16)
        wv16_ref[...] = wv_ref[...].astype(BF16)

    h = h_ref[...]
    tm = h.shape[0]
    rowid = lax.broadcasted_iota(jnp.int32, (tm, 1), 0)

    def conv(w_ref, cw_ref, cb_ref, carry_ref):
        up = jnp.dot(h, w_ref[...], preferred_element_type=F32)
        c1 = carry_ref[CONV_CARRY - 1:CONV_CARRY, :]
        c2 = carry_ref[CONV_CARRY - 2:CONV_CARRY - 1, :]
        s1 = jnp.where(rowid == 0, c1, pltpu.roll(up, 1, axis=0))
        s2 = jnp.where(rowid == 0, c2, jnp.where(rowid == 1, c1, pltpu.roll(up, 2, axis=0)))
        carry_ref[...] = up[tm - CONV_CARRY:, :]
        return cb_ref[...] + cw_ref[0:1, :] * s2 + cw_ref[1:2, :] * s1 + cw_ref[2:3, :] * up

    gate = conv(wg16_ref, cwg_ref, cbg_ref, cg_ref)
    val = conv(wv16_ref, cwv_ref, cbv_ref, cv_ref)
    o_ref[...] = (gate * _sigmoid(gate) * val).astype(o_ref.dtype)


def _ffn_up(h, w_up, layer, conv_w, conv_b):
    t, d = h.shape
    tm = min(1024, t)
    tn = 512
    nj = D_FF // tn
    return pl.pallas_call(
        _ffn_up_body,
        grid=(nj, t // tm),
        in_specs=[pl.BlockSpec((tm, d), lambda j, i: (i, 0)),
                  pl.BlockSpec((None, d, tn), lambda j, i: (layer, 0, j)),
                  pl.BlockSpec((None, d, tn), lambda j, i: (layer, 0, nj + j)),
                  pl.BlockSpec((CONV_W, tn), lambda j, i: (0, j)),
                  pl.BlockSpec((CONV_W, tn), lambda j, i: (0, nj + j)),
                  pl.BlockSpec((1, tn), lambda j, i: (0, j)),
                  pl.BlockSpec((1, tn), lambda j, i: (0, nj + j))],
        out_specs=pl.BlockSpec((tm, tn), lambda j, i: (i, j)),
        out_shape=jax.ShapeDtypeStruct((t, D_FF), BF16),
        scratch_shapes=[pltpu.VMEM((CONV_CARRY, tn), F32), pltpu.VMEM((CONV_CARRY, tn), F32),
                        pltpu.VMEM((d, tn), BF16), pltpu.VMEM((d, tn), BF16)],
        compiler_params=_cparams(("arbitrary", "arbitrary")),
        name="ffn_up_conv",
    )(h, w_up, w_up, conv_w, conv_w, conv_b, conv_b)


def _pad_cols(a, n):
    return jnp.pad(a, ((0, 0), (0, n - a.shape[1])))


def _rw_cols(a):
    o = 3 * RW_DIM
    return jnp.concatenate([a[:, :o], _pad_cols(a[:, o:o + W_LORA], LORA_PAD),
                            _pad_cols(a[:, o + W_LORA:o + W_LORA + A_LORA], LORA_PAD),
                            a[:, o + W_LORA + A_LORA:N_RW]], axis=1)


def _pad_rows(a, n):
    return jnp.pad(a, ((0, n - a.shape[0]), (0, 0)))


def kernel(x, c, positions, ada_w, ada_b, norm_mix_g, norm_ffn_g, w_in, rw_mu, rw_w0, rw_w2, rw_a0, rw_a2, rw_g2, rw_kk, rw_ka, rw_rk, rw_lnx_w, rw_lnx_b, rw_v0, rw_v1, rw_v2, mla_q_norm, mla_kv_norm, mla_w_uq, mla_w_ukv, mla_qn_nope, mla_qn_rope, mla_kn_nope, mla_kn_rope, sg_ln_w, sg_ln_b, sg_ws, sg_b, w_br_a, w_br_b, w_br_c, w_out, ffn_up, ffn_conv, ffn_conv_b, ffn_down):
    b_, t, d = x.shape
    assert b_ == 1 and d == D_MODEL
    depth = w_in.shape[0]
    xs = x.reshape(t, d)

    mod = _ada_all(c, ada_w, ada_b)
    cos, sin = _rope_tables(positions, t)
    head_id = jnp.arange(RW_DIM) // RW_HEAD
    ones_bd = (head_id[:, None] == head_id[None, :]).astype(BF16)
    row = lambda a: a.reshape(1, -1)

    v_first = None
    for l in range(depth):
        sh1, sc1, gt1, sh2, sc2, gt2 = [mod[l, :, i * d:(i + 1) * d] for i in range(6)]

        h = _norm_mod(xs, row(norm_mix_g[l]), sc1, sh1)
        wl = w_in[l]
        o_dq = N_RW
        o_kr = N_RW + Q_LORA + KV_LORA
        o_sg = o_kr + QK_ROPE
        o_gt = o_sg + 2 * SG_DIM
        w_rw = _rw_cols(wl).astype(BF16)
        w_mla = jnp.concatenate([wl[:, o_dq:o_kr], _rope_pad(wl[:, o_kr:o_sg])], axis=1).astype(BF16)
        p_rw = _matmul(h, w_rw, out_dtype=F32, tm=1024, tn=512, name="proj_rw")
        p_mla = _matmul(h, w_mla, out_dtype=F32, tm=1024, tn=N_MLA_PAD, name="proj_mla")
        p_sg = _matmul(h, w_in, layer=l, col0=o_sg, n=2 * SG_DIM, out_dtype=F32, tm=2048, tn=512,
                       name="proj_sg")
        gates = _matmul(h, w_in, layer=l, col0=o_gt, n=3 * d, out_dtype=BF16, tm=2048, tn=512,
                        epi="sigmoid", name="proj_gates")

        vres = None
        if l > 0:
            vres = (v_first, row(rw_v0[l - 1]), rw_v1[l - 1].astype(BF16),
                    rw_v2[l - 1].astype(BF16))
        r_, w_, k_, v_, a_, b2_, g_ = _rw_prep(
            p_rw, _rw_cols(row(rw_mu[l])), row(rw_w0[l]),
            _pad_rows(rw_w2[l], LORA_PAD).astype(BF16), row(rw_a0[l]),
            _pad_rows(rw_a2[l], LORA_PAD).astype(BF16), rw_g2[l].astype(BF16),
            row(rw_kk[l]), row(rw_ka[l]), ones_bd, vres)
        if l == 0:
            v_first = v_
        y_scan = _rw_scan_chunked(r_, w_, k_, v_, a_, b2_)
        y_a = _rw_post(y_scan, r_, k_, v_, g_, row(rw_lnx_w[l]), row(rw_lnx_b[l]),
                       row(rw_rk[l]), ones_bd)

        wq = mla_w_uq[l].reshape(Q_LORA, MLA_HEADS, QK_NOPE + QK_ROPE)
        wq = jnp.concatenate([wq[..., :QK_NOPE], _rope_pad(wq[..., QK_NOPE:])], axis=-1)
        wq = wq.reshape(Q_LORA, MLA_HEADS * QK_PAD).astype(BF16)
        wkv = mla_w_ukv[l].reshape(KV_LORA, MLA_HEADS, QK_NOPE + V_HEAD)
        wk = wkv[..., :QK_NOPE].reshape(KV_LORA, MLA_HEADS * QK_NOPE).astype(BF16)
        wv = wkv[..., QK_NOPE:].reshape(KV_LORA, MLA_DIM).astype(BF16)
        q, k, v = _mla_prep(p_mla, cos, sin, row(mla_q_norm[l]), row(mla_kv_norm[l]), wq, wk, wv,
                            row(mla_qn_nope[l]), row(_rope_pad(mla_qn_rope[l])),
                            row(mla_kn_nope[l]), row(_rope_pad(mla_kn_rope[l])))
        y_b = _attention(q, k, v)

        bs_full = jnp.repeat(sg_b[l].T, SG_GROUP_DIM, axis=1)
        y_c = _spatial_gating(p_sg, row(sg_ln_w[l]), row(sg_ln_b[l]), sg_ws[l], bs_full)

        merged = _merge(y_a, y_b, y_c, w_br_a, w_br_b, w_br_c, l, gates)
        xs = _matmul(merged, w_out, layer=l, out_dtype=F32, tm=1024, tn=512,
                     epi="residual", res=xs, gate=gt1, name="out_proj")

        h = _norm_mod(xs, row(norm_ffn_g[l]), sc2, sh2)
        act = _ffn_up(h, ffn_up, l, ffn_conv[l], row(ffn_conv_b[l]))
        xs = _matmul(act, ffn_down, layer=l, out_dtype=F32, tm=1024, tn=512,
                     tk=D_FF // 2, epi="residual", res=xs, gate=gt2, name="ffn_down")

    return xs.reshape(b_, t, d)
```

```python
import functools

import jax
import jax.numpy as jnp
from jax import lax
from jax.experimental import pallas as pl
from jax.experimental.pallas import tpu as pltpu

F32 = jnp.float32
BF16 = jnp.bfloat16

D_MODEL = 2048
RW_HEAD = 64
RW_HEADS = 16
RW_DIM = RW_HEADS * RW_HEAD
W_LORA = 96
A_LORA = 96
V_LORA = 64
G_LORA = 256
GN_EPS = 64e-5
MLA_HEADS = 8
Q_LORA = 512
KV_LORA = 512
QK_NOPE = 128
QK_ROPE = 64
V_HEAD = 128
MLA_DIM = MLA_HEADS * V_HEAD
ROPE_THETA = 10000.0
CHUNK = 128
SG_GROUPS = 8
SG_GROUP_DIM = 128
SG_DIM = SG_GROUPS * SG_GROUP_DIM
D_FF = 5632
CONV_W = 3
RMS_EPS = 1e-6
LN_EPS = 1e-5

LANES = 128
SUBLANES = 8
LORA_PAD = 128
N_RW = 3 * RW_DIM + W_LORA + A_LORA + G_LORA
N_RW_PAD = 3 * RW_DIM + 2 * LORA_PAD + G_LORA
ROPE_PAD = 128
QK_PAD = QK_NOPE + ROPE_PAD
N_MLA_PAD = Q_LORA + KV_LORA + ROPE_PAD
VMEM_LIMIT = 52 * 1024 * 1024
LOG2_E = 1.4426950408889634


def _cparams(sem):
    return pltpu.CompilerParams(dimension_semantics=sem, vmem_limit_bytes=VMEM_LIMIT)


def _sigmoid(x):
    return 1.0 / (1.0 + jnp.exp(-x))


def _softplus(x):
    return jnp.maximum(x, 0.0) + jnp.log(1.0 + jnp.exp(-jnp.abs(x)))


def _gelu_tanh(x):
    return 0.5 * x * (1.0 + jnp.tanh(0.7978845608028654 * (x + 0.044715 * (x * x * x))))


def _bdot(a, b):
    return jnp.dot(a.astype(BF16), b, preferred_element_type=F32)


def _group_sum(x, ones2):
    hi = x.astype(BF16)
    lo = (x - hi.astype(F32)).astype(BF16)
    out = []
    for c in range(x.shape[1] // LANES):
        sl = slice(c * LANES, (c + 1) * LANES)
        out.append(jnp.dot(jnp.concatenate([hi[:, sl], lo[:, sl]], axis=1), ones2,
                           preferred_element_type=F32))
    return jnp.concatenate(out, axis=1)


def _ada_body(c_ref, w_ref, b_ref, o_ref, *, kc):
    d = c_ref.shape[0]
    tn = o_ref.shape[-1]

    def step(i, acc):
        ck = c_ref[pl.ds(i * kc, kc), :]
        sk = ck * _sigmoid(ck)
        wk = w_ref[0, pl.ds(i * kc, kc), :]
        return acc + jnp.sum(wk * sk, axis=0, keepdims=True)

    acc = lax.fori_loop(0, d // kc, step, jnp.zeros((1, tn), F32))
    o_ref[0] = acc + b_ref[0]


def _ada_all(c, ada_w, ada_b):
    nl, d, n = ada_w.shape
    tn = 1024
    out = pl.pallas_call(
        functools.partial(_ada_body, kc=256),
        grid=(nl, n // tn),
        in_specs=[pl.BlockSpec((d, 1), lambda l, j: (0, 0)),
                  pl.BlockSpec((1, d, tn), lambda l, j: (l, 0, j)),
                  pl.BlockSpec((1, 1, tn), lambda l, j: (l, 0, j))],
        out_specs=pl.BlockSpec((1, 1, tn), lambda l, j: (l, 0, j)),
        out_shape=jax.ShapeDtypeStruct((nl, 1, n), F32),
        compiler_params=_cparams(("arbitrary", "arbitrary")),
        name="ada_mod",
    )(c.reshape(d, 1), ada_w, ada_b.reshape(nl, 1, n))
    return out


def _rope_body(pos_ref, inv_ref, msk_ref, sgn_ref, cos_ref, sin_ref):
    ang = pos_ref[...].astype(F32) * inv_ref[...]
    cos_ref[...] = jnp.cos(ang) * msk_ref[...]
    sin_ref[...] = jnp.sin(ang) * sgn_ref[...]


def _rope_pad(v):
    h = QK_ROPE // 2
    z = jnp.zeros(v.shape[:-1] + (ROPE_PAD // 2 - h,), v.dtype)
    return jnp.concatenate([v[..., :h], z, v[..., h:], z], axis=-1)


def _rope_tables(positions, t):
    tm = min(1024, t)
    inv = ROPE_THETA ** (-jnp.arange(0, QK_ROPE, 2, dtype=F32) / QK_ROPE)
    ones = jnp.ones((QK_ROPE // 2,), F32)
    inv_p = _rope_pad(jnp.concatenate([inv, inv]))[None]
    msk_p = _rope_pad(jnp.concatenate([ones, ones]))[None]
    sgn_p = _rope_pad(jnp.concatenate([-ones, ones]))[None]
    row = pl.BlockSpec((1, ROPE_PAD), lambda i: (0, 0))
    blk = pl.BlockSpec((tm, ROPE_PAD), lambda i: (i, 0))
    return pl.pallas_call(
        _rope_body,
        grid=(t // tm,),
        in_specs=[pl.BlockSpec((tm, 1), lambda i: (i, 0)), row, row, row],
        out_specs=[blk, blk],
        out_shape=[jax.ShapeDtypeStruct((t, ROPE_PAD), F32)] * 2,
        compiler_params=_cparams(("arbitrary",)),
        name="rope_tables",
    )(positions.reshape(t, 1), inv_p, msk_p, sgn_p)


def _norm_mod_body(x_ref, g_ref, sc_ref, sh_ref, o_ref):
    x = x_ref[...]
    y = x * lax.rsqrt(jnp.mean(x * x, axis=-1, keepdims=True) + RMS_EPS) * g_ref[...]
    o_ref[...] = (y * (1.0 + sc_ref[...]) + sh_ref[...]).astype(o_ref.dtype)


def _norm_mod(x, g, sc, sh):
    t, d = x.shape
    tm = min(512, t)
    row = pl.BlockSpec((1, d), lambda i: (0, 0))
    return pl.pallas_call(
        _norm_mod_body,
        grid=(t // tm,),
        in_specs=[pl.BlockSpec((tm, d), lambda i: (i, 0)), row, row, row],
        out_specs=pl.BlockSpec((tm, d), lambda i: (i, 0)),
        out_shape=jax.ShapeDtypeStruct((t, d), BF16),
        compiler_params=_cparams(("arbitrary",)),
        name="norm_mod",
    )(x, g, sc, sh)


def _mm_body(*refs, nk, epi):
    if epi == "residual":
        a_ref, b_ref, res_ref, gt_ref, o_ref = refs[:5]
        rest = refs[5:]
    else:
        a_ref, b_ref, o_ref = refs[:3]
        rest = refs[3:]

    def finish(acc):
        if epi == "sigmoid":
            acc = _sigmoid(acc)
        elif epi == "residual":
            acc = res_ref[...] + gt_ref[...] * acc
        o_ref[...] = acc.astype(o_ref.dtype)

    part = jnp.dot(a_ref[...], b_ref[...].astype(BF16), preferred_element_type=F32)
    if nk == 1:
        finish(part)
        return
    acc_ref, = rest
    k = pl.program_id(2)

    @pl.when(k == 0)
    def _():
        acc_ref[...] = part

    @pl.when(k > 0)
    def _():
        acc_ref[...] += part

    @pl.when(k == nk - 1)
    def _():
        finish(acc_ref[...])


def _matmul(a, b, *, out_dtype, tm, tn, tk=None, epi="none", res=None, gate=None, name="matmul",
            layer=None, col0=0, n=None):
    m, kd = a.shape
    n = b.shape[-1] if n is None else n
    tm = min(tm, m)
    tn = min(tn, n)
    tk = kd if tk is None else tk
    nk = kd // tk
    assert m % tm == 0 and n % tn == 0 and kd % tk == 0 and col0 % tn == 0
    if layer is None:
        b_spec = pl.BlockSpec((tk, tn), lambda i, j, k: (k, j))
    else:
        b_spec = pl.BlockSpec((None, tk, tn), lambda i, j, k: (layer, k, col0 // tn + j))
    in_specs = [pl.BlockSpec((tm, tk), lambda i, j, k: (i, k)), b_spec]
    args = [a, b]
    if epi == "residual":
        in_specs += [pl.BlockSpec((tm, tn), lambda i, j, k: (i, j)),
                     pl.BlockSpec((1, tn), lambda i, j, k: (0, j))]
        args += [res, gate]
    scratch = [pltpu.VMEM((tm, tn), F32)] if nk > 1 else []
    return pl.pallas_call(
        functools.partial(_mm_body, nk=nk, epi=epi),
        grid=(m // tm, n // tn, nk),
        in_specs=in_specs,
        out_specs=pl.BlockSpec((tm, tn), lambda i, j, k: (i, j)),
        out_shape=jax.ShapeDtypeStruct((m, n), out_dtype),
        scratch_shapes=scratch,
        compiler_params=_cparams(("arbitrary", "arbitrary", "arbitrary")),
        name=name,
    )(*args)


def _rw_prep_body(*refs, has_vres):
    (p_ref, pprev_ref, mu_ref, w0_ref, w2_ref, a0_ref, a2_ref, g2_ref, kk_ref, ka_ref,
     bd_ref) = refs[:11]
    pos = 11
    if has_vres:
        vf_ref, v0_ref, v1_ref, v2_ref = refs[pos:pos + 4]
        pos += 4
    r_out, w_out, k_out, v_out, a_out, b_out, g_out = refs[pos:pos + 7]

    i = pl.program_id(0)
    p = p_ref[...]
    tm = p.shape[0]
    last = pprev_ref[7:8, :] * (i > 0).astype(F32)
    rowid = lax.broadcasted_iota(jnp.int32, (tm, 1), 0)
    prev = jnp.where(rowid == 0, last, pltpu.roll(p, 1, axis=0))
    pm = p + (prev - p) * mu_ref[...]

    o = 3 * RW_DIM
    r = pm[:, 0:RW_DIM]
    k = pm[:, RW_DIM:2 * RW_DIM]
    v = pm[:, 2 * RW_DIM:o]
    xw = pm[:, o:o + LORA_PAD]
    xa = pm[:, o + LORA_PAD:o + 2 * LORA_PAD]
    xg = pm[:, o + 2 * LORA_PAD:]

    if has_vres:
        mix = _sigmoid(v0_ref[...] + _bdot(_bdot(v, v1_ref[...]), v2_ref[...]))
        v = v + (vf_ref[...] - v) * mix

    w_log = -_softplus(-(w0_ref[...] + _bdot(jnp.tanh(xw), w2_ref[...]))) - 0.5
    log_decay = -jnp.exp(w_log)
    a = _sigmoid(a0_ref[...] + _bdot(xa, a2_ref[...]))
    g = _bdot(_sigmoid(xg), g2_ref[...])

    kk = k * kk_ref[...]
    nrm = jnp.sqrt(_group_sum(kk * kk, bd_ref[...]))
    kk = kk / jnp.maximum(nrm, 1e-12)
    kh = k * (1.0 + (a - 1.0) * ka_ref[...])

    r_out[...] = r
    w_out[...] = log_decay
    k_out[...] = kh
    v_out[...] = v
    a_out[...] = -kk
    b_out[...] = kk * a
    g_out[...] = g


def _rw_prep(p_rw, mu, w0, w2, a0, a2, g2, k_k, k_a, ones_bd, vres):
    t = p_rw.shape[0]
    tm = min(256, t)
    has_vres = vres is not None
    row = lambda n: pl.BlockSpec((1, n), lambda i: (0, 0))
    full = lambda a: pl.BlockSpec(a.shape, lambda i: (0, 0))
    blk = pl.BlockSpec((tm, RW_DIM), lambda i: (i, 0))
    in_specs = [pl.BlockSpec((tm, N_RW_PAD), lambda i: (i, 0)),
                pl.BlockSpec((8, N_RW_PAD), lambda i: (jnp.maximum(i * (tm // 8) - 1, 0), 0)),
                row(N_RW_PAD), row(RW_DIM), full(w2), row(RW_DIM), full(a2), full(g2),
                row(RW_DIM), row(RW_DIM), full(ones_bd)]
    args = [p_rw, p_rw, mu, w0, w2, a0, a2, g2, k_k, k_a, ones_bd]
    if has_vres:
        v_first, v0, v1, v2 = vres
        in_specs += [blk, row(RW_DIM), full(v1), full(v2)]
        args += [v_first, v0, v1, v2]
    return pl.pallas_call(
        functools.partial(_rw_prep_body, has_vres=has_vres),
        grid=(t // tm,),
        in_specs=in_specs,
        out_specs=[blk] * 7,
        out_shape=[jax.ShapeDtypeStruct((t, RW_DIM), F32)] * 7,
        compiler_params=_cparams(("arbitrary",)),
        name="rwkv_prep",
    )(*args)


SCAN_T = 128
HEAD_PAIRS = RW_HEADS // 2
GROUP = 32
NT_DIMS = (((1,), (1,)), ((), ()))
TN_DIMS = (((0,), (0,)), ((), ()))


def _split_bf16(x):
    hi = x.astype(BF16)
    return hi, (x - hi.astype(F32)).astype(BF16)


def _group_cumsum(x, rowmod):
    sh = 1
    while sh < GROUP:
        x = x + jnp.where(rowmod >= sh, pltpu.roll(x, sh, axis=0), 0.0)
        sh *= 2
    return x


def _group_last(x, rowmod):
    n = x.shape[0]
    x = jnp.where(rowmod == GROUP - 1, x, 0.0)
    sh = 1
    while sh < GROUP:
        x = x + jnp.where(rowmod < GROUP - sh, pltpu.roll(x, n - sh, axis=0), 0.0)
        sh *= 2
    return x


def _rw_chunk_body(r_ref, lw_ref, k_ref, v_ref, a_ref, b_ref, y_ref,
                   s_ref, w4_ref, rt_ref, bhh_ref, bhl_ref, kh_ref, vc_ref, c1c_ref, c1r_ref, p8_ref,
                   z_ref, ys_ref, vv_ref, arb_ref, ark_ref):
    @pl.when(pl.program_id(0) == 0)
    def _():
        s_ref[...] = jnp.zeros_like(s_ref)

    tb = r_ref.shape[0]
    ng = tb // GROUP
    pairs = range(HEAD_PAIRS)
    items = [(p, h) for p in pairs for h in range(2)]
    rowmod = lax.broadcasted_iota(jnp.int32, (tb, LANES), 0) & (GROUP - 1)
    lo = lax.broadcasted_iota(jnp.int32, (tb, LANES), 1) < RW_HEAD
    ti = lax.broadcasted_iota(jnp.int32, (tb, tb), 0)
    si = lax.broadcasted_iota(jnp.int32, (tb, tb), 1)
    same = (ti // GROUP) == (si // GROUP)
    strict = same & (si < ti)
    incl = same & (si <= ti)
    bdot = lambda x, y: jnp.dot(x, y, preferred_element_type=F32)
    ntdot = lambda x, y: lax.dot_general(x, y, NT_DIMS, preferred_element_type=F32)

    pre = []
    for p in pairs:
        sl = slice(p * LANES, (p + 1) * LANES)
        lw = lw_ref[:, sl]
        cs = _group_cumsum(lw, rowmod)
        cse = _group_last(cs, rowmod)
        pend = jnp.exp(cse - cs)
        pinv = jnp.exp(-cs)
        at_hi, at_lo = _split_bf16(a_ref[:, sl] * jnp.exp(cs - lw))
        bh_hi, bh_lo = _split_bf16(b_ref[:, sl] * pend)
        pre.append(dict(
            at_hi=at_hi, at_lo=at_lo, bh_hi=bh_hi, bh_lo=bh_lo,
            rt=(r_ref[:, sl] * jnp.exp(cs)).astype(BF16),
            bt=(b_ref[:, sl] * pinv).astype(BF16), kt=(k_ref[:, sl] * pinv).astype(BF16),
            kh=(k_ref[:, sl] * pend).astype(BF16), v=v_ref[:, sl], p8=jnp.exp(cse)))
    zero = jnp.zeros((tb, LANES), BF16)
    grams = []
    for p in pairs:
        d = pre[p]
        lhs = jnp.concatenate([jnp.where(lo, d["at_hi"], zero), jnp.where(lo, zero, d["at_hi"]),
                               jnp.where(lo, d["rt"], zero), jnp.where(lo, zero, d["rt"])], axis=0)
        grams.append(ntdot(lhs, jnp.concatenate([d["bt"], d["kt"]], axis=0)))
    mask = lambda h, x: jnp.where(lo, x, zero) if h == 0 else jnp.where(lo, zero, x)
    n1, mk, npow, tm1, vh, ath = {}, {}, {}, {}, {}, {}
    for p, h in items:
        g = grams[p]
        n1[p, h] = jnp.where(strict, g[h * tb:(h + 1) * tb, :tb], 0.0)
        mk[p, h] = jnp.where(strict, g[h * tb:(h + 1) * tb, tb:], 0.0)
        arb_ref[p, h] = jnp.where(incl, g[(2 + h) * tb:(3 + h) * tb, :tb], 0.0).astype(BF16)
        ark_ref[p, h] = jnp.where(incl, g[(2 + h) * tb:(3 + h) * tb, tb:], 0.0).astype(BF16)
        tm1[p, h] = n1[p, h]
        npow[p, h] = n1[p, h].astype(BF16)
        vh[p, h] = pre[p]["v"][:, h * RW_HEAD:(h + 1) * RW_HEAD]
        ath[p, h] = mask(h, pre[p]["at_hi"])
    sh = 2
    while sh < GROUP:
        sq = {it: bdot(npow[it], npow[it]) for it in items}
        for it in items:
            npow[it] = sq[it].astype(BF16)
        for it in items:
            tm1[it] = tm1[it] + sq[it] + bdot(tm1[it].astype(BF16), npow[it])
        sh *= 2
    tm1 = {it: tm1[it].astype(BF16) for it in items}
    tmk = {it: (mk[it] + bdot(tm1[it], mk[it].astype(BF16))).astype(BF16) for it in items}
    corr = {it: bdot(tm1[it], ath[it]).astype(BF16) for it in items}
    vt = {it: vh[it].T.astype(BF16) for it in items}
    vh = {it: vh[it].astype(BF16) for it in items}
    c1r = {it: bdot(tmk[it], vh[it]) for it in items}
    c1c = {it: ntdot(vt[it], tmk[it]) for it in items}
    for p, h in items:
        vv_ref[p, h] = vh[p, h]
    for p in pairs:
        d = pre[p]
        for g in range(ng):
            rows = slice(g * GROUP, (g + 1) * GROUP)
            both = lambda f: jnp.concatenate([f(0), f(1)], axis=0)
            w4_ref[p, g] = jnp.concatenate(
                [both(lambda h: ath[p, h][rows]), both(lambda h: mask(h, d["at_lo"])[rows]),
                 both(lambda h: corr[p, h][rows]), both(lambda h: ath[p, h][rows])], axis=1)
            rt_ref[p, g] = both(lambda h: mask(h, d["rt"])[rows])
            bhh_ref[p, g] = both(lambda h: mask(h, d["bh_hi"])[rows])
            bhl_ref[p, g] = both(lambda h: mask(h, d["bh_lo"])[rows])
            kh_ref[p, g] = both(lambda h: mask(h, d["kh"])[rows])
            vc_ref[p, g] = jnp.concatenate([vt[p, 0][:, rows], vt[p, 1][:, rows]], axis=1)
            c1c_ref[p, g] = jnp.concatenate([c1c[p, 0][:, rows], c1c[p, 1][:, rows]], axis=1)
            c1r_ref[p, g] = both(lambda h: c1r[p, h][rows])
            p8_ref[p, g] = d["p8"][g * GROUP:g * GROUP + SUBLANES]

    def group_step(g, carry):
        r0 = pl.multiple_of(g * GROUP, GROUP)
        s0 = [s_ref[p] for p in pairs]
        sp = [_split_bf16(s) for s in s0]
        s4 = [jnp.concatenate([hi, hi, hi, lo_], axis=1) for hi, lo_ in sp]
        zc = [ntdot(s4[p], w4_ref[p, g]) + c1c_ref[p, g] for p in pairs]
        vk = [bdot(vc_ref[p, g], kh_ref[p, g]) for p in pairs]
        zs = [_split_bf16(z) for z in zc]
        for p in pairs:
            z_hi, z_lo = zs[p]
            upd = (bdot(z_hi, bhh_ref[p, g]) + bdot(z_lo, bhh_ref[p, g])
                   + bdot(z_hi, bhl_ref[p, g]) + vk[p])
            s_ref[p] = s0[p] * p8_ref[p, g][0:1, :] + upd
        for p in pairs:
            zr = ntdot(w4_ref[p, g], s4[p]) + c1r_ref[p, g]
            yr = ntdot(rt_ref[p, g], sp[p][0])
            for h in range(2):
                z_ref[p, h, pl.ds(r0, GROUP), :] = zr[h * GROUP:(h + 1) * GROUP].astype(BF16)
                ys_ref[p, h, pl.ds(r0, GROUP), :] = yr[h * GROUP:(h + 1) * GROUP]
        return carry

    lax.fori_loop(0, ng, group_step, 0)

    ys = {it: ys_ref[it] + bdot(arb_ref[it], z_ref[it]) + bdot(ark_ref[it], vv_ref[it])
          for it in items}
    for p in pairs:
        y_ref[:, p * LANES:(p + 1) * LANES] = jnp.concatenate([ys[p, 0], ys[p, 1]], axis=1)


def _rw_scan_chunked(r, lw, k, v, a, b):
    t = r.shape[0]
    ng = SCAN_T // GROUP
    g2 = 2 * GROUP
    blk = pl.BlockSpec((SCAN_T, RW_DIM), lambda i: (i, 0))
    per_head = lambda n, dt: pltpu.VMEM((HEAD_PAIRS, 2, SCAN_T, n), dt)
    per_group = lambda rows, n, dt: pltpu.VMEM((HEAD_PAIRS, ng, rows, n), dt)
    return pl.pallas_call(
        _rw_chunk_body,
        grid=(t // SCAN_T,),
        in_specs=[blk] * 6,
        out_specs=blk,
        out_shape=jax.ShapeDtypeStruct((t, RW_DIM), F32),
        scratch_shapes=[pltpu.VMEM((HEAD_PAIRS, RW_HEAD, LANES), F32),
                        per_group(g2, 4 * LANES, BF16), per_group(g2, LANES, BF16),
                        per_group(g2, LANES, BF16), per_group(g2, LANES, BF16),
                        per_group(g2, LANES, BF16), per_group(RW_HEAD, g2, BF16),
                        per_group(RW_HEAD, g2, F32), per_group(g2, RW_HEAD, F32),
                        per_group(SUBLANES, LANES, F32),
                        per_head(RW_HEAD, BF16), per_head(RW_HEAD, F32), per_head(RW_HEAD, BF16),
                        per_head(SCAN_T, BF16), per_head(SCAN_T, BF16)],
        compiler_params=_cparams(("arbitrary",)),
        name="rwkv_scan",
    )(r, lw, k, v, a, b)


def _rw_post_body(y_ref, r_ref, k_ref, v_ref, g_ref, lw_ref, lb_ref, rk_ref, bd_ref, o_ref):
    bd = bd_ref[...]
    y = y_ref[...]
    inv_n = 1.0 / RW_HEAD
    d = y - _group_sum(y, bd) * inv_n
    var = _group_sum(d * d, bd) * inv_n
    yn = d * lax.rsqrt(var + GN_EPS) * lw_ref[...] + lb_ref[...]
    bonus = _group_sum(r_ref[...] * k_ref[...] * rk_ref[...], bd) * v_ref[...]
    o_ref[...] = ((yn + bonus) * g_ref[...]).astype(o_ref.dtype)


def _rw_post(y, r, k, v, g, lnx_w, lnx_b, r_k, ones_bd):
    t = y.shape[0]
    tm = min(256, t)
    blk = pl.BlockSpec((tm, RW_DIM), lambda i: (i, 0))
    row = pl.BlockSpec((1, RW_DIM), lambda i: (0, 0))
    return pl.pallas_call(
        _rw_post_body,
        grid=(t // tm,),
        in_specs=[blk] * 5 + [row] * 3 + [pl.BlockSpec(ones_bd.shape, lambda i: (0, 0))],
        out_specs=blk,
        out_shape=jax.ShapeDtypeStruct((t, RW_DIM), BF16),
        compiler_params=_cparams(("arbitrary",)),
        name="rwkv_post",
    )(y, r, k, v, g, lnx_w, lnx_b, r_k, ones_bd)


def _rms(x, n):
    return x * lax.rsqrt(jnp.sum(x * x, axis=-1, keepdims=True) * (1.0 / n) + RMS_EPS)


def _mla_prep_body(p_ref, cos_ref, sin_ref, qn_ref, kvn_ref, wq_ref, wk_ref, wv_ref,
                   qnn_ref, qnr_ref, knn_ref, knr_ref, q_out, k_out, v_out):
    p = p_ref[...]
    cos = cos_ref[...]
    sin = sin_ref[...]
    scale = (QK_NOPE + QK_ROPE) ** -0.5 * LOG2_E

    def rope(x):
        return x * cos + pltpu.roll(x, ROPE_PAD // 2, axis=1) * sin

    cq = _rms(p[:, :Q_LORA], Q_LORA) * qn_ref[...]
    ckv = _rms(p[:, Q_LORA:Q_LORA + KV_LORA], KV_LORA) * kvn_ref[...]
    qf = _bdot(cq, wq_ref[...])
    kn = _bdot(ckv, wk_ref[...])
    v_out[...] = _bdot(ckv, wv_ref[...]).astype(v_out.dtype)
    kr = rope(_rms(p[:, Q_LORA + KV_LORA:], QK_ROPE) * knr_ref[...]).astype(k_out.dtype)
    for h in range(MLA_HEADS):
        o = h * QK_PAD
        qn = _rms(qf[:, o:o + QK_NOPE], QK_NOPE) * qnn_ref[...]
        qr = rope(_rms(qf[:, o + QK_NOPE:o + QK_PAD], QK_ROPE) * qnr_ref[...])
        q_out[:, o:o + QK_NOPE] = (qn * scale).astype(q_out.dtype)
        q_out[:, o + QK_NOPE:o + QK_PAD] = (qr * scale).astype(q_out.dtype)
        kh = _rms(kn[:, h * QK_NOPE:(h + 1) * QK_NOPE], QK_NOPE) * knn_ref[...]
        k_out[:, o:o + QK_NOPE] = kh.astype(k_out.dtype)
        k_out[:, o + QK_NOPE:o + QK_PAD] = kr


def _mla_prep(p_mla, cos, sin, q_norm, kv_norm, wq, wk, wv, qnn, qnr, knn, knr):
    t = p_mla.shape[0]
    tm = min(256, t)
    row = lambda a: pl.BlockSpec(a.shape, lambda i: (0, 0))
    blk = lambda n: pl.BlockSpec((tm, n), lambda i: (i, 0))
    return pl.pallas_call(
        _mla_prep_body,
        grid=(t // tm,),
        in_specs=[blk(N_MLA_PAD), blk(ROPE_PAD), blk(ROPE_PAD), row(q_norm), row(kv_norm),
                  row(wq), row(wk), row(wv), row(qnn), row(qnr), row(knn), row(knr)],
        out_specs=[blk(MLA_HEADS * QK_PAD), blk(MLA_HEADS * QK_PAD), blk(MLA_DIM)],
        out_shape=[jax.ShapeDtypeStruct((t, MLA_HEADS * QK_PAD), BF16),
                   jax.ShapeDtypeStruct((t, MLA_HEADS * QK_PAD), BF16),
                   jax.ShapeDtypeStruct((t, MLA_DIM), BF16)],
        compiler_params=_cparams(("arbitrary",)),
        name="mla_prep",
    )(p_mla, cos, sin, q_norm, kv_norm, wq, wk, wv, qnn, qnr, knn, knr)


ATT_Q = 1024
ATT_HALF = ATT_Q // 2
ATT_K = 1024


def _attn_update(q, k, v, m_ref, l_ref, acc_ref, half, col_shift):
    s = lax.dot_general(q, k, (((1,), (1,)), ((), ())), preferred_element_type=F32)
    if col_shift is not None:
        rows = lax.broadcasted_iota(jnp.int32, s.shape, 0)
        cols = lax.broadcasted_iota(jnp.int32, s.shape, 1)
        s = jnp.where(cols <= rows + col_shift, s, -jnp.inf)
    tiles = [s[:, c * LANES:(c + 1) * LANES] for c in range(s.shape[1] // LANES)]
    mx = functools.reduce(jnp.maximum, tiles)
    m_old = m_ref[half]
    m_new = jnp.maximum(m_old, jnp.max(mx, axis=1, keepdims=True))
    alpha = jnp.exp2(m_old - m_new)
    ps = [jnp.exp2(x - m_new) for x in tiles]
    l_ref[half] = alpha * l_ref[half] + functools.reduce(jnp.add, ps)
    p = jnp.concatenate([x.astype(v.dtype) for x in ps], axis=1)
    acc_ref[half] = alpha * acc_ref[half] + jnp.dot(p, v, preferred_element_type=F32)
    m_ref[half] = m_new


def _attn_body(q_ref, k_ref, v_ref, o_ref, m_ref, l_ref, acc_ref):
    qi = pl.program_id(1)
    m_ref[...] = jnp.full_like(m_ref, -jnp.inf)
    l_ref[...] = jnp.zeros_like(l_ref)
    acc_ref[...] = jnp.zeros_like(acc_ref)
    halves = [(0, slice(0, ATT_HALF)), (1, slice(ATT_HALF, ATT_Q))]

    def full_block(j, carry):
        r0 = pl.multiple_of(j * ATT_K, ATT_K)
        k = k_ref[pl.ds(r0, ATT_K), :]
        v = v_ref[pl.ds(r0, ATT_K), :]
        for half, rows in halves:
            _attn_update(q_ref[rows, :], k, v, m_ref, l_ref, acc_ref, half, None)
        return carry

    lax.fori_loop(0, qi * (ATT_Q // ATT_K), full_block, 0)

    d0 = pl.multiple_of(qi * ATT_Q, ATT_Q)
    _attn_update(q_ref[:ATT_HALF, :], k_ref[pl.ds(d0, ATT_HALF), :], v_ref[pl.ds(d0, ATT_HALF), :],
                 m_ref, l_ref, acc_ref, 0, 0)
    _attn_update(q_ref[ATT_HALF:, :], k_ref[pl.ds(d0, ATT_Q), :], v_ref[pl.ds(d0, ATT_Q), :],
                 m_ref, l_ref, acc_ref, 1, ATT_HALF)
    for half, rows in halves:
        denom = jnp.sum(l_ref[half], axis=1, keepdims=True)
        o_ref[rows, :] = (acc_ref[half] / denom).astype(o_ref.dtype)


def _attention(q, k, v):
    t = q.shape[0]
    assert t % ATT_Q == 0 and ATT_Q % ATT_K == 0
    stat = pltpu.VMEM((2, ATT_HALF, LANES), F32)
    return pl.pallas_call(
        _attn_body,
        grid=(MLA_HEADS, t // ATT_Q),
        in_specs=[pl.BlockSpec((ATT_Q, QK_PAD), lambda h, i: (i, h)),
                  pl.BlockSpec((t, QK_PAD), lambda h, i: (0, h)),
                  pl.BlockSpec((t, V_HEAD), lambda h, i: (0, h))],
        out_specs=pl.BlockSpec((ATT_Q, V_HEAD), lambda h, i: (i, h)),
        out_shape=jax.ShapeDtypeStruct((t, MLA_DIM), BF16),
        scratch_shapes=[stat, stat, pltpu.VMEM((2, ATT_HALF, V_HEAD), F32)],
        compiler_params=_cparams(("arbitrary", "arbitrary")),
        name="mla_attention",
    )(q, k, v)


def _sg_body(p_ref, lw_ref, lb_ref, ws_ref, bs_ref, o_ref):
    tm = p_ref.shape[0]
    u = _gelu_tanh(p_ref[:, :SG_DIM])
    gv = _gelu_tanh(p_ref[:, SG_DIM:])
    mu = jnp.mean(gv, axis=-1, keepdims=True)
    d = gv - mu
    var = jnp.mean(d * d, axis=-1, keepdims=True)
    vn = (d * lax.rsqrt(var + LN_EPS) * lw_ref[...] + lb_ref[...]).astype(BF16)
    rows = lax.broadcasted_iota(jnp.int32, (CHUNK, CHUNK), 0)
    cols = lax.broadcasted_iota(jnp.int32, (CHUNK, CHUNK), 1)
    tril = cols <= rows
    for g in range(SG_GROUPS):
        gs = slice(g * SG_GROUP_DIM, (g + 1) * SG_GROUP_DIM)
        wg = jnp.where(tril, ws_ref[g], 0.0).astype(BF16)
        for c in range(tm // CHUNK):
            cs = slice(c * CHUNK, (c + 1) * CHUNK)
            s = jnp.dot(wg, vn[cs, gs], preferred_element_type=F32) + bs_ref[:, gs]
            o_ref[cs, gs] = (u[cs, gs] * s).astype(o_ref.dtype)


def _spatial_gating(p_sg, ln_w, ln_b, ws, bs_full):
    t = p_sg.shape[0]
    tm = min(256, t)
    row = pl.BlockSpec((1, SG_DIM), lambda i: (0, 0))
    return pl.pallas_call(
        _sg_body,
        grid=(t // tm,),
        in_specs=[pl.BlockSpec((tm, 2 * SG_DIM), lambda i: (i, 0)), row, row,
                  pl.BlockSpec(ws.shape, lambda i: (0, 0, 0)),
                  pl.BlockSpec(bs_full.shape, lambda i: (0, 0))],
        out_specs=pl.BlockSpec((tm, SG_DIM), lambda i: (i, 0)),
        out_shape=jax.ShapeDtypeStruct((t, SG_DIM), BF16),
        compiler_params=_cparams(("arbitrary",)),
        name="spatial_gating",
    )(p_sg, ln_w, ln_b, ws, bs_full)


def _merge_body(ya_ref, yb_ref, yc_ref, wa_ref, wb_ref, wc_ref, ga_ref, gb_ref, gc_ref, o_ref):
    m = ga_ref[...].astype(F32) * jnp.dot(ya_ref[...], wa_ref[...], preferred_element_type=F32)
    m += gb_ref[...].astype(F32) * jnp.dot(yb_ref[...], wb_ref[...], preferred_element_type=F32)
    m += gc_ref[...].astype(F32) * jnp.dot(yc_ref[...], wc_ref[...], preferred_element_type=F32)
    o_ref[...] = m.astype(o_ref.dtype)


def _merge(ya, yb, yc, wa, wb, wc, gates):
    t = ya.shape[0]
    d = wa.shape[1]
    tm = min(512, t)
    tn = 1024
    nj = d // tn
    yblk = pl.BlockSpec((tm, ya.shape[1]), lambda i, j: (i, 0))
    wblk = pl.BlockSpec((wa.shape[0], tn), lambda i, j: (0, j))
    gblk = lambda b: pl.BlockSpec((tm, tn), lambda i, j: (i, b * nj + j))
    return pl.pallas_call(
        _merge_body,
        grid=(t // tm, nj),
        in_specs=[yblk, yblk, yblk, wblk, wblk, wblk, gblk(0), gblk(1), gblk(2)],
        out_specs=pl.BlockSpec((tm, tn), lambda i, j: (i, j)),
        out_shape=jax.ShapeDtypeStruct((t, d), BF16),
        compiler_params=_cparams(("arbitrary", "arbitrary")),
        name="branch_merge",
    )(ya, yb, yc, wa, wb, wc, gates, gates, gates)


CONV_CARRY = 8


def _ffn_up_body(h_ref, wg_ref, wv_ref, cwg_ref, cwv_ref, cbg_ref, cbv_ref, o_ref,
                 cg_ref, cv_ref, wg16_ref, wv16_ref):
    i = pl.program_id(1)

    @pl.when(i == 0)
    def _():
        cg_ref[...] = jnp.zeros_like(cg_ref)
        cv_ref[...] = jnp.zeros_like(cv_ref)
        wg16_ref[...] = wg_ref[...].astype(BF16)
        wv16_ref[...] = wv_ref[...].astype(BF16)

    h = h_ref[...]
    tm = h.shape[0]
    rowid = lax.broadcasted_iota(jnp.int32, (tm, 1), 0)

    def conv(w_ref, cw_ref, cb_ref, carry_ref):
        up = jnp.dot(h, w_ref[...], preferred_element_type=F32)
        c1 = carry_ref[CONV_CARRY - 1:CONV_CARRY, :]
        c2 = carry_ref[CONV_CARRY - 2:CONV_CARRY - 1, :]
        s1 = jnp.where(rowid == 0, c1, pltpu.roll(up, 1, axis=0))
        s2 = jnp.where(rowid == 0, c2, jnp.where(rowid == 1, c1, pltpu.roll(up, 2, axis=0)))
        carry_ref[...] = up[tm - CONV_CARRY:, :]
        return cb_ref[...] + cw_ref[0:1, :] * s2 + cw_ref[1:2, :] * s1 + cw_ref[2:3, :] * up

    gate = conv(wg16_ref, cwg_ref, cbg_ref, cg_ref)
    val = conv(wv16_ref, cwv_ref, cbv_ref, cv_ref)
    o_ref[...] = (gate * _sigmoid(gate) * val).astype(o_ref.dtype)


def _ffn_up(h, w_up, layer, conv_w, conv_b):
    t, d = h.shape
    tm = min(1024, t)
    tn = 512
    nj = D_FF // tn
    return pl.pallas_call(
        _ffn_up_body,
        grid=(nj, t // tm),
        in_specs=[pl.BlockSpec((tm, d), lambda j, i: (i, 0)),
                  pl.BlockSpec((None, d, tn), lambda j, i: (layer, 0, j)),
                  pl.BlockSpec((None, d, tn), lambda j, i: (layer, 0, nj + j)),
                  pl.BlockSpec((CONV_W, tn), lambda j, i: (0, j)),
                  pl.BlockSpec((CONV_W, tn), lambda j, i: (0, nj + j)),
                  pl.BlockSpec((1, tn), lambda j, i: (0, j)),
                  pl.BlockSpec((1, tn), lambda j, i: (0, nj + j))],
        out_specs=pl.BlockSpec((tm, tn), lambda j, i: (i, j)),
        out_shape=jax.ShapeDtypeStruct((t, D_FF), BF16),
        scratch_shapes=[pltpu.VMEM((CONV_CARRY, tn), F32), pltpu.VMEM((CONV_CARRY, tn), F32),
                        pltpu.VMEM((d, tn), BF16), pltpu.VMEM((d, tn), BF16)],
        compiler_params=_cparams(("arbitrary", "arbitrary")),
        name="ffn_up_conv",
    )(h, w_up, w_up, conv_w, conv_w, conv_b, conv_b)


def _pad_cols(a, n):
    return jnp.pad(a, ((0, 0), (0, n - a.shape[1])))


def _rw_cols(a):
    o = 3 * RW_DIM
    return jnp.concatenate([a[:, :o], _pad_cols(a[:, o:o + W_LORA], LORA_PAD),
                            _pad_cols(a[:, o + W_LORA:o + W_LORA + A_LORA], LORA_PAD),
                            a[:, o + W_LORA + A_LORA:N_RW]], axis=1)


def _pad_rows(a, n):
    return jnp.pad(a, ((0, n - a.shape[0]), (0, 0)))


def kernel(x, c, positions, ada_w, ada_b, norm_mix_g, norm_ffn_g, w_in, rw_mu, rw_w0, rw_w2, rw_a0, rw_a2, rw_g2, rw_kk, rw_ka, rw_rk, rw_lnx_w, rw_lnx_b, rw_v0, rw_v1, rw_v2, mla_q_norm, mla_kv_norm, mla_w_uq, mla_w_ukv, mla_qn_nope, mla_qn_rope, mla_kn_nope, mla_kn_rope, sg_ln_w, sg_ln_b, sg_ws, sg_b, w_br_a, w_br_b, w_br_c, w_out, ffn_up, ffn_conv, ffn_conv_b, ffn_down):
    b_, t, d = x.shape
    assert b_ == 1 and d == D_MODEL
    depth = w_in.shape[0]
    xs = x.reshape(t, d)

    mod = _ada_all(c, ada_w, ada_b)
    cos, sin = _rope_tables(positions, t)
    lane_head = jnp.arange(LANES) // RW_HEAD
    ones_bd = (jnp.tile(lane_head, 2)[:, None] == lane_head[None, :]).astype(BF16)
    row = lambda a: a.reshape(1, -1)

    v_first = None
    for l in range(depth):
        sh1, sc1, gt1, sh2, sc2, gt2 = [mod[l, :, i * d:(i + 1) * d] for i in range(6)]

        h = _norm_mod(xs, row(norm_mix_g[l]), sc1, sh1)
        wl = w_in[l]
        o_dq = N_RW
        o_kr = N_RW + Q_LORA + KV_LORA
        o_sg = o_kr + QK_ROPE
        o_gt = o_sg + 2 * SG_DIM
        w_rw = _rw_cols(wl).astype(BF16)
        w_mla = jnp.concatenate([wl[:, o_dq:o_kr], _rope_pad(wl[:, o_kr:o_sg])], axis=1).astype(BF16)
        p_rw = _matmul(h, w_rw, out_dtype=F32, tm=1024, tn=512, name="proj_rw")
        p_mla = _matmul(h, w_mla, out_dtype=F32, tm=1024, tn=N_MLA_PAD, name="proj_mla")
        p_sg = _matmul(h, w_in, layer=l, col0=o_sg, n=2 * SG_DIM, out_dtype=F32, tm=2048, tn=512,
                       name="proj_sg")
        gates = _matmul(h, w_in, layer=l, col0=o_gt, n=3 * d, out_dtype=BF16, tm=2048, tn=512,
                        epi="sigmoid", name="proj_gates")

        vres = None
        if l > 0:
            vres = (v_first, row(rw_v0[l - 1]), rw_v1[l - 1].astype(BF16),
                    rw_v2[l - 1].astype(BF16))
        r_, w_, k_, v_, a_, b2_, g_ = _rw_prep(
            p_rw, _rw_cols(row(rw_mu[l])), row(rw_w0[l]),
            _pad_rows(rw_w2[l], LORA_PAD).astype(BF16), row(rw_a0[l]),
            _pad_rows(rw_a2[l], LORA_PAD).astype(BF16), rw_g2[l].astype(BF16),
            row(rw_kk[l]), row(rw_ka[l]), ones_bd, vres)
        if l == 0:
            v_first = v_
        y_scan = _rw_scan_chunked(r_, w_, k_, v_, a_, b2_)
        y_a = _rw_post(y_scan, r_, k_, v_, g_, row(rw_lnx_w[l]), row(rw_lnx_b[l]),
                       row(rw_rk[l]), ones_bd)

        wq = mla_w_uq[l].reshape(Q_LORA, MLA_HEADS, QK_NOPE + QK_ROPE)
        wq = jnp.concatenate([wq[..., :QK_NOPE], _rope_pad(wq[..., QK_NOPE:])], axis=-1)
        wq = wq.reshape(Q_LORA, MLA_HEADS * QK_PAD).astype(BF16)
        wkv = mla_w_ukv[l].reshape(KV_LORA, MLA_HEADS, QK_NOPE + V_HEAD)
        wk = wkv[..., :QK_NOPE].reshape(KV_LORA, MLA_HEADS * QK_NOPE).astype(BF16)
        wv = wkv[..., QK_NOPE:].reshape(KV_LORA, MLA_DIM).astype(BF16)
        q, k, v = _mla_prep(p_mla, cos, sin, row(mla_q_norm[l]), row(mla_kv_norm[l]), wq, wk, wv,
                            row(mla_qn_nope[l]), row(_rope_pad(mla_qn_rope[l])),
                            row(mla_kn_nope[l]), row(_rope_pad(mla_kn_rope[l])))
        y_b = _attention(q, k, v)

        bs_full = jnp.repeat(sg_b[l].T, SG_GROUP_DIM, axis=1)
        y_c = _spatial_gating(p_sg, row(sg_ln_w[l]), row(sg_ln_b[l]), sg_ws[l], bs_full)

        merged = _merge(y_a, y_b, y_c, w_br_a[l].astype(BF16), w_br_b[l].astype(BF16),
                        w_br_c[l].astype(BF16), gates)
        xs = _matmul(merged, w_out[l].astype(BF16), out_dtype=F32, tm=1024, tn=1024,
                     epi="residual", res=xs, gate=gt1, name="out_proj")

        h = _norm_mod(xs, row(norm_ffn_g[l]), sc2, sh2)
        act = _ffn_up(h, ffn_up, l, ffn_conv[l], row(ffn_conv_b[l]))
        xs = _matmul(act, ffn_down[l].astype(BF16), out_dtype=F32, tm=1024, tn=1024,
                     tk=D_FF // 2, epi="residual", res=xs, gate=gt2, name="ffn_down")

    return xs.reshape(b_, t, d)
```

```python
import functools

import jax
import jax.numpy as jnp
from jax import lax
from jax.experimental import pallas as pl
from jax.experimental.pallas import tpu as pltpu

F32 = jnp.float32
BF16 = jnp.bfloat16

D_MODEL = 2048
RW_HEAD = 64
RW_HEADS = 16
RW_DIM = RW_HEADS * RW_HEAD
W_LORA = 96
A_LORA = 96
V_LORA = 64
G_LORA = 256
GN_EPS = 64e-5
MLA_HEADS = 8
Q_LORA = 512
KV_LORA = 512
QK_NOPE = 128
QK_ROPE = 64
V_HEAD = 128
MLA_DIM = MLA_HEADS * V_HEAD
ROPE_THETA = 10000.0
CHUNK = 128
SG_GROUPS = 8
SG_GROUP_DIM = 128
SG_DIM = SG_GROUPS * SG_GROUP_DIM
D_FF = 5632
CONV_W = 3
RMS_EPS = 1e-6
LN_EPS = 1e-5

LANES = 128
SUBLANES = 8
LORA_PAD = 128
N_RW = 3 * RW_DIM + W_LORA + A_LORA + G_LORA
N_RW_PAD = 3 * RW_DIM + 2 * LORA_PAD + G_LORA
ROPE_PAD = 128
QK_PAD = QK_NOPE + ROPE_PAD
N_MLA_PAD = Q_LORA + KV_LORA + ROPE_PAD
VMEM_LIMIT = 52 * 1024 * 1024
LOG2_E = 1.4426950408889634


def _cparams(sem):
    return pltpu.CompilerParams(dimension_semantics=sem, vmem_limit_bytes=VMEM_LIMIT)


def _sigmoid(x):
    return 1.0 / (1.0 + jnp.exp(-x))


def _softplus(x):
    return jnp.maximum(x, 0.0) + jnp.log(1.0 + jnp.exp(-jnp.abs(x)))


def _gelu_tanh(x):
    return 0.5 * x * (1.0 + jnp.tanh(0.7978845608028654 * (x + 0.044715 * (x * x * x))))


def _bdot(a, b):
    return jnp.dot(a.astype(BF16), b, preferred_element_type=F32)


def _group_sum(x, ones2):
    hi = x.astype(BF16)
    lo = (x - hi.astype(F32)).astype(BF16)
    out = []
    for c in range(x.shape[1] // LANES):
        sl = slice(c * LANES, (c + 1) * LANES)
        out.append(jnp.dot(jnp.concatenate([hi[:, sl], lo[:, sl]], axis=1), ones2,
                           preferred_element_type=F32))
    return jnp.concatenate(out, axis=1)


def _ada_body(c_ref, w_ref, b_ref, o_ref, *, kc):
    d = c_ref.shape[0]
    tn = o_ref.shape[-1]

    def step(i, acc):
        ck = c_ref[pl.ds(i * kc, kc), :]
        sk = ck * _sigmoid(ck)
        wk = w_ref[0, pl.ds(i * kc, kc), :]
        return acc + jnp.sum(wk * sk, axis=0, keepdims=True)

    acc = lax.fori_loop(0, d // kc, step, jnp.zeros((1, tn), F32))
    o_ref[0] = acc + b_ref[0]


def _ada_all(c, ada_w, ada_b):
    nl, d, n = ada_w.shape
    tn = 1024
    out = pl.pallas_call(
        functools.partial(_ada_body, kc=256),
        grid=(nl, n // tn),
        in_specs=[pl.BlockSpec((d, 1), lambda l, j: (0, 0)),
                  pl.BlockSpec((1, d, tn), lambda l, j: (l, 0, j)),
                  pl.BlockSpec((1, 1, tn), lambda l, j: (l, 0, j))],
        out_specs=pl.BlockSpec((1, 1, tn), lambda l, j: (l, 0, j)),
        out_shape=jax.ShapeDtypeStruct((nl, 1, n), F32),
        compiler_params=_cparams(("arbitrary", "arbitrary")),
        name="ada_mod",
    )(c.reshape(d, 1), ada_w, ada_b.reshape(nl, 1, n))
    return out


def _rope_body(pos_ref, inv_ref, msk_ref, sgn_ref, cos_ref, sin_ref):
    ang = pos_ref[...].astype(F32) * inv_ref[...]
    cos_ref[...] = jnp.cos(ang) * msk_ref[...]
    sin_ref[...] = jnp.sin(ang) * sgn_ref[...]


def _rope_pad(v):
    h = QK_ROPE // 2
    z = jnp.zeros(v.shape[:-1] + (ROPE_PAD // 2 - h,), v.dtype)
    return jnp.concatenate([v[..., :h], z, v[..., h:], z], axis=-1)


def _rope_tables(positions, t):
    tm = min(1024, t)
    inv = ROPE_THETA ** (-jnp.arange(0, QK_ROPE, 2, dtype=F32) / QK_ROPE)
    ones = jnp.ones((QK_ROPE // 2,), F32)
    inv_p = _rope_pad(jnp.concatenate([inv, inv]))[None]
    msk_p = _rope_pad(jnp.concatenate([ones, ones]))[None]
    sgn_p = _rope_pad(jnp.concatenate([-ones, ones]))[None]
    row = pl.BlockSpec((1, ROPE_PAD), lambda i: (0, 0))
    blk = pl.BlockSpec((tm, ROPE_PAD), lambda i: (i, 0))
    return pl.pallas_call(
        _rope_body,
        grid=(t // tm,),
        in_specs=[pl.BlockSpec((tm, 1), lambda i: (i, 0)), row, row, row],
        out_specs=[blk, blk],
        out_shape=[jax.ShapeDtypeStruct((t, ROPE_PAD), F32)] * 2,
        compiler_params=_cparams(("arbitrary",)),
        name="rope_tables",
    )(positions.reshape(t, 1), inv_p, msk_p, sgn_p)


def _norm_mod_body(x_ref, g_ref, sc_ref, sh_ref, o_ref):
    x = x_ref[...]
    y = x * lax.rsqrt(jnp.mean(x * x, axis=-1, keepdims=True) + RMS_EPS) * g_ref[...]
    o_ref[...] = (y * (1.0 + sc_ref[...]) + sh_ref[...]).astype(o_ref.dtype)


def _norm_mod(x, g, sc, sh):
    t, d = x.shape
    tm = min(512, t)
    row = pl.BlockSpec((1, d), lambda i: (0, 0))
    return pl.pallas_call(
        _norm_mod_body,
        grid=(t // tm,),
        in_specs=[pl.BlockSpec((tm, d), lambda i: (i, 0)), row, row, row],
        out_specs=pl.BlockSpec((tm, d), lambda i: (i, 0)),
        out_shape=jax.ShapeDtypeStruct((t, d), BF16),
        compiler_params=_cparams(("arbitrary",)),
        name="norm_mod",
    )(x, g, sc, sh)


def _mm_body(*refs, nk, epi):
    if epi == "residual":
        a_ref, b_ref, res_ref, gt_ref, o_ref = refs[:5]
        rest = refs[5:]
    else:
        a_ref, b_ref, o_ref = refs[:3]
        rest = refs[3:]

    def finish(acc):
        if epi == "sigmoid":
            acc = _sigmoid(acc)
        elif epi == "residual":
            acc = res_ref[...] + gt_ref[...] * acc
        o_ref[...] = acc.astype(o_ref.dtype)

    part = jnp.dot(a_ref[...], b_ref[...].astype(BF16), preferred_element_type=F32)
    if nk == 1:
        finish(part)
        return
    acc_ref, = rest
    k = pl.program_id(2)

    @pl.when(k == 0)
    def _():
        acc_ref[...] = part

    @pl.when(k > 0)
    def _():
        acc_ref[...] += part

    @pl.when(k == nk - 1)
    def _():
        finish(acc_ref[...])


def _matmul(a, b, *, out_dtype, tm, tn, tk=None, epi="none", res=None, gate=None, name="matmul",
            layer=None, col0=0, n=None):
    m, kd = a.shape
    n = b.shape[-1] if n is None else n
    tm = min(tm, m)
    tn = min(tn, n)
    tk = kd if tk is None else tk
    nk = kd // tk
    assert m % tm == 0 and n % tn == 0 and kd % tk == 0 and col0 % tn == 0
    if layer is None:
        b_spec = pl.BlockSpec((tk, tn), lambda i, j, k: (k, j))
    else:
        b_spec = pl.BlockSpec((None, tk, tn), lambda i, j, k: (layer, k, col0 // tn + j))
    in_specs = [pl.BlockSpec((tm, tk), lambda i, j, k: (i, k)), b_spec]
    args = [a, b]
    if epi == "residual":
        in_specs += [pl.BlockSpec((tm, tn), lambda i, j, k: (i, j)),
                     pl.BlockSpec((1, tn), lambda i, j, k: (0, j))]
        args += [res, gate]
    scratch = [pltpu.VMEM((tm, tn), F32)] if nk > 1 else []
    return pl.pallas_call(
        functools.partial(_mm_body, nk=nk, epi=epi),
        grid=(m // tm, n // tn, nk),
        in_specs=in_specs,
        out_specs=pl.BlockSpec((tm, tn), lambda i, j, k: (i, j)),
        out_shape=jax.ShapeDtypeStruct((m, n), out_dtype),
        scratch_shapes=scratch,
        compiler_params=_cparams(("arbitrary", "arbitrary", "arbitrary")),
        name=name,
    )(*args)


def _rw_prep_body(*refs, has_vres):
    (p_ref, pprev_ref, mu_ref, w0_ref, w2_ref, a0_ref, a2_ref, g2_ref, kk_ref, ka_ref,
     bd_ref) = refs[:11]
    pos = 11
    if has_vres:
        vf_ref, v0_ref, v1_ref, v2_ref = refs[pos:pos + 4]
        pos += 4
    r_out, w_out, k_out, v_out, a_out, b_out, g_out = refs[pos:pos + 7]

    i = pl.program_id(0)
    p = p_ref[...]
    tm = p.shape[0]
    last = pprev_ref[7:8, :] * (i > 0).astype(F32)
    rowid = lax.broadcasted_iota(jnp.int32, (tm, 1), 0)
    prev = jnp.where(rowid == 0, last, pltpu.roll(p, 1, axis=0))
    pm = p + (prev - p) * mu_ref[...]

    o = 3 * RW_DIM
    r = pm[:, 0:RW_DIM]
    k = pm[:, RW_DIM:2 * RW_DIM]
    v = pm[:, 2 * RW_DIM:o]
    xw = pm[:, o:o + LORA_PAD]
    xa = pm[:, o + LORA_PAD:o + 2 * LORA_PAD]
    xg = pm[:, o + 2 * LORA_PAD:]

    if has_vres:
        mix = _sigmoid(v0_ref[...] + _bdot(_bdot(v, v1_ref[...]), v2_ref[...]))
        v = v + (vf_ref[...] - v) * mix

    w_log = -_softplus(-(w0_ref[...] + _bdot(jnp.tanh(xw), w2_ref[...]))) - 0.5
    log_decay = -jnp.exp(w_log)
    a = _sigmoid(a0_ref[...] + _bdot(xa, a2_ref[...]))
    g = _bdot(_sigmoid(xg), g2_ref[...])

    kk = k * kk_ref[...]
    nrm = jnp.sqrt(_group_sum(kk * kk, bd_ref[...]))
    kk = kk / jnp.maximum(nrm, 1e-12)
    kh = k * (1.0 + (a - 1.0) * ka_ref[...])

    r_out[...] = r
    w_out[...] = log_decay
    k_out[...] = kh
    v_out[...] = v
    a_out[...] = -kk
    b_out[...] = kk * a
    g_out[...] = g


def _rw_prep(p_rw, mu, w0, w2, a0, a2, g2, k_k, k_a, ones_bd, vres):
    t = p_rw.shape[0]
    tm = min(256, t)
    has_vres = vres is not None
    row = lambda n: pl.BlockSpec((1, n), lambda i: (0, 0))
    full = lambda a: pl.BlockSpec(a.shape, lambda i: (0, 0))
    blk = pl.BlockSpec((tm, RW_DIM), lambda i: (i, 0))
    in_specs = [pl.BlockSpec((tm, N_RW_PAD), lambda i: (i, 0)),
                pl.BlockSpec((8, N_RW_PAD), lambda i: (jnp.maximum(i * (tm // 8) - 1, 0), 0)),
                row(N_RW_PAD), row(RW_DIM), full(w2), row(RW_DIM), full(a2), full(g2),
                row(RW_DIM), row(RW_DIM), full(ones_bd)]
    args = [p_rw, p_rw, mu, w0, w2, a0, a2, g2, k_k, k_a, ones_bd]
    if has_vres:
        v_first, v0, v1, v2 = vres
        in_specs += [blk, row(RW_DIM), full(v1), full(v2)]
        args += [v_first, v0, v1, v2]
    return pl.pallas_call(
        functools.partial(_rw_prep_body, has_vres=has_vres),
        grid=(t // tm,),
        in_specs=in_specs,
        out_specs=[blk] * 7,
        out_shape=[jax.ShapeDtypeStruct((t, RW_DIM), F32)] * 7,
        compiler_params=_cparams(("arbitrary",)),
        name="rwkv_prep",
    )(*args)


SCAN_T = 128
HEAD_PAIRS = RW_HEADS // 2
GROUP = 32
NT_DIMS = (((1,), (1,)), ((), ()))
TN_DIMS = (((0,), (0,)), ((), ()))


def _split_bf16(x):
    hi = x.astype(BF16)
    return hi, (x - hi.astype(F32)).astype(BF16)


def _group_cumsum(x, rowmod):
    sh = 1
    while sh < GROUP:
        x = x + jnp.where(rowmod >= sh, pltpu.roll(x, sh, axis=0), 0.0)
        sh *= 2
    return x


def _group_last(x, rowmod):
    n = x.shape[0]
    x = jnp.where(rowmod == GROUP - 1, x, 0.0)
    sh = 1
    while sh < GROUP:
        x = x + jnp.where(rowmod < GROUP - sh, pltpu.roll(x, n - sh, axis=0), 0.0)
        sh *= 2
    return x


def _rw_chunk_body(r_ref, lw_ref, k_ref, v_ref, a_ref, b_ref, y_ref,
                   s_ref, w4_ref, rt_ref, bhh_ref, bhl_ref, kh_ref, vc_ref, c1c_ref, p8_ref,
                   z_ref, ys_ref, vv_ref, arb_ref, ark_ref):
    @pl.when(pl.program_id(0) == 0)
    def _():
        s_ref[...] = jnp.zeros_like(s_ref)

    tb = r_ref.shape[0]
    ng = tb // GROUP
    pairs = range(HEAD_PAIRS)
    items = [(p, h) for p in pairs for h in range(2)]
    rowmod = lax.broadcasted_iota(jnp.int32, (tb, LANES), 0) & (GROUP - 1)
    lo = lax.broadcasted_iota(jnp.int32, (tb, LANES), 1) < RW_HEAD
    ti = lax.broadcasted_iota(jnp.int32, (tb, tb), 0)
    si = lax.broadcasted_iota(jnp.int32, (tb, tb), 1)
    same = (ti // GROUP) == (si // GROUP)
    strict = same & (si < ti)
    incl = same & (si <= ti)
    bdot = lambda x, y: jnp.dot(x, y, preferred_element_type=F32)
    ntdot = lambda x, y: lax.dot_general(x, y, NT_DIMS, preferred_element_type=F32)

    pre = []
    for p in pairs:
        sl = slice(p * LANES, (p + 1) * LANES)
        lw = lw_ref[:, sl]
        cs = _group_cumsum(lw, rowmod)
        cse = _group_last(cs, rowmod)
        pend = jnp.exp(cse - cs)
        pinv = jnp.exp(-cs)
        at_hi, at_lo = _split_bf16(a_ref[:, sl] * jnp.exp(cs - lw))
        bh_hi, bh_lo = _split_bf16(b_ref[:, sl] * pend)
        pre.append(dict(
            at_hi=at_hi, at_lo=at_lo, bh_hi=bh_hi, bh_lo=bh_lo,
            rt=(r_ref[:, sl] * jnp.exp(cs)).astype(BF16),
            bt=(b_ref[:, sl] * pinv).astype(BF16), kt=(k_ref[:, sl] * pinv).astype(BF16),
            kh=(k_ref[:, sl] * pend).astype(BF16), v=v_ref[:, sl], p8=jnp.exp(cse)))
    zero = jnp.zeros((tb, LANES), BF16)
    grams = []
    for p in pairs:
        d = pre[p]
        lhs = jnp.concatenate([jnp.where(lo, d["at_hi"], zero), jnp.where(lo, zero, d["at_hi"]),
                               jnp.where(lo, d["rt"], zero), jnp.where(lo, zero, d["rt"])], axis=0)
        grams.append(ntdot(lhs, jnp.concatenate([d["bt"], d["kt"]], axis=0)))
    mask = lambda h, x: jnp.where(lo, x, zero) if h == 0 else jnp.where(lo, zero, x)
    n1, mk, npow, tm1, vh, ath = {}, {}, {}, {}, {}, {}
    for p, h in items:
        g = grams[p]
        n1[p, h] = jnp.where(strict, g[h * tb:(h + 1) * tb, :tb], 0.0)
        mk[p, h] = jnp.where(strict, g[h * tb:(h + 1) * tb, tb:], 0.0)
        arb_ref[p, h] = jnp.where(incl, g[(2 + h) * tb:(3 + h) * tb, :tb], 0.0).astype(BF16)
        ark_ref[p, h] = jnp.where(incl, g[(2 + h) * tb:(3 + h) * tb, tb:], 0.0).astype(BF16)
        tm1[p, h] = n1[p, h]
        npow[p, h] = n1[p, h].astype(BF16)
        vh[p, h] = pre[p]["v"][:, h * RW_HEAD:(h + 1) * RW_HEAD]
        ath[p, h] = mask(h, pre[p]["at_hi"])
    sh = 2
    while sh < GROUP:
        sq = {it: bdot(npow[it], npow[it]) for it in items}
        for it in items:
            npow[it] = sq[it].astype(BF16)
        for it in items:
            tm1[it] = tm1[it] + sq[it] + bdot(tm1[it].astype(BF16), npow[it])
        sh *= 2
    tm1 = {it: tm1[it].astype(BF16) for it in items}
    tmk = {it: (mk[it] + bdot(tm1[it], mk[it].astype(BF16))).astype(BF16) for it in items}
    corr = {it: bdot(tm1[it], ath[it]).astype(BF16) for it in items}
    vt = {it: vh[it].T.astype(BF16) for it in items}
    vh = {it: vh[it].astype(BF16) for it in items}
    c1c = {it: ntdot(vt[it], tmk[it]) for it in items}
    for p, h in items:
        vv_ref[p, h] = vh[p, h]
    for p in pairs:
        d = pre[p]
        for g in range(ng):
            rows = slice(g * GROUP, (g + 1) * GROUP)
            both = lambda f: jnp.concatenate([f(0), f(1)], axis=0)
            w4_ref[p, g] = jnp.concatenate(
                [both(lambda h: ath[p, h][rows]), both(lambda h: mask(h, d["at_lo"])[rows]),
                 both(lambda h: corr[p, h][rows]), both(lambda h: ath[p, h][rows])], axis=1)
            rt_ref[p, g] = both(lambda h: mask(h, d["rt"])[rows])
            bhh_ref[p, g] = both(lambda h: mask(h, d["bh_hi"])[rows])
            bhl_ref[p, g] = both(lambda h: mask(h, d["bh_lo"])[rows])
            kh_ref[p, g] = both(lambda h: mask(h, d["kh"])[rows])
            vc_ref[p, g] = jnp.concatenate([vt[p, 0][:, rows], vt[p, 1][:, rows]], axis=1)
            c1c_ref[p, g] = jnp.concatenate([c1c[p, 0][:, rows], c1c[p, 1][:, rows]], axis=1)
            p8_ref[p, g] = d["p8"][g * GROUP:g * GROUP + SUBLANES]

    def group_step(g, carry):
        r0 = pl.multiple_of(g * GROUP, GROUP)
        s0 = [s_ref[p] for p in pairs]
        sp = [_split_bf16(s) for s in s0]
        s4 = [jnp.concatenate([hi, hi, hi, lo_], axis=1) for hi, lo_ in sp]
        zc = [ntdot(s4[p], w4_ref[p, g]) + c1c_ref[p, g] for p in pairs]
        vk = [bdot(vc_ref[p, g], kh_ref[p, g]) for p in pairs]
        zs = [_split_bf16(z) for z in zc]
        for p in pairs:
            z_hi, z_lo = zs[p]
            upd = (bdot(z_hi, bhh_ref[p, g]) + bdot(z_lo, bhh_ref[p, g])
                   + bdot(z_hi, bhl_ref[p, g]) + vk[p])
            s_ref[p] = s0[p] * p8_ref[p, g][0:1, :] + upd
        for p in pairs:
            zr = zc[p].T
            yr = ntdot(rt_ref[p, g], sp[p][0])
            for h in range(2):
                z_ref[p, h, pl.ds(r0, GROUP), :] = zr[h * GROUP:(h + 1) * GROUP].astype(BF16)
                ys_ref[p, h, pl.ds(r0, GROUP), :] = yr[h * GROUP:(h + 1) * GROUP]
        return carry

    lax.fori_loop(0, ng, group_step, 0)

    ys = {it: ys_ref[it] + bdot(arb_ref[it], z_ref[it]) + bdot(ark_ref[it], vv_ref[it])
          for it in items}
    for p in pairs:
        y_ref[:, p * LANES:(p + 1) * LANES] = jnp.concatenate([ys[p, 0], ys[p, 1]], axis=1)


def _rw_scan_chunked(r, lw, k, v, a, b):
    t = r.shape[0]
    ng = SCAN_T // GROUP
    g2 = 2 * GROUP
    blk = pl.BlockSpec((SCAN_T, RW_DIM), lambda i: (i, 0))
    per_head = lambda n, dt: pltpu.VMEM((HEAD_PAIRS, 2, SCAN_T, n), dt)
    per_group = lambda rows, n, dt: pltpu.VMEM((HEAD_PAIRS, ng, rows, n), dt)
    return pl.pallas_call(
        _rw_chunk_body,
        grid=(t // SCAN_T,),
        in_specs=[blk] * 6,
        out_specs=blk,
        out_shape=jax.ShapeDtypeStruct((t, RW_DIM), F32),
        scratch_shapes=[pltpu.VMEM((HEAD_PAIRS, RW_HEAD, LANES), F32),
                        per_group(g2, 4 * LANES, BF16), per_group(g2, LANES, BF16),
                        per_group(g2, LANES, BF16), per_group(g2, LANES, BF16),
                        per_group(g2, LANES, BF16), per_group(RW_HEAD, g2, BF16),
                        per_group(RW_HEAD, g2, F32),
                        per_group(SUBLANES, LANES, F32),
                        per_head(RW_HEAD, BF16), per_head(RW_HEAD, F32), per_head(RW_HEAD, BF16),
                        per_head(SCAN_T, BF16), per_head(SCAN_T, BF16)],
        compiler_params=_cparams(("arbitrary",)),
        name="rwkv_scan",
    )(r, lw, k, v, a, b)


def _rw_post_body(y_ref, r_ref, k_ref, v_ref, g_ref, lw_ref, lb_ref, rk_ref, bd_ref, o_ref):
    bd = bd_ref[...]
    y = y_ref[...]
    inv_n = 1.0 / RW_HEAD
    d = y - _group_sum(y, bd) * inv_n
    var = _group_sum(d * d, bd) * inv_n
    yn = d * lax.rsqrt(var + GN_EPS) * lw_ref[...] + lb_ref[...]
    bonus = _group_sum(r_ref[...] * k_ref[...] * rk_ref[...], bd) * v_ref[...]
    o_ref[...] = ((yn + bonus) * g_ref[...]).astype(o_ref.dtype)


def _rw_post(y, r, k, v, g, lnx_w, lnx_b, r_k, ones_bd):
    t = y.shape[0]
    tm = min(256, t)
    blk = pl.BlockSpec((tm, RW_DIM), lambda i: (i, 0))
    row = pl.BlockSpec((1, RW_DIM), lambda i: (0, 0))
    return pl.pallas_call(
        _rw_post_body,
        grid=(t // tm,),
        in_specs=[blk] * 5 + [row] * 3 + [pl.BlockSpec(ones_bd.shape, lambda i: (0, 0))],
        out_specs=blk,
        out_shape=jax.ShapeDtypeStruct((t, RW_DIM), BF16),
        compiler_params=_cparams(("arbitrary",)),
        name="rwkv_post",
    )(y, r, k, v, g, lnx_w, lnx_b, r_k, ones_bd)


def _rms(x, n):
    return x * lax.rsqrt(jnp.sum(x * x, axis=-1, keepdims=True) * (1.0 / n) + RMS_EPS)


def _mla_prep_body(p_ref, cos_ref, sin_ref, qn_ref, kvn_ref, wq_ref, wk_ref, wv_ref,
                   qnn_ref, qnr_ref, knn_ref, knr_ref, q_out, k_out, v_out):
    p = p_ref[...]
    cos = cos_ref[...]
    sin = sin_ref[...]
    scale = (QK_NOPE + QK_ROPE) ** -0.5 * LOG2_E

    def rope(x):
        return x * cos + pltpu.roll(x, ROPE_PAD // 2, axis=1) * sin

    cq = _rms(p[:, :Q_LORA], Q_LORA) * qn_ref[...]
    ckv = _rms(p[:, Q_LORA:Q_LORA + KV_LORA], KV_LORA) * kvn_ref[...]
    qf = _bdot(cq, wq_ref[...])
    kn = _bdot(ckv, wk_ref[...])
    v_out[...] = _bdot(ckv, wv_ref[...]).astype(v_out.dtype)
    kr = rope(_rms(p[:, Q_LORA + KV_LORA:], QK_ROPE) * knr_ref[...]).astype(k_out.dtype)
    for h in range(MLA_HEADS):
        o = h * QK_PAD
        qn = _rms(qf[:, o:o + QK_NOPE], QK_NOPE) * qnn_ref[...]
        qr = rope(_rms(qf[:, o + QK_NOPE:o + QK_PAD], QK_ROPE) * qnr_ref[...])
        q_out[:, o:o + QK_NOPE] = (qn * scale).astype(q_out.dtype)
        q_out[:, o + QK_NOPE:o + QK_PAD] = (qr * scale).astype(q_out.dtype)
        kh = _rms(kn[:, h * QK_NOPE:(h + 1) * QK_NOPE], QK_NOPE) * knn_ref[...]
        k_out[:, o:o + QK_NOPE] = kh.astype(k_out.dtype)
        k_out[:, o + QK_NOPE:o + QK_PAD] = kr


def _mla_prep(p_mla, cos, sin, q_norm, kv_norm, wq, wk, wv, qnn, qnr, knn, knr):
    t = p_mla.shape[0]
    tm = min(256, t)
    row = lambda a: pl.BlockSpec(a.shape, lambda i: (0, 0))
    blk = lambda n: pl.BlockSpec((tm, n), lambda i: (i, 0))
    return pl.pallas_call(
        _mla_prep_body,
        grid=(t // tm,),
        in_specs=[blk(N_MLA_PAD), blk(ROPE_PAD), blk(ROPE_PAD), row(q_norm), row(kv_norm),
                  row(wq), row(wk), row(wv), row(qnn), row(qnr), row(knn), row(knr)],
        out_specs=[blk(MLA_HEADS * QK_PAD), blk(MLA_HEADS * QK_PAD), blk(MLA_DIM)],
        out_shape=[jax.ShapeDtypeStruct((t, MLA_HEADS * QK_PAD), BF16),
                   jax.ShapeDtypeStruct((t, MLA_HEADS * QK_PAD), BF16),
                   jax.ShapeDtypeStruct((t, MLA_DIM), BF16)],
        compiler_params=_cparams(("arbitrary",)),
        name="mla_prep",
    )(p_mla, cos, sin, q_norm, kv_norm, wq, wk, wv, qnn, qnr, knn, knr)


ATT_Q = 1024
ATT_HALF = ATT_Q // 2
ATT_K = 1024


def _attn_update(q, k, v, m_ref, l_ref, acc_ref, half, col_shift):
    s = lax.dot_general(q, k, (((1,), (1,)), ((), ())), preferred_element_type=F32)
    if col_shift is not None:
        rows = lax.broadcasted_iota(jnp.int32, s.shape, 0)
        cols = lax.broadcasted_iota(jnp.int32, s.shape, 1)
        s = jnp.where(cols <= rows + col_shift, s, -jnp.inf)
    tiles = [s[:, c * LANES:(c + 1) * LANES] for c in range(s.shape[1] // LANES)]
    mx = functools.reduce(jnp.maximum, tiles)
    m_old = m_ref[half]
    m_new = jnp.maximum(m_old, jnp.max(mx, axis=1, keepdims=True))
    alpha = jnp.exp2(m_old - m_new)
    ps = [jnp.exp2(x - m_new) for x in tiles]
    l_ref[half] = alpha * l_ref[half] + functools.reduce(jnp.add, ps)
    p = jnp.concatenate([x.astype(v.dtype) for x in ps], axis=1)
    acc_ref[half] = alpha * acc_ref[half] + jnp.dot(p, v, preferred_element_type=F32)
    m_ref[half] = m_new


def _attn_body(q_ref, k_ref, v_ref, o_ref, m_ref, l_ref, acc_ref):
    qi = pl.program_id(1)
    m_ref[...] = jnp.full_like(m_ref, -jnp.inf)
    l_ref[...] = jnp.zeros_like(l_ref)
    acc_ref[...] = jnp.zeros_like(acc_ref)
    halves = [(0, slice(0, ATT_HALF)), (1, slice(ATT_HALF, ATT_Q))]

    def full_block(j, carry):
        r0 = pl.multiple_of(j * ATT_K, ATT_K)
        k = k_ref[pl.ds(r0, ATT_K), :]
        v = v_ref[pl.ds(r0, ATT_K), :]
        for half, rows in halves:
            _attn_update(q_ref[rows, :], k, v, m_ref, l_ref, acc_ref, half, None)
        return carry

    lax.fori_loop(0, qi * (ATT_Q // ATT_K), full_block, 0)

    d0 = pl.multiple_of(qi * ATT_Q, ATT_Q)
    _attn_update(q_ref[:ATT_HALF, :], k_ref[pl.ds(d0, ATT_HALF), :], v_ref[pl.ds(d0, ATT_HALF), :],
                 m_ref, l_ref, acc_ref, 0, 0)
    _attn_update(q_ref[ATT_HALF:, :], k_ref[pl.ds(d0, ATT_Q), :], v_ref[pl.ds(d0, ATT_Q), :],
                 m_ref, l_ref, acc_ref, 1, ATT_HALF)
    for half, rows in halves:
        denom = jnp.sum(l_ref[half], axis=1, keepdims=True)
        o_ref[rows, :] = (acc_ref[half] / denom).astype(o_ref.dtype)


def _attention(q, k, v):
    t = q.shape[0]
    assert t % ATT_Q == 0 and ATT_Q % ATT_K == 0
    stat = pltpu.VMEM((2, ATT_HALF, LANES), F32)
    return pl.pallas_call(
        _attn_body,
        grid=(MLA_HEADS, t // ATT_Q),
        in_specs=[pl.BlockSpec((ATT_Q, QK_PAD), lambda h, i: (i, h)),
                  pl.BlockSpec((t, QK_PAD), lambda h, i: (0, h)),
                  pl.BlockSpec((t, V_HEAD), lambda h, i: (0, h))],
        out_specs=pl.BlockSpec((ATT_Q, V_HEAD), lambda h, i: (i, h)),
        out_shape=jax.ShapeDtypeStruct((t, MLA_DIM), BF16),
        scratch_shapes=[stat, stat, pltpu.VMEM((2, ATT_HALF, V_HEAD), F32)],
        compiler_params=_cparams(("arbitrary", "arbitrary")),
        name="mla_attention",
    )(q, k, v)


def _sg_body(p_ref, lw_ref, lb_ref, ws_ref, bs_ref, o_ref):
    tm = p_ref.shape[0]
    u = _gelu_tanh(p_ref[:, :SG_DIM])
    gv = _gelu_tanh(p_ref[:, SG_DIM:])
    mu = jnp.mean(gv, axis=-1, keepdims=True)
    d = gv - mu
    var = jnp.mean(d * d, axis=-1, keepdims=True)
    vn = (d * lax.rsqrt(var + LN_EPS) * lw_ref[...] + lb_ref[...]).astype(BF16)
    rows = lax.broadcasted_iota(jnp.int32, (CHUNK, CHUNK), 0)
    cols = lax.broadcasted_iota(jnp.int32, (CHUNK, CHUNK), 1)
    tril = cols <= rows
    for g in range(SG_GROUPS):
        gs = slice(g * SG_GROUP_DIM, (g + 1) * SG_GROUP_DIM)
        wg = jnp.where(tril, ws_ref[g], 0.0).astype(BF16)
        for c in range(tm // CHUNK):
            cs = slice(c * CHUNK, (c + 1) * CHUNK)
            s = jnp.dot(wg, vn[cs, gs], preferred_element_type=F32) + bs_ref[:, gs]
            o_ref[cs, gs] = (u[cs, gs] * s).astype(o_ref.dtype)


def _spatial_gating(p_sg, ln_w, ln_b, ws, bs_full):
    t = p_sg.shape[0]
    tm = min(256, t)
    row = pl.BlockSpec((1, SG_DIM), lambda i: (0, 0))
    return pl.pallas_call(
        _sg_body,
        grid=(t // tm,),
        in_specs=[pl.BlockSpec((tm, 2 * SG_DIM), lambda i: (i, 0)), row, row,
                  pl.BlockSpec(ws.shape, lambda i: (0, 0, 0)),
                  pl.BlockSpec(bs_full.shape, lambda i: (0, 0))],
        out_specs=pl.BlockSpec((tm, SG_DIM), lambda i: (i, 0)),
        out_shape=jax.ShapeDtypeStruct((t, SG_DIM), BF16),
        compiler_params=_cparams(("arbitrary",)),
        name="spatial_gating",
    )(p_sg, ln_w, ln_b, ws, bs_full)


def _merge_body(ya_ref, yb_ref, yc_ref, wa_ref, wb_ref, wc_ref, ga_ref, gb_ref, gc_ref, o_ref):
    m = ga_ref[...].astype(F32) * jnp.dot(ya_ref[...], wa_ref[...], preferred_element_type=F32)
    m += gb_ref[...].astype(F32) * jnp.dot(yb_ref[...], wb_ref[...], preferred_element_type=F32)
    m += gc_ref[...].astype(F32) * jnp.dot(yc_ref[...], wc_ref[...], preferred_element_type=F32)
    o_ref[...] = m.astype(o_ref.dtype)


def _merge(ya, yb, yc, wa, wb, wc, gates):
    t = ya.shape[0]
    d = wa.shape[1]
    tm = min(512, t)
    tn = 1024
    nj = d // tn
    yblk = pl.BlockSpec((tm, ya.shape[1]), lambda i, j: (i, 0))
    wblk = pl.BlockSpec((wa.shape[0], tn), lambda i, j: (0, j))
    gblk = lambda b: pl.BlockSpec((tm, tn), lambda i, j: (i, b * nj + j))
    return pl.pallas_call(
        _merge_body,
        grid=(t // tm, nj),
        in_specs=[yblk, yblk, yblk, wblk, wblk, wblk, gblk(0), gblk(1), gblk(2)],
        out_specs=pl.BlockSpec((tm, tn), lambda i, j: (i, j)),
        out_shape=jax.ShapeDtypeStruct((t, d), BF16),
        compiler_params=_cparams(("arbitrary", "arbitrary")),
        name="branch_merge",
    )(ya, yb, yc, wa, wb, wc, gates, gates, gates)


CONV_CARRY = 8


def _ffn_up_body(h_ref, wg_ref, wv_ref, cwg_ref, cwv_ref, cbg_ref, cbv_ref, o_ref,
                 cg_ref, cv_ref, wg16_ref, wv16_ref):
    i = pl.program_id(1)

    @pl.when(i == 0)
    def _():
        cg_ref[...] = jnp.zeros_like(cg_ref)
        cv_ref[...] = jnp.zeros_like(cv_ref)
        wg16_ref[...] = wg_ref[...].astype(BF16)
        wv16_ref[...] = wv_ref[...].astype(BF16)

    h = h_ref[...]
    tm = h.shape[0]
    rowid = lax.broadcasted_iota(jnp.int32, (tm, 1), 0)

    def conv(w_ref, cw_ref, cb_ref, carry_ref):
        up = jnp.dot(h, w_ref[...], preferred_element_type=F32)
        c1 = carry_ref[CONV_CARRY - 1:CONV_CARRY, :]
        c2 = carry_ref[CONV_CARRY - 2:CONV_CARRY - 1, :]
        s1 = jnp.where(rowid == 0, c1, pltpu.roll(up, 1, axis=0))
        s2 = jnp.where(rowid == 0, c2, jnp.where(rowid == 1, c1, pltpu.roll(up, 2, axis=0)))
        carry_ref[...] = up[tm - CONV_CARRY:, :]
        return cb_ref[...] + cw_ref[0:1, :] * s2 + cw_ref[1:2, :] * s1 + cw_ref[2:3, :] * up

    gate = conv(wg16_ref, cwg_ref, cbg_ref, cg_ref)
    val = conv(wv16_ref, cwv_ref, cbv_ref, cv_ref)
    o_ref[...] = (gate * _sigmoid(gate) * val).astype(o_ref.dtype)


def _ffn_up(h, w_up, layer, conv_w, conv_b):
    t, d = h.shape
    tm = min(1024, t)
    tn = 512
    nj = D_FF // tn
    return pl.pallas_call(
        _ffn_up_body,
        grid=(nj, t // tm),
        in_specs=[pl.BlockSpec((tm, d), lambda j, i: (i, 0)),
                  pl.BlockSpec((None, d, tn), lambda j, i: (layer, 0, j)),
                  pl.BlockSpec((None, d, tn), lambda j, i: (layer, 0, nj + j)),
                  pl.BlockSpec((CONV_W, tn), lambda j, i: (0, j)),
                  pl.BlockSpec((CONV_W, tn), lambda j, i: (0, nj + j)),
                  pl.BlockSpec((1, tn), lambda j, i: (0, j)),
                  pl.BlockSpec((1, tn), lambda j, i: (0, nj + j))],
        out_specs=pl.BlockSpec((tm, tn), lambda j, i: (i, j)),
        out_shape=jax.ShapeDtypeStruct((t, D_FF), BF16),
        scratch_shapes=[pltpu.VMEM((CONV_CARRY, tn), F32), pltpu.VMEM((CONV_CARRY, tn), F32),
                        pltpu.VMEM((d, tn), BF16), pltpu.VMEM((d, tn), BF16)],
        compiler_params=_cparams(("arbitrary", "arbitrary")),
        name="ffn_up_conv",
    )(h, w_up, w_up, conv_w, conv_w, conv_b, conv_b)


def _pad_cols(a, n):
    return jnp.pad(a, ((0, 0), (0, n - a.shape[1])))


def _rw_cols(a):
    o = 3 * RW_DIM
    return jnp.concatenate([a[:, :o], _pad_cols(a[:, o:o + W_LORA], LORA_PAD),
                            _pad_cols(a[:, o + W_LORA:o + W_LORA + A_LORA], LORA_PAD),
                            a[:, o + W_LORA + A_LORA:N_RW]], axis=1)


def _pad_rows(a, n):
    return jnp.pad(a, ((0, n - a.shape[0]), (0, 0)))


def kernel(x, c, positions, ada_w, ada_b, norm_mix_g, norm_ffn_g, w_in, rw_mu, rw_w0, rw_w2, rw_a0, rw_a2, rw_g2, rw_kk, rw_ka, rw_rk, rw_lnx_w, rw_lnx_b, rw_v0, rw_v1, rw_v2, mla_q_norm, mla_kv_norm, mla_w_uq, mla_w_ukv, mla_qn_nope, mla_qn_rope, mla_kn_nope, mla_kn_rope, sg_ln_w, sg_ln_b, sg_ws, sg_b, w_br_a, w_br_b, w_br_c, w_out, ffn_up, ffn_conv, ffn_conv_b, ffn_down):
    b_, t, d = x.shape
    assert b_ == 1 and d == D_MODEL
    depth = w_in.shape[0]
    xs = x.reshape(t, d)

    mod = _ada_all(c, ada_w, ada_b)
    cos, sin = _rope_tables(positions, t)
    lane_head = jnp.arange(LANES) // RW_HEAD
    ones_bd = (jnp.tile(lane_head, 2)[:, None] == lane_head[None, :]).astype(BF16)
    row = lambda a: a.reshape(1, -1)

    v_first = None
    for l in range(depth):
        sh1, sc1, gt1, sh2, sc2, gt2 = [mod[l, :, i * d:(i + 1) * d] for i in range(6)]

        h = _norm_mod(xs, row(norm_mix_g[l]), sc1, sh1)
        wl = w_in[l]
        o_dq = N_RW
        o_kr = N_RW + Q_LORA + KV_LORA
        o_sg = o_kr + QK_ROPE
        o_gt = o_sg + 2 * SG_DIM
        w_rw = _rw_cols(wl).astype(BF16)
        w_mla = jnp.concatenate([wl[:, o_dq:o_kr], _rope_pad(wl[:, o_kr:o_sg])], axis=1).astype(BF16)
        p_rw = _matmul(h, w_rw, out_dtype=F32, tm=1024, tn=512, name="proj_rw")
        p_mla = _matmul(h, w_mla, out_dtype=F32, tm=1024, tn=N_MLA_PAD, name="proj_mla")
        p_sg = _matmul(h, w_in, layer=l, col0=o_sg, n=2 * SG_DIM, out_dtype=F32, tm=2048, tn=512,
                       name="proj_sg")
        gates = _matmul(h, w_in, layer=l, col0=o_gt, n=3 * d, out_dtype=BF16, tm=2048, tn=512,
                        epi="sigmoid", name="proj_gates")

        vres = None
        if l > 0:
            vres = (v_first, row(rw_v0[l - 1]), rw_v1[l - 1].astype(BF16),
                    rw_v2[l - 1].astype(BF16))
        r_, w_, k_, v_, a_, b2_, g_ = _rw_prep(
            p_rw, _rw_cols(row(rw_mu[l])), row(rw_w0[l]),
            _pad_rows(rw_w2[l], LORA_PAD).astype(BF16), row(rw_a0[l]),
            _pad_rows(rw_a2[l], LORA_PAD).astype(BF16), rw_g2[l].astype(BF16),
            row(rw_kk[l]), row(rw_ka[l]), ones_bd, vres)
        if l == 0:
            v_first = v_
        y_scan = _rw_scan_chunked(r_, w_, k_, v_, a_, b2_)
        y_a = _rw_post(y_scan, r_, k_, v_, g_, row(rw_lnx_w[l]), row(rw_lnx_b[l]),
                       row(rw_rk[l]), ones_bd)

        wq = mla_w_uq[l].reshape(Q_LORA, MLA_HEADS, QK_NOPE + QK_ROPE)
        wq = jnp.concatenate([wq[..., :QK_NOPE], _rope_pad(wq[..., QK_NOPE:])], axis=-1)
        wq = wq.reshape(Q_LORA, MLA_HEADS * QK_PAD).astype(BF16)
        wkv = mla_w_ukv[l].reshape(KV_LORA, MLA_HEADS, QK_NOPE + V_HEAD)
        wk = wkv[..., :QK_NOPE].reshape(KV_LORA, MLA_HEADS * QK_NOPE).astype(BF16)
        wv = wkv[..., QK_NOPE:].reshape(KV_LORA, MLA_DIM).astype(BF16)
        q, k, v = _mla_prep(p_mla, cos, sin, row(mla_q_norm[l]), row(mla_kv_norm[l]), wq, wk, wv,
                            row(mla_qn_nope[l]), row(_rope_pad(mla_qn_rope[l])),
                            row(mla_kn_nope[l]), row(_rope_pad(mla_kn_rope[l])))
        y_b = _attention(q, k, v)

        bs_full = jnp.repeat(sg_b[l].T, SG_GROUP_DIM, axis=1)
        y_c = _spatial_gating(p_sg, row(sg_ln_w[l]), row(sg_ln_b[l]), sg_ws[l], bs_full)

        merged = _merge(y_a, y_b, y_c, w_br_a[l].astype(BF16), w_br_b[l].astype(BF16),
                        w_br_c[l].astype(BF16), gates)
        xs = _matmul(merged, w_out[l].astype(BF16), out_dtype=F32, tm=1024, tn=1024,
                     epi="residual", res=xs, gate=gt1, name="out_proj")

        h = _norm_mod(xs, row(norm_ffn_g[l]), sc2, sh2)
        act = _ffn_up(h, ffn_up, l, ffn_conv[l], row(ffn_conv_b[l]))
        xs = _matmul(act, ffn_down[l].astype(BF16), out_dtype=F32, tm=1024, tn=1024,
                     tk=D_FF // 2, epi="residual", res=xs, gate=gt2, name="ffn_down")

    return xs.reshape(b_, t, d)
```

```python
import functools

import jax
import jax.numpy as jnp
from jax import lax
from jax.experimental import pallas as pl
from jax.experimental.pallas import tpu as pltpu

F32 = jnp.float32
BF16 = jnp.bfloat16

D_MODEL = 2048
RW_HEAD = 64
RW_HEADS = 16
RW_DIM = RW_HEADS * RW_HEAD
W_LORA = 96
A_LORA = 96
V_LORA = 64
G_LORA = 256
GN_EPS = 64e-5
MLA_HEADS = 8
Q_LORA = 512
KV_LORA = 512
QK_NOPE = 128
QK_ROPE = 64
V_HEAD = 128
MLA_DIM = MLA_HEADS * V_HEAD
ROPE_THETA = 10000.0
CHUNK = 128
SG_GROUPS = 8
SG_GROUP_DIM = 128
SG_DIM = SG_GROUPS * SG_GROUP_DIM
D_FF = 5632
CONV_W = 3
RMS_EPS = 1e-6
LN_EPS = 1e-5

LANES = 128
SUBLANES = 8
LORA_PAD = 128
N_RW = 3 * RW_DIM + W_LORA + A_LORA + G_LORA
N_RW_PAD = 3 * RW_DIM + 2 * LORA_PAD + G_LORA
ROPE_PAD = 128
QK_PAD = QK_NOPE + ROPE_PAD
N_MLA_PAD = Q_LORA + KV_LORA + ROPE_PAD
VMEM_LIMIT = 52 * 1024 * 1024
LOG2_E = 1.4426950408889634


def _cparams(sem):
    return pltpu.CompilerParams(dimension_semantics=sem, vmem_limit_bytes=VMEM_LIMIT)


def _sigmoid(x):
    return 1.0 / (1.0 + jnp.exp(-x))


def _softplus(x):
    return jnp.maximum(x, 0.0) + jnp.log(1.0 + jnp.exp(-jnp.abs(x)))


def _gelu_tanh(x):
    return 0.5 * x * (1.0 + jnp.tanh(0.7978845608028654 * (x + 0.044715 * (x * x * x))))


def _bdot(a, b):
    return jnp.dot(a.astype(BF16), b, preferred_element_type=F32)


def _group_sum(x, ones2):
    hi = x.astype(BF16)
    lo = (x - hi.astype(F32)).astype(BF16)
    out = []
    for c in range(x.shape[1] // LANES):
        sl = slice(c * LANES, (c + 1) * LANES)
        out.append(jnp.dot(jnp.concatenate([hi[:, sl], lo[:, sl]], axis=1), ones2,
                           preferred_element_type=F32))
    return jnp.concatenate(out, axis=1)


def _ada_body(c_ref, w_ref, b_ref, o_ref, *, kc):
    d = c_ref.shape[0]
    tn = o_ref.shape[-1]

    def step(i, acc):
        ck = c_ref[pl.ds(i * kc, kc), :]
        sk = ck * _sigmoid(ck)
        wk = w_ref[0, pl.ds(i * kc, kc), :]
        return acc + jnp.sum(wk * sk, axis=0, keepdims=True)

    acc = lax.fori_loop(0, d // kc, step, jnp.zeros((1, tn), F32))
    o_ref[0] = acc + b_ref[0]


def _ada_all(c, ada_w, ada_b):
    nl, d, n = ada_w.shape
    tn = 1024
    out = pl.pallas_call(
        functools.partial(_ada_body, kc=256),
        grid=(nl, n // tn),
        in_specs=[pl.BlockSpec((d, 1), lambda l, j: (0, 0)),
                  pl.BlockSpec((1, d, tn), lambda l, j: (l, 0, j)),
                  pl.BlockSpec((1, 1, tn), lambda l, j: (l, 0, j))],
        out_specs=pl.BlockSpec((1, 1, tn), lambda l, j: (l, 0, j)),
        out_shape=jax.ShapeDtypeStruct((nl, 1, n), F32),
        compiler_params=_cparams(("arbitrary", "arbitrary")),
        name="ada_mod",
    )(c.reshape(d, 1), ada_w, ada_b.reshape(nl, 1, n))
    return out


def _rope_body(pos_ref, inv_ref, msk_ref, sgn_ref, cos_ref, sin_ref):
    ang = pos_ref[...].astype(F32) * inv_ref[...]
    cos_ref[...] = jnp.cos(ang) * msk_ref[...]
    sin_ref[...] = jnp.sin(ang) * sgn_ref[...]


def _rope_pad(v):
    h = QK_ROPE // 2
    z = jnp.zeros(v.shape[:-1] + (ROPE_PAD // 2 - h,), v.dtype)
    return jnp.concatenate([v[..., :h], z, v[..., h:], z], axis=-1)


def _rope_tables(positions, t):
    tm = min(1024, t)
    inv = ROPE_THETA ** (-jnp.arange(0, QK_ROPE, 2, dtype=F32) / QK_ROPE)
    ones = jnp.ones((QK_ROPE // 2,), F32)
    inv_p = _rope_pad(jnp.concatenate([inv, inv]))[None]
    msk_p = _rope_pad(jnp.concatenate([ones, ones]))[None]
    sgn_p = _rope_pad(jnp.concatenate([-ones, ones]))[None]
    row = pl.BlockSpec((1, ROPE_PAD), lambda i: (0, 0))
    blk = pl.BlockSpec((tm, ROPE_PAD), lambda i: (i, 0))
    return pl.pallas_call(
        _rope_body,
        grid=(t // tm,),
        in_specs=[pl.BlockSpec((tm, 1), lambda i: (i, 0)), row, row, row],
        out_specs=[blk, blk],
        out_shape=[jax.ShapeDtypeStruct((t, ROPE_PAD), F32)] * 2,
        compiler_params=_cparams(("arbitrary",)),
        name="rope_tables",
    )(positions.reshape(t, 1), inv_p, msk_p, sgn_p)


def _norm_mod_body(x_ref, g_ref, sc_ref, sh_ref, o_ref):
    x = x_ref[...]
    y = x * lax.rsqrt(jnp.mean(x * x, axis=-1, keepdims=True) + RMS_EPS) * g_ref[...]
    o_ref[...] = (y * (1.0 + sc_ref[...]) + sh_ref[...]).astype(o_ref.dtype)


def _norm_mod(x, g, sc, sh):
    t, d = x.shape
    tm = min(512, t)
    row = pl.BlockSpec((1, d), lambda i: (0, 0))
    return pl.pallas_call(
        _norm_mod_body,
        grid=(t // tm,),
        in_specs=[pl.BlockSpec((tm, d), lambda i: (i, 0)), row, row, row],
        out_specs=pl.BlockSpec((tm, d), lambda i: (i, 0)),
        out_shape=jax.ShapeDtypeStruct((t, d), BF16),
        compiler_params=_cparams(("arbitrary",)),
        name="norm_mod",
    )(x, g, sc, sh)


def _mm_body(*refs, nk, epi):
    if epi == "residual":
        a_ref, b_ref, res_ref, gt_ref, o_ref = refs[:5]
        rest = refs[5:]
    else:
        a_ref, b_ref, o_ref = refs[:3]
        rest = refs[3:]

    def finish(acc):
        if epi == "sigmoid":
            acc = _sigmoid(acc)
        elif epi == "residual":
            acc = res_ref[...] + gt_ref[...] * acc
        o_ref[...] = acc.astype(o_ref.dtype)

    part = jnp.dot(a_ref[...], b_ref[...].astype(BF16), preferred_element_type=F32)
    if nk == 1:
        finish(part)
        return
    acc_ref, = rest
    k = pl.program_id(2)

    @pl.when(k == 0)
    def _():
        acc_ref[...] = part

    @pl.when(k > 0)
    def _():
        acc_ref[...] += part

    @pl.when(k == nk - 1)
    def _():
        finish(acc_ref[...])


def _matmul(a, b, *, out_dtype, tm, tn, tk=None, epi="none", res=None, gate=None, name="matmul",
            layer=None, col0=0, n=None):
    m, kd = a.shape
    n = b.shape[-1] if n is None else n
    tm = min(tm, m)
    tn = min(tn, n)
    tk = kd if tk is None else tk
    nk = kd // tk
    assert m % tm == 0 and n % tn == 0 and kd % tk == 0 and col0 % tn == 0
    if layer is None:
        b_spec = pl.BlockSpec((tk, tn), lambda i, j, k: (k, j))
    else:
        b_spec = pl.BlockSpec((None, tk, tn), lambda i, j, k: (layer, k, col0 // tn + j))
    in_specs = [pl.BlockSpec((tm, tk), lambda i, j, k: (i, k)), b_spec]
    args = [a, b]
    if epi == "residual":
        in_specs += [pl.BlockSpec((tm, tn), lambda i, j, k: (i, j)),
                     pl.BlockSpec((1, tn), lambda i, j, k: (0, j))]
        args += [res, gate]
    scratch = [pltpu.VMEM((tm, tn), F32)] if nk > 1 else []
    return pl.pallas_call(
        functools.partial(_mm_body, nk=nk, epi=epi),
        grid=(m // tm, n // tn, nk),
        in_specs=in_specs,
        out_specs=pl.BlockSpec((tm, tn), lambda i, j, k: (i, j)),
        out_shape=jax.ShapeDtypeStruct((m, n), out_dtype),
        scratch_shapes=scratch,
        compiler_params=_cparams(("arbitrary", "arbitrary", "arbitrary")),
        name=name,
    )(*args)


def _rw_prep_body(*refs, has_vres):
    (p_ref, pprev_ref, mu_ref, w0_ref, w2_ref, a0_ref, a2_ref, g2_ref, kk_ref, ka_ref,
     bd_ref) = refs[:11]
    pos = 11
    if has_vres:
        vf_ref, v0_ref, v1_ref, v2_ref = refs[pos:pos + 4]
        pos += 4
    r_out, w_out, k_out, v_out, a_out, b_out, g_out = refs[pos:pos + 7]

    i = pl.program_id(0)
    p = p_ref[...]
    tm = p.shape[0]
    last = pprev_ref[7:8, :] * (i > 0).astype(F32)
    rowid = lax.broadcasted_iota(jnp.int32, (tm, 1), 0)
    prev = jnp.where(rowid == 0, last, pltpu.roll(p, 1, axis=0))
    pm = p + (prev - p) * mu_ref[...]

    o = 3 * RW_DIM
    r = pm[:, 0:RW_DIM]
    k = pm[:, RW_DIM:2 * RW_DIM]
    v = pm[:, 2 * RW_DIM:o]
    xw = pm[:, o:o + LORA_PAD]
    xa = pm[:, o + LORA_PAD:o + 2 * LORA_PAD]
    xg = pm[:, o + 2 * LORA_PAD:]

    if has_vres:
        mix = _sigmoid(v0_ref[...] + _bdot(_bdot(v, v1_ref[...]), v2_ref[...]))
        v = v + (vf_ref[...] - v) * mix

    w_log = -_softplus(-(w0_ref[...] + _bdot(jnp.tanh(xw), w2_ref[...]))) - 0.5
    log_decay = -jnp.exp(w_log)
    a = _sigmoid(a0_ref[...] + _bdot(xa, a2_ref[...]))
    g = _bdot(_sigmoid(xg), g2_ref[...])

    kk = k * kk_ref[...]
    nrm = jnp.sqrt(_group_sum(kk * kk, bd_ref[...]))
    kk = kk / jnp.maximum(nrm, 1e-12)
    kh = k * (1.0 + (a - 1.0) * ka_ref[...])

    r_out[...] = r
    w_out[...] = log_decay
    k_out[...] = kh
    v_out[...] = v
    a_out[...] = -kk
    b_out[...] = kk * a
    g_out[...] = g


def _rw_prep(p_rw, mu, w0, w2, a0, a2, g2, k_k, k_a, ones_bd, vres):
    t = p_rw.shape[0]
    tm = min(256, t)
    has_vres = vres is not None
    row = lambda n: pl.BlockSpec((1, n), lambda i: (0, 0))
    full = lambda a: pl.BlockSpec(a.shape, lambda i: (0, 0))
    blk = pl.BlockSpec((tm, RW_DIM), lambda i: (i, 0))
    in_specs = [pl.BlockSpec((tm, N_RW_PAD), lambda i: (i, 0)),
                pl.BlockSpec((8, N_RW_PAD), lambda i: (jnp.maximum(i * (tm // 8) - 1, 0), 0)),
                row(N_RW_PAD), row(RW_DIM), full(w2), row(RW_DIM), full(a2), full(g2),
                row(RW_DIM), row(RW_DIM), full(ones_bd)]
    args = [p_rw, p_rw, mu, w0, w2, a0, a2, g2, k_k, k_a, ones_bd]
    if has_vres:
        v_first, v0, v1, v2 = vres
        in_specs += [blk, row(RW_DIM), full(v1), full(v2)]
        args += [v_first, v0, v1, v2]
    return pl.pallas_call(
        functools.partial(_rw_prep_body, has_vres=has_vres),
        grid=(t // tm,),
        in_specs=in_specs,
        out_specs=[blk] * 7,
        out_shape=[jax.ShapeDtypeStruct((t, RW_DIM), F32)] * 7,
        compiler_params=_cparams(("arbitrary",)),
        name="rwkv_prep",
    )(*args)


SCAN_T = 128
HEAD_PAIRS = RW_HEADS // 2
GROUP = 64
NT_DIMS = (((1,), (1,)), ((), ()))
TN_DIMS = (((0,), (0,)), ((), ()))


def _split_bf16(x):
    hi = x.astype(BF16)
    return hi, (x - hi.astype(F32)).astype(BF16)


def _group_cumsum(x, rowmod):
    sh = 1
    while sh < GROUP:
        x = x + jnp.where(rowmod >= sh, pltpu.roll(x, sh, axis=0), 0.0)
        sh *= 2
    return x


def _group_last(x, rowmod):
    n = x.shape[0]
    x = jnp.where(rowmod == GROUP - 1, x, 0.0)
    sh = 1
    while sh < GROUP:
        x = x + jnp.where(rowmod < GROUP - sh, pltpu.roll(x, n - sh, axis=0), 0.0)
        sh *= 2
    return x


def _rw_chunk_body(r_ref, lw_ref, k_ref, v_ref, a_ref, b_ref, y_ref,
                   s_ref, w4_ref, rt_ref, bhh_ref, bhl_ref, kh_ref, vc_ref, c1c_ref, p8_ref,
                   z_ref, ys_ref, vv_ref, arb_ref, ark_ref):
    @pl.when(pl.program_id(0) == 0)
    def _():
        s_ref[...] = jnp.zeros_like(s_ref)

    tb = r_ref.shape[0]
    ng = tb // GROUP
    pairs = range(HEAD_PAIRS)
    items = [(p, h) for p in pairs for h in range(2)]
    rowmod = lax.broadcasted_iota(jnp.int32, (tb, LANES), 0) & (GROUP - 1)
    lo = lax.broadcasted_iota(jnp.int32, (tb, LANES), 1) < RW_HEAD
    ti = lax.broadcasted_iota(jnp.int32, (tb, tb), 0)
    si = lax.broadcasted_iota(jnp.int32, (tb, tb), 1)
    same = (ti // GROUP) == (si // GROUP)
    strict = same & (si < ti)
    incl = same & (si <= ti)
    bdot = lambda x, y: jnp.dot(x, y, preferred_element_type=F32)
    ntdot = lambda x, y: lax.dot_general(x, y, NT_DIMS, preferred_element_type=F32)

    pre = []
    for p in pairs:
        sl = slice(p * LANES, (p + 1) * LANES)
        lw = lw_ref[:, sl]
        cs = _group_cumsum(lw, rowmod)
        cse = _group_last(cs, rowmod)
        pend = jnp.exp(cse - cs)
        pinv = jnp.exp(-cs)
        at_hi, at_lo = _split_bf16(a_ref[:, sl] * jnp.exp(cs - lw))
        bh_hi, bh_lo = _split_bf16(b_ref[:, sl] * pend)
        pre.append(dict(
            at_hi=at_hi, at_lo=at_lo, bh_hi=bh_hi, bh_lo=bh_lo,
            rt=(r_ref[:, sl] * jnp.exp(cs)).astype(BF16),
            bt=(b_ref[:, sl] * pinv).astype(BF16), kt=(k_ref[:, sl] * pinv).astype(BF16),
            kh=(k_ref[:, sl] * pend).astype(BF16), v=v_ref[:, sl], p8=jnp.exp(cse)))
    zero = jnp.zeros((tb, LANES), BF16)
    grams = []
    for p in pairs:
        d = pre[p]
        lhs = jnp.concatenate([jnp.where(lo, d["at_hi"], zero), jnp.where(lo, zero, d["at_hi"]),
                               jnp.where(lo, d["rt"], zero), jnp.where(lo, zero, d["rt"])], axis=0)
        grams.append(ntdot(lhs, jnp.concatenate([d["bt"], d["kt"]], axis=0)))
    mask = lambda h, x: jnp.where(lo, x, zero) if h == 0 else jnp.where(lo, zero, x)
    n1, mk, npow, tm1, vh, ath = {}, {}, {}, {}, {}, {}
    for p, h in items:
        g = grams[p]
        n1[p, h] = jnp.where(strict, g[h * tb:(h + 1) * tb, :tb], 0.0)
        mk[p, h] = jnp.where(strict, g[h * tb:(h + 1) * tb, tb:], 0.0)
        arb_ref[p, h] = jnp.where(incl, g[(2 + h) * tb:(3 + h) * tb, :tb], 0.0).astype(BF16)
        ark_ref[p, h] = jnp.where(incl, g[(2 + h) * tb:(3 + h) * tb, tb:], 0.0).astype(BF16)
        tm1[p, h] = n1[p, h]
        npow[p, h] = n1[p, h].astype(BF16)
        vh[p, h] = pre[p]["v"][:, h * RW_HEAD:(h + 1) * RW_HEAD]
        ath[p, h] = mask(h, pre[p]["at_hi"])
    sh = 2
    while sh < GROUP:
        sq = {it: bdot(npow[it], npow[it]) for it in items}
        for it in items:
            npow[it] = sq[it].astype(BF16)
        for it in items:
            tm1[it] = tm1[it] + sq[it] + bdot(tm1[it].astype(BF16), npow[it])
        sh *= 2
    tm1 = {it: tm1[it].astype(BF16) for it in items}
    tmk = {it: (mk[it] + bdot(tm1[it], mk[it].astype(BF16))).astype(BF16) for it in items}
    corr = {it: bdot(tm1[it], ath[it]).astype(BF16) for it in items}
    vt = {it: vh[it].T.astype(BF16) for it in items}
    vh = {it: vh[it].astype(BF16) for it in items}
    c1c = {it: ntdot(vt[it], tmk[it]) for it in items}
    for p, h in items:
        vv_ref[p, h] = vh[p, h]
    for p in pairs:
        d = pre[p]
        for g in range(ng):
            rows = slice(g * GROUP, (g + 1) * GROUP)
            both = lambda f: jnp.concatenate([f(0), f(1)], axis=0)
            w4_ref[p, g] = jnp.concatenate(
                [both(lambda h: ath[p, h][rows]), both(lambda h: mask(h, d["at_lo"])[rows]),
                 both(lambda h: corr[p, h][rows]), both(lambda h: ath[p, h][rows])], axis=1)
            rt_ref[p, g] = both(lambda h: mask(h, d["rt"])[rows])
            bhh_ref[p, g] = both(lambda h: mask(h, d["bh_hi"])[rows])
            bhl_ref[p, g] = both(lambda h: mask(h, d["bh_lo"])[rows])
            kh_ref[p, g] = both(lambda h: mask(h, d["kh"])[rows])
            vc_ref[p, g] = jnp.concatenate([vt[p, 0][:, rows], vt[p, 1][:, rows]], axis=1)
            c1c_ref[p, g] = jnp.concatenate([c1c[p, 0][:, rows], c1c[p, 1][:, rows]], axis=1)
            p8_ref[p, g] = d["p8"][g * GROUP:g * GROUP + SUBLANES]

    def group_step(g, carry):
        r0 = pl.multiple_of(g * GROUP, GROUP)
        s0 = [s_ref[p] for p in pairs]
        sp = [_split_bf16(s) for s in s0]
        s4 = [jnp.concatenate([hi, hi, hi, lo_], axis=1) for hi, lo_ in sp]
        zc = [ntdot(s4[p], w4_ref[p, g]) + c1c_ref[p, g] for p in pairs]
        vk = [bdot(vc_ref[p, g], kh_ref[p, g]) for p in pairs]
        zs = [_split_bf16(z) for z in zc]
        for p in pairs:
            z_hi, z_lo = zs[p]
            upd = (bdot(z_hi, bhh_ref[p, g]) + bdot(z_lo, bhh_ref[p, g])
                   + bdot(z_hi, bhl_ref[p, g]) + vk[p])
            s_ref[p] = s0[p] * p8_ref[p, g][0:1, :] + upd
        for p in pairs:
            zr = zc[p].T
            yr = ntdot(rt_ref[p, g], sp[p][0])
            for h in range(2):
                z_ref[p, h, pl.ds(r0, GROUP), :] = zr[h * GROUP:(h + 1) * GROUP].astype(BF16)
                ys_ref[p, h, pl.ds(r0, GROUP), :] = yr[h * GROUP:(h + 1) * GROUP]
        return carry

    lax.fori_loop(0, ng, group_step, 0)

    ys = {it: ys_ref[it] + bdot(arb_ref[it], z_ref[it]) + bdot(ark_ref[it], vv_ref[it])
          for it in items}
    for p in pairs:
        y_ref[:, p * LANES:(p + 1) * LANES] = jnp.concatenate([ys[p, 0], ys[p, 1]], axis=1)


def _rw_scan_chunked(r, lw, k, v, a, b):
    t = r.shape[0]
    ng = SCAN_T // GROUP
    g2 = 2 * GROUP
    blk = pl.BlockSpec((SCAN_T, RW_DIM), lambda i: (i, 0))
    per_head = lambda n, dt: pltpu.VMEM((HEAD_PAIRS, 2, SCAN_T, n), dt)
    per_group = lambda rows, n, dt: pltpu.VMEM((HEAD_PAIRS, ng, rows, n), dt)
    return pl.pallas_call(
        _rw_chunk_body,
        grid=(t // SCAN_T,),
        in_specs=[blk] * 6,
        out_specs=blk,
        out_shape=jax.ShapeDtypeStruct((t, RW_DIM), F32),
        scratch_shapes=[pltpu.VMEM((HEAD_PAIRS, RW_HEAD, LANES), F32),
                        per_group(g2, 4 * LANES, BF16), per_group(g2, LANES, BF16),
                        per_group(g2, LANES, BF16), per_group(g2, LANES, BF16),
                        per_group(g2, LANES, BF16), per_group(RW_HEAD, g2, BF16),
                        per_group(RW_HEAD, g2, F32),
                        per_group(SUBLANES, LANES, F32),
                        per_head(RW_HEAD, BF16), per_head(RW_HEAD, F32), per_head(RW_HEAD, BF16),
                        per_head(SCAN_T, BF16), per_head(SCAN_T, BF16)],
        compiler_params=_cparams(("arbitrary",)),
        name="rwkv_scan",
    )(r, lw, k, v, a, b)


def _rw_post_body(y_ref, r_ref, k_ref, v_ref, g_ref, lw_ref, lb_ref, rk_ref, bd_ref, o_ref):
    bd = bd_ref[...]
    y = y_ref[...]
    inv_n = 1.0 / RW_HEAD
    d = y - _group_sum(y, bd) * inv_n
    var = _group_sum(d * d, bd) * inv_n
    yn = d * lax.rsqrt(var + GN_EPS) * lw_ref[...] + lb_ref[...]
    bonus = _group_sum(r_ref[...] * k_ref[...] * rk_ref[...], bd) * v_ref[...]
    o_ref[...] = ((yn + bonus) * g_ref[...]).astype(o_ref.dtype)


def _rw_post(y, r, k, v, g, lnx_w, lnx_b, r_k, ones_bd):
    t = y.shape[0]
    tm = min(256, t)
    blk = pl.BlockSpec((tm, RW_DIM), lambda i: (i, 0))
    row = pl.BlockSpec((1, RW_DIM), lambda i: (0, 0))
    return pl.pallas_call(
        _rw_post_body,
        grid=(t // tm,),
        in_specs=[blk] * 5 + [row] * 3 + [pl.BlockSpec(ones_bd.shape, lambda i: (0, 0))],
        out_specs=blk,
        out_shape=jax.ShapeDtypeStruct((t, RW_DIM), BF16),
        compiler_params=_cparams(("arbitrary",)),
        name="rwkv_post",
    )(y, r, k, v, g, lnx_w, lnx_b, r_k, ones_bd)


def _rms(x, n):
    return x * lax.rsqrt(jnp.sum(x * x, axis=-1, keepdims=True) * (1.0 / n) + RMS_EPS)


def _mla_prep_body(p_ref, cos_ref, sin_ref, qn_ref, kvn_ref, wq_ref, wk_ref, wv_ref,
                   qnn_ref, qnr_ref, knn_ref, knr_ref, q_out, k_out, v_out):
    p = p_ref[...]
    cos = cos_ref[...]
    sin = sin_ref[...]
    scale = (QK_NOPE + QK_ROPE) ** -0.5 * LOG2_E

    def rope(x):
        return x * cos + pltpu.roll(x, ROPE_PAD // 2, axis=1) * sin

    cq = _rms(p[:, :Q_LORA], Q_LORA) * qn_ref[...]
    ckv = _rms(p[:, Q_LORA:Q_LORA + KV_LORA], KV_LORA) * kvn_ref[...]
    qf = _bdot(cq, wq_ref[...])
    kn = _bdot(ckv, wk_ref[...])
    v_out[...] = _bdot(ckv, wv_ref[...]).astype(v_out.dtype)
    kr = rope(_rms(p[:, Q_LORA + KV_LORA:], QK_ROPE) * knr_ref[...]).astype(k_out.dtype)
    for h in range(MLA_HEADS):
        o = h * QK_PAD
        qn = _rms(qf[:, o:o + QK_NOPE], QK_NOPE) * qnn_ref[...]
        qr = rope(_rms(qf[:, o + QK_NOPE:o + QK_PAD], QK_ROPE) * qnr_ref[...])
        q_out[:, o:o + QK_NOPE] = (qn * scale).astype(q_out.dtype)
        q_out[:, o + QK_NOPE:o + QK_PAD] = (qr * scale).astype(q_out.dtype)
        kh = _rms(kn[:, h * QK_NOPE:(h + 1) * QK_NOPE], QK_NOPE) * knn_ref[...]
        k_out[:, o:o + QK_NOPE] = kh.astype(k_out.dtype)
        k_out[:, o + QK_NOPE:o + QK_PAD] = kr


def _mla_prep(p_mla, cos, sin, q_norm, kv_norm, wq, wk, wv, qnn, qnr, knn, knr):
    t = p_mla.shape[0]
    tm = min(512, t)
    row = lambda a: pl.BlockSpec(a.shape, lambda i: (0, 0))
    blk = lambda n: pl.BlockSpec((tm, n), lambda i: (i, 0))
    return pl.pallas_call(
        _mla_prep_body,
        grid=(t // tm,),
        in_specs=[blk(N_MLA_PAD), blk(ROPE_PAD), blk(ROPE_PAD), row(q_norm), row(kv_norm),
                  row(wq), row(wk), row(wv), row(qnn), row(qnr), row(knn), row(knr)],
        out_specs=[blk(MLA_HEADS * QK_PAD), blk(MLA_HEADS * QK_PAD), blk(MLA_DIM)],
        out_shape=[jax.ShapeDtypeStruct((t, MLA_HEADS * QK_PAD), BF16),
                   jax.ShapeDtypeStruct((t, MLA_HEADS * QK_PAD), BF16),
                   jax.ShapeDtypeStruct((t, MLA_DIM), BF16)],
        compiler_params=_cparams(("arbitrary",)),
        name="mla_prep",
    )(p_mla, cos, sin, q_norm, kv_norm, wq, wk, wv, qnn, qnr, knn, knr)


ATT_Q = 1024
ATT_HALF = ATT_Q // 2
ATT_K = 1024


def _attn_update(q, k, v, m_ref, l_ref, acc_ref, half, col_shift):
    s = lax.dot_general(q, k, (((1,), (1,)), ((), ())), preferred_element_type=F32)
    if col_shift is not None:
        rows = lax.broadcasted_iota(jnp.int32, s.shape, 0)
        cols = lax.broadcasted_iota(jnp.int32, s.shape, 1)
        s = jnp.where(cols <= rows + col_shift, s, -jnp.inf)
    tiles = [s[:, c * LANES:(c + 1) * LANES] for c in range(s.shape[1] // LANES)]
    mx = functools.reduce(jnp.maximum, tiles)
    m_old = m_ref[half]
    m_new = jnp.maximum(m_old, jnp.max(mx, axis=1, keepdims=True))
    alpha = jnp.exp2(m_old - m_new)
    ps = [jnp.exp2(x - m_new) for x in tiles]
    l_ref[half] = alpha * l_ref[half] + functools.reduce(jnp.add, ps)
    p = jnp.concatenate([x.astype(v.dtype) for x in ps], axis=1)
    acc_ref[half] = alpha * acc_ref[half] + jnp.dot(p, v, preferred_element_type=F32)
    m_ref[half] = m_new


def _attn_body(q_ref, k_ref, v_ref, o_ref, m_ref, l_ref, acc_ref):
    qi = pl.program_id(1)
    m_ref[...] = jnp.full_like(m_ref, -jnp.inf)
    l_ref[...] = jnp.zeros_like(l_ref)
    acc_ref[...] = jnp.zeros_like(acc_ref)
    halves = [(0, slice(0, ATT_HALF)), (1, slice(ATT_HALF, ATT_Q))]

    def full_block(j, carry):
        r0 = pl.multiple_of(j * ATT_K, ATT_K)
        k = k_ref[pl.ds(r0, ATT_K), :]
        v = v_ref[pl.ds(r0, ATT_K), :]
        for half, rows in halves:
            _attn_update(q_ref[rows, :], k, v, m_ref, l_ref, acc_ref, half, None)
        return carry

    lax.fori_loop(0, qi * (ATT_Q // ATT_K), full_block, 0)

    d0 = pl.multiple_of(qi * ATT_Q, ATT_Q)
    _attn_update(q_ref[:ATT_HALF, :], k_ref[pl.ds(d0, ATT_HALF), :], v_ref[pl.ds(d0, ATT_HALF), :],
                 m_ref, l_ref, acc_ref, 0, 0)
    _attn_update(q_ref[ATT_HALF:, :], k_ref[pl.ds(d0, ATT_Q), :], v_ref[pl.ds(d0, ATT_Q), :],
                 m_ref, l_ref, acc_ref, 1, ATT_HALF)
    for half, rows in halves:
        denom = jnp.sum(l_ref[half], axis=1, keepdims=True)
        o_ref[rows, :] = (acc_ref[half] / denom).astype(o_ref.dtype)


def _attention(q, k, v):
    t = q.shape[0]
    assert t % ATT_Q == 0 and ATT_Q % ATT_K == 0
    stat = pltpu.VMEM((2, ATT_HALF, LANES), F32)
    return pl.pallas_call(
        _attn_body,
        grid=(MLA_HEADS, t // ATT_Q),
        in_specs=[pl.BlockSpec((ATT_Q, QK_PAD), lambda h, i: (i, h)),
                  pl.BlockSpec((t, QK_PAD), lambda h, i: (0, h)),
                  pl.BlockSpec((t, V_HEAD), lambda h, i: (0, h))],
        out_specs=pl.BlockSpec((ATT_Q, V_HEAD), lambda h, i: (i, h)),
        out_shape=jax.ShapeDtypeStruct((t, MLA_DIM), BF16),
        scratch_shapes=[stat, stat, pltpu.VMEM((2, ATT_HALF, V_HEAD), F32)],
        compiler_params=_cparams(("arbitrary", "arbitrary")),
        name="mla_attention",
    )(q, k, v)


def _sg_body(p_ref, lw_ref, lb_ref, ws_ref, bs_ref, o_ref):
    tm = p_ref.shape[0]
    u = _gelu_tanh(p_ref[:, :SG_DIM])
    gv = _gelu_tanh(p_ref[:, SG_DIM:])
    mu = jnp.mean(gv, axis=-1, keepdims=True)
    d = gv - mu
    var = jnp.mean(d * d, axis=-1, keepdims=True)
    vn = (d * lax.rsqrt(var + LN_EPS) * lw_ref[...] + lb_ref[...]).astype(BF16)
    rows = lax.broadcasted_iota(jnp.int32, (CHUNK, CHUNK), 0)
    cols = lax.broadcasted_iota(jnp.int32, (CHUNK, CHUNK), 1)
    tril = cols <= rows
    for g in range(SG_GROUPS):
        gs = slice(g * SG_GROUP_DIM, (g + 1) * SG_GROUP_DIM)
        wg = jnp.where(tril, ws_ref[g], 0.0).astype(BF16)
        for c in range(tm // CHUNK):
            cs = slice(c * CHUNK, (c + 1) * CHUNK)
            s = jnp.dot(wg, vn[cs, gs], preferred_element_type=F32) + bs_ref[:, gs]
            o_ref[cs, gs] = (u[cs, gs] * s).astype(o_ref.dtype)


def _spatial_gating(p_sg, ln_w, ln_b, ws, bs_full):
    t = p_sg.shape[0]
    tm = min(256, t)
    row = pl.BlockSpec((1, SG_DIM), lambda i: (0, 0))
    return pl.pallas_call(
        _sg_body,
        grid=(t // tm,),
        in_specs=[pl.BlockSpec((tm, 2 * SG_DIM), lambda i: (i, 0)), row, row,
                  pl.BlockSpec(ws.shape, lambda i: (0, 0, 0)),
                  pl.BlockSpec(bs_full.shape, lambda i: (0, 0))],
        out_specs=pl.BlockSpec((tm, SG_DIM), lambda i: (i, 0)),
        out_shape=jax.ShapeDtypeStruct((t, SG_DIM), BF16),
        compiler_params=_cparams(("arbitrary",)),
        name="spatial_gating",
    )(p_sg, ln_w, ln_b, ws, bs_full)


def _merge_body(ya_ref, yb_ref, yc_ref, wa_ref, wb_ref, wc_ref, ga_ref, gb_ref, gc_ref, o_ref):
    m = ga_ref[...].astype(F32) * jnp.dot(ya_ref[...], wa_ref[...], preferred_element_type=F32)
    m += gb_ref[...].astype(F32) * jnp.dot(yb_ref[...], wb_ref[...], preferred_element_type=F32)
    m += gc_ref[...].astype(F32) * jnp.dot(yc_ref[...], wc_ref[...], preferred_element_type=F32)
    o_ref[...] = m.astype(o_ref.dtype)


def _merge(ya, yb, yc, wa, wb, wc, gates):
    t = ya.shape[0]
    d = wa.shape[1]
    tm = min(512, t)
    tn = 1024
    nj = d // tn
    yblk = pl.BlockSpec((tm, ya.shape[1]), lambda i, j: (i, 0))
    wblk = pl.BlockSpec((wa.shape[0], tn), lambda i, j: (0, j))
    gblk = lambda b: pl.BlockSpec((tm, tn), lambda i, j: (i, b * nj + j))
    return pl.pallas_call(
        _merge_body,
        grid=(t // tm, nj),
        in_specs=[yblk, yblk, yblk, wblk, wblk, wblk, gblk(0), gblk(1), gblk(2)],
        out_specs=pl.BlockSpec((tm, tn), lambda i, j: (i, j)),
        out_shape=jax.ShapeDtypeStruct((t, d), BF16),
        compiler_params=_cparams(("arbitrary", "arbitrary")),
        name="branch_merge",
    )(ya, yb, yc, wa, wb, wc, gates, gates, gates)


CONV_CARRY = 8


def _ffn_up_body(h_ref, wg_ref, wv_ref, cwg_ref, cwv_ref, cbg_ref, cbv_ref, o_ref,
                 cg_ref, cv_ref, wg16_ref, wv16_ref):
    i = pl.program_id(1)

    @pl.when(i == 0)
    def _():
        cg_ref[...] = jnp.zeros_like(cg_ref)
        cv_ref[...] = jnp.zeros_like(cv_ref)
        wg16_ref[...] = wg_ref[...].astype(BF16)
        wv16_ref[...] = wv_ref[...].astype(BF16)

    h = h_ref[...]
    tm = h.shape[0]
    rowid = lax.broadcasted_iota(jnp.int32, (tm, 1), 0)

    def conv(w_ref, cw_ref, cb_ref, carry_ref):
        up = jnp.dot(h, w_ref[...], preferred_element_type=F32)
        c1 = carry_ref[CONV_CARRY - 1:CONV_CARRY, :]
        c2 = carry_ref[CONV_CARRY - 2:CONV_CARRY - 1, :]
        s1 = jnp.where(rowid == 0, c1, pltpu.roll(up, 1, axis=0))
        s2 = jnp.where(rowid == 0, c2, jnp.where(rowid == 1, c1, pltpu.roll(up, 2, axis=0)))
        carry_ref[...] = up[tm - CONV_CARRY:, :]
        return cb_ref[...] + cw_ref[0:1, :] * s2 + cw_ref[1:2, :] * s1 + cw_ref[2:3, :] * up

    gate = conv(wg16_ref, cwg_ref, cbg_ref, cg_ref)
    val = conv(wv16_ref, cwv_ref, cbv_ref, cv_ref)
    o_ref[...] = (gate * _sigmoid(gate) * val).astype(o_ref.dtype)


def _ffn_up(h, w_up, layer, conv_w, conv_b):
    t, d = h.shape
    tm = min(1024, t)
    tn = 512
    nj = D_FF // tn
    return pl.pallas_call(
        _ffn_up_body,
        grid=(nj, t // tm),
        in_specs=[pl.BlockSpec((tm, d), lambda j, i: (i, 0)),
                  pl.BlockSpec((None, d, tn), lambda j, i: (layer, 0, j)),
                  pl.BlockSpec((None, d, tn), lambda j, i: (layer, 0, nj + j)),
                  pl.BlockSpec((CONV_W, tn), lambda j, i: (0, j)),
                  pl.BlockSpec((CONV_W, tn), lambda j, i: (0, nj + j)),
                  pl.BlockSpec((1, tn), lambda j, i: (0, j)),
                  pl.BlockSpec((1, tn), lambda j, i: (0, nj + j))],
        out_specs=pl.BlockSpec((tm, tn), lambda j, i: (i, j)),
        out_shape=jax.ShapeDtypeStruct((t, D_FF), BF16),
        scratch_shapes=[pltpu.VMEM((CONV_CARRY, tn), F32), pltpu.VMEM((CONV_CARRY, tn), F32),
                        pltpu.VMEM((d, tn), BF16), pltpu.VMEM((d, tn), BF16)],
        compiler_params=_cparams(("arbitrary", "arbitrary")),
        name="ffn_up_conv",
    )(h, w_up, w_up, conv_w, conv_w, conv_b, conv_b)


def _pad_cols(a, n):
    return jnp.pad(a, ((0, 0), (0, n - a.shape[1])))


def _rw_cols(a):
    o = 3 * RW_DIM
    return jnp.concatenate([a[:, :o], _pad_cols(a[:, o:o + W_LORA], LORA_PAD),
                            _pad_cols(a[:, o + W_LORA:o + W_LORA + A_LORA], LORA_PAD),
                            a[:, o + W_LORA + A_LORA:N_RW]], axis=1)


def _pad_rows(a, n):
    return jnp.pad(a, ((0, n - a.shape[0]), (0, 0)))


def kernel(x, c, positions, ada_w, ada_b, norm_mix_g, norm_ffn_g, w_in, rw_mu, rw_w0, rw_w2, rw_a0, rw_a2, rw_g2, rw_kk, rw_ka, rw_rk, rw_lnx_w, rw_lnx_b, rw_v0, rw_v1, rw_v2, mla_q_norm, mla_kv_norm, mla_w_uq, mla_w_ukv, mla_qn_nope, mla_qn_rope, mla_kn_nope, mla_kn_rope, sg_ln_w, sg_ln_b, sg_ws, sg_b, w_br_a, w_br_b, w_br_c, w_out, ffn_up, ffn_conv, ffn_conv_b, ffn_down):
    b_, t, d = x.shape
    assert b_ == 1 and d == D_MODEL
    depth = w_in.shape[0]
    xs = x.reshape(t, d)

    mod = _ada_all(c, ada_w, ada_b)
    cos, sin = _rope_tables(positions, t)
    lane_head = jnp.arange(LANES) // RW_HEAD
    ones_bd = (jnp.tile(lane_head, 2)[:, None] == lane_head[None, :]).astype(BF16)
    row = lambda a: a.reshape(1, -1)

    v_first = None
    for l in range(depth):
        sh1, sc1, gt1, sh2, sc2, gt2 = [mod[l, :, i * d:(i + 1) * d] for i in range(6)]

        h = _norm_mod(xs, row(norm_mix_g[l]), sc1, sh1)
        wl = w_in[l]
        o_dq = N_RW
        o_kr = N_RW + Q_LORA + KV_LORA
        o_sg = o_kr + QK_ROPE
        o_gt = o_sg + 2 * SG_DIM
        w_rw = _rw_cols(wl).astype(BF16)
        w_mla = jnp.concatenate([wl[:, o_dq:o_kr], _rope_pad(wl[:, o_kr:o_sg])], axis=1).astype(BF16)
        p_rw = _matmul(h, w_rw, out_dtype=F32, tm=1024, tn=896, name="proj_rw")
        p_mla = _matmul(h, w_mla, out_dtype=F32, tm=1024, tn=N_MLA_PAD, name="proj_mla")
        p_sg = _matmul(h, w_in, layer=l, col0=o_sg, n=2 * SG_DIM, out_dtype=F32, tm=2048, tn=512,
                       name="proj_sg")
        gates = _matmul(h, w_in, layer=l, col0=o_gt, n=3 * d, out_dtype=BF16, tm=2048, tn=512,
                        epi="sigmoid", name="proj_gates")

        vres = None
        if l > 0:
            vres = (v_first, row(rw_v0[l - 1]), rw_v1[l - 1].astype(BF16),
                    rw_v2[l - 1].astype(BF16))
        r_, w_, k_, v_, a_, b2_, g_ = _rw_prep(
            p_rw, _rw_cols(row(rw_mu[l])), row(rw_w0[l]),
            _pad_rows(rw_w2[l], LORA_PAD).astype(BF16), row(rw_a0[l]),
            _pad_rows(rw_a2[l], LORA_PAD).astype(BF16), rw_g2[l].astype(BF16),
            row(rw_kk[l]), row(rw_ka[l]), ones_bd, vres)
        if l == 0:
            v_first = v_
        y_scan = _rw_scan_chunked(r_, w_, k_, v_, a_, b2_)
        y_a = _rw_post(y_scan, r_, k_, v_, g_, row(rw_lnx_w[l]), row(rw_lnx_b[l]),
                       row(rw_rk[l]), ones_bd)

        wq = mla_w_uq[l].reshape(Q_LORA, MLA_HEADS, QK_NOPE + QK_ROPE)
        wq = jnp.concatenate([wq[..., :QK_NOPE], _rope_pad(wq[..., QK_NOPE:])], axis=-1)
        wq = wq.reshape(Q_LORA, MLA_HEADS * QK_PAD).astype(BF16)
        wkv = mla_w_ukv[l].reshape(KV_LORA, MLA_HEADS, QK_NOPE + V_HEAD)
        wk = wkv[..., :QK_NOPE].reshape(KV_LORA, MLA_HEADS * QK_NOPE).astype(BF16)
        wv = wkv[..., QK_NOPE:].reshape(KV_LORA, MLA_DIM).astype(BF16)
        q, k, v = _mla_prep(p_mla, cos, sin, row(mla_q_norm[l]), row(mla_kv_norm[l]), wq, wk, wv,
                            row(mla_qn_nope[l]), row(_rope_pad(mla_qn_rope[l])),
                            row(mla_kn_nope[l]), row(_rope_pad(mla_kn_rope[l])))
        y_b = _attention(q, k, v)

        bs_full = jnp.repeat(sg_b[l].T, SG_GROUP_DIM, axis=1)
        y_c = _spatial_gating(p_sg, row(sg_ln_w[l]), row(sg_ln_b[l]), sg_ws[l], bs_full)

        merged = _merge(y_a, y_b, y_c, w_br_a[l].astype(BF16), w_br_b[l].astype(BF16),
                        w_br_c[l].astype(BF16), gates)
        xs = _matmul(merged, w_out[l].astype(BF16), out_dtype=F32, tm=1024, tn=1024,
                     epi="residual", res=xs, gate=gt1, name="out_proj")

        h = _norm_mod(xs, row(norm_ffn_g[l]), sc2, sh2)
        act = _ffn_up(h, ffn_up, l, ffn_conv[l], row(ffn_conv_b[l]))
        xs = _matmul(act, ffn_down[l].astype(BF16), out_dtype=F32, tm=1024, tn=1024,
                     tk=D_FF // 2, epi="residual", res=xs, gate=gt2, name="ffn_down")

    return xs.reshape(b_, t, d)
```

```python
import functools

import jax
import jax.numpy as jnp
from jax import lax
from jax.experimental import pallas as pl
from jax.experimental.pallas import tpu as pltpu

F32 = jnp.float32
BF16 = jnp.bfloat16

D_MODEL = 2048
RW_HEAD = 64
RW_HEADS = 16
RW_DIM = RW_HEADS * RW_HEAD
W_LORA = 96
A_LORA = 96
V_LORA = 64
G_LORA = 256
GN_EPS = 64e-5
MLA_HEADS = 8
Q_LORA = 512
KV_LORA = 512
QK_NOPE = 128
QK_ROPE = 64
V_HEAD = 128
MLA_DIM = MLA_HEADS * V_HEAD
ROPE_THETA = 10000.0
CHUNK = 128
SG_GROUPS = 8
SG_GROUP_DIM = 128
SG_DIM = SG_GROUPS * SG_GROUP_DIM
D_FF = 5632
CONV_W = 3
RMS_EPS = 1e-6
LN_EPS = 1e-5

LANES = 128
SUBLANES = 8
LORA_PAD = 128
N_RW = 3 * RW_DIM + W_LORA + A_LORA + G_LORA
N_RW_PAD = 3 * RW_DIM + 2 * LORA_PAD + G_LORA
ROPE_PAD = 128
QK_PAD = QK_NOPE + ROPE_PAD
N_MLA_PAD = Q_LORA + KV_LORA + ROPE_PAD
VMEM_LIMIT = 52 * 1024 * 1024
LOG2_E = 1.4426950408889634


def _cparams(sem):
    return pltpu.CompilerParams(dimension_semantics=sem, vmem_limit_bytes=VMEM_LIMIT)


def _sigmoid(x):
    return 1.0 / (1.0 + jnp.exp(-x))


def _softplus(x):
    return jnp.maximum(x, 0.0) + jnp.log(1.0 + jnp.exp(-jnp.abs(x)))


def _gelu_tanh(x):
    return 0.5 * x * (1.0 + jnp.tanh(0.7978845608028654 * (x + 0.044715 * (x * x * x))))


def _bdot(a, b):
    return jnp.dot(a.astype(BF16), b, preferred_element_type=F32)


def _group_sum(x, ones2):
    hi = x.astype(BF16)
    lo = (x - hi.astype(F32)).astype(BF16)
    out = []
    for c in range(x.shape[1] // LANES):
        sl = slice(c * LANES, (c + 1) * LANES)
        out.append(jnp.dot(jnp.concatenate([hi[:, sl], lo[:, sl]], axis=1), ones2,
                           preferred_element_type=F32))
    return jnp.concatenate(out, axis=1)


def _ada_body(c_ref, w_ref, b_ref, o_ref, *, kc):
    d = c_ref.shape[0]
    tn = o_ref.shape[-1]

    def step(i, acc):
        ck = c_ref[pl.ds(i * kc, kc), :]
        sk = ck * _sigmoid(ck)
        wk = w_ref[0, pl.ds(i * kc, kc), :]
        return acc + jnp.sum(wk * sk, axis=0, keepdims=True)

    acc = lax.fori_loop(0, d // kc, step, jnp.zeros((1, tn), F32))
    o_ref[0] = acc + b_ref[0]


def _ada_all(c, ada_w, ada_b):
    nl, d, n = ada_w.shape
    tn = 1024
    out = pl.pallas_call(
        functools.partial(_ada_body, kc=256),
        grid=(nl, n // tn),
        in_specs=[pl.BlockSpec((d, 1), lambda l, j: (0, 0)),
                  pl.BlockSpec((1, d, tn), lambda l, j: (l, 0, j)),
                  pl.BlockSpec((1, 1, tn), lambda l, j: (l, 0, j))],
        out_specs=pl.BlockSpec((1, 1, tn), lambda l, j: (l, 0, j)),
        out_shape=jax.ShapeDtypeStruct((nl, 1, n), F32),
        compiler_params=_cparams(("arbitrary", "arbitrary")),
        name="ada_mod",
    )(c.reshape(d, 1), ada_w, ada_b.reshape(nl, 1, n))
    return out


def _rope_body(pos_ref, inv_ref, msk_ref, sgn_ref, cos_ref, sin_ref):
    ang = pos_ref[...].astype(F32) * inv_ref[...]
    cos_ref[...] = jnp.cos(ang) * msk_ref[...]
    sin_ref[...] = jnp.sin(ang) * sgn_ref[...]


def _rope_pad(v):
    h = QK_ROPE // 2
    z = jnp.zeros(v.shape[:-1] + (ROPE_PAD // 2 - h,), v.dtype)
    return jnp.concatenate([v[..., :h], z, v[..., h:], z], axis=-1)


def _rope_tables(positions, t):
    tm = min(1024, t)
    inv = ROPE_THETA ** (-jnp.arange(0, QK_ROPE, 2, dtype=F32) / QK_ROPE)
    ones = jnp.ones((QK_ROPE // 2,), F32)
    inv_p = _rope_pad(jnp.concatenate([inv, inv]))[None]
    msk_p = _rope_pad(jnp.concatenate([ones, ones]))[None]
    sgn_p = _rope_pad(jnp.concatenate([-ones, ones]))[None]
    row = pl.BlockSpec((1, ROPE_PAD), lambda i: (0, 0))
    blk = pl.BlockSpec((tm, ROPE_PAD), lambda i: (i, 0))
    return pl.pallas_call(
        _rope_body,
        grid=(t // tm,),
        in_specs=[pl.BlockSpec((tm, 1), lambda i: (i, 0)), row, row, row],
        out_specs=[blk, blk],
        out_shape=[jax.ShapeDtypeStruct((t, ROPE_PAD), F32)] * 2,
        compiler_params=_cparams(("arbitrary",)),
        name="rope_tables",
    )(positions.reshape(t, 1), inv_p, msk_p, sgn_p)


def _norm_mod_body(x_ref, g_ref, sc_ref, sh_ref, o_ref):
    x = x_ref[...]
    y = x * lax.rsqrt(jnp.mean(x * x, axis=-1, keepdims=True) + RMS_EPS) * g_ref[...]
    o_ref[...] = (y * (1.0 + sc_ref[...]) + sh_ref[...]).astype(o_ref.dtype)


def _norm_mod(x, g, sc, sh):
    t, d = x.shape
    tm = min(512, t)
    row = pl.BlockSpec((1, d), lambda i: (0, 0))
    return pl.pallas_call(
        _norm_mod_body,
        grid=(t // tm,),
        in_specs=[pl.BlockSpec((tm, d), lambda i: (i, 0)), row, row, row],
        out_specs=pl.BlockSpec((tm, d), lambda i: (i, 0)),
        out_shape=jax.ShapeDtypeStruct((t, d), BF16),
        compiler_params=_cparams(("arbitrary",)),
        name="norm_mod",
    )(x, g, sc, sh)


def _mm_body(*refs, nk, epi):
    if epi == "residual":
        a_ref, b_ref, res_ref, gt_ref, o_ref = refs[:5]
        rest = refs[5:]
    else:
        a_ref, b_ref, o_ref = refs[:3]
        rest = refs[3:]

    def finish(acc):
        if epi == "sigmoid":
            acc = _sigmoid(acc)
        elif epi == "residual":
            acc = res_ref[...] + gt_ref[...] * acc
        o_ref[...] = acc.astype(o_ref.dtype)

    part = jnp.dot(a_ref[...], b_ref[...].astype(BF16), preferred_element_type=F32)
    if nk == 1:
        finish(part)
        return
    acc_ref, = rest
    k = pl.program_id(2)

    @pl.when(k == 0)
    def _():
        acc_ref[...] = part

    @pl.when(k > 0)
    def _():
        acc_ref[...] += part

    @pl.when(k == nk - 1)
    def _():
        finish(acc_ref[...])


def _matmul(a, b, *, out_dtype, tm, tn, tk=None, epi="none", res=None, gate=None, name="matmul",
            layer=None, col0=0, n=None):
    m, kd = a.shape
    n = b.shape[-1] if n is None else n
    tm = min(tm, m)
    tn = min(tn, n)
    tk = kd if tk is None else tk
    nk = kd // tk
    assert m % tm == 0 and n % tn == 0 and kd % tk == 0 and col0 % tn == 0
    if layer is None:
        b_spec = pl.BlockSpec((tk, tn), lambda i, j, k: (k, j))
    else:
        b_spec = pl.BlockSpec((None, tk, tn), lambda i, j, k: (layer, k, col0 // tn + j))
    in_specs = [pl.BlockSpec((tm, tk), lambda i, j, k: (i, k)), b_spec]
    args = [a, b]
    if epi == "residual":
        in_specs += [pl.BlockSpec((tm, tn), lambda i, j, k: (i, j)),
                     pl.BlockSpec((1, tn), lambda i, j, k: (0, j))]
        args += [res, gate]
    scratch = [pltpu.VMEM((tm, tn), F32)] if nk > 1 else []
    return pl.pallas_call(
        functools.partial(_mm_body, nk=nk, epi=epi),
        grid=(m // tm, n // tn, nk),
        in_specs=in_specs,
        out_specs=pl.BlockSpec((tm, tn), lambda i, j, k: (i, j)),
        out_shape=jax.ShapeDtypeStruct((m, n), out_dtype),
        scratch_shapes=scratch,
        compiler_params=_cparams(("arbitrary", "arbitrary", "arbitrary")),
        name=name,
    )(*args)


def _rw_prep_body(*refs, has_vres):
    (p_ref, pprev_ref, mu_ref, w0_ref, w2_ref, a0_ref, a2_ref, g2_ref, kk_ref, ka_ref,
     bd_ref) = refs[:11]
    pos = 11
    if has_vres:
        vf_ref, v0_ref, v1_ref, v2_ref = refs[pos:pos + 4]
        pos += 4
    r_out, w_out, k_out, v_out, a_out, b_out, g_out = refs[pos:pos + 7]

    i = pl.program_id(0)
    p = p_ref[...]
    tm = p.shape[0]
    last = pprev_ref[7:8, :] * (i > 0).astype(F32)
    rowid = lax.broadcasted_iota(jnp.int32, (tm, 1), 0)
    prev = jnp.where(rowid == 0, last, pltpu.roll(p, 1, axis=0))
    pm = p + (prev - p) * mu_ref[...]

    o = 3 * RW_DIM
    r = pm[:, 0:RW_DIM]
    k = pm[:, RW_DIM:2 * RW_DIM]
    v = pm[:, 2 * RW_DIM:o]
    xw = pm[:, o:o + LORA_PAD]
    xa = pm[:, o + LORA_PAD:o + 2 * LORA_PAD]
    xg = pm[:, o + 2 * LORA_PAD:]

    if has_vres:
        mix = _sigmoid(v0_ref[...] + _bdot(_bdot(v, v1_ref[...]), v2_ref[...]))
        v = v + (vf_ref[...] - v) * mix

    w_log = -_softplus(-(w0_ref[...] + _bdot(jnp.tanh(xw), w2_ref[...]))) - 0.5
    log_decay = -jnp.exp(w_log)
    a = _sigmoid(a0_ref[...] + _bdot(xa, a2_ref[...]))
    g = _bdot(_sigmoid(xg), g2_ref[...])

    kk = k * kk_ref[...]
    nrm = jnp.sqrt(_group_sum(kk * kk, bd_ref[...]))
    kk = kk / jnp.maximum(nrm, 1e-12)
    kh = k * (1.0 + (a - 1.0) * ka_ref[...])

    r_out[...] = r
    w_out[...] = log_decay
    k_out[...] = kh
    v_out[...] = v
    a_out[...] = -kk
    b_out[...] = kk * a
    g_out[...] = g


def _rw_prep(p_rw, mu, w0, w2, a0, a2, g2, k_k, k_a, ones_bd, vres):
    t = p_rw.shape[0]
    tm = min(256, t)
    has_vres = vres is not None
    row = lambda n: pl.BlockSpec((1, n), lambda i: (0, 0))
    full = lambda a: pl.BlockSpec(a.shape, lambda i: (0, 0))
    blk = pl.BlockSpec((tm, RW_DIM), lambda i: (i, 0))
    in_specs = [pl.BlockSpec((tm, N_RW_PAD), lambda i: (i, 0)),
                pl.BlockSpec((8, N_RW_PAD), lambda i: (jnp.maximum(i * (tm // 8) - 1, 0), 0)),
                row(N_RW_PAD), row(RW_DIM), full(w2), row(RW_DIM), full(a2), full(g2),
                row(RW_DIM), row(RW_DIM), full(ones_bd)]
    args = [p_rw, p_rw, mu, w0, w2, a0, a2, g2, k_k, k_a, ones_bd]
    if has_vres:
        v_first, v0, v1, v2 = vres
        in_specs += [blk, row(RW_DIM), full(v1), full(v2)]
        args += [v_first, v0, v1, v2]
    return pl.pallas_call(
        functools.partial(_rw_prep_body, has_vres=has_vres),
        grid=(t // tm,),
        in_specs=in_specs,
        out_specs=[blk] * 7,
        out_shape=[jax.ShapeDtypeStruct((t, RW_DIM), F32)] * 7,
        compiler_params=_cparams(("arbitrary",)),
        name="rwkv_prep",
    )(*args)


SCAN_T = 128
HEAD_PAIRS = RW_HEADS // 2
GROUP = 128
NT_DIMS = (((1,), (1,)), ((), ()))
TN_DIMS = (((0,), (0,)), ((), ()))


def _split_bf16(x):
    hi = x.astype(BF16)
    return hi, (x - hi.astype(F32)).astype(BF16)


def _group_cumsum(x, rowmod):
    sh = 1
    while sh < GROUP:
        x = x + jnp.where(rowmod >= sh, pltpu.roll(x, sh, axis=0), 0.0)
        sh *= 2
    return x


def _group_last(x, rowmod):
    n = x.shape[0]
    x = jnp.where(rowmod == GROUP - 1, x, 0.0)
    sh = 1
    while sh < GROUP:
        x = x + jnp.where(rowmod < GROUP - sh, pltpu.roll(x, n - sh, axis=0), 0.0)
        sh *= 2
    return x


def _rw_chunk_body(r_ref, lw_ref, k_ref, v_ref, a_ref, b_ref, y_ref,
                   s_ref, w4_ref, rt_ref, bhh_ref, bhl_ref, kh_ref, vc_ref, c1c_ref, p8_ref,
                   z_ref, ys_ref, vv_ref, arb_ref, ark_ref):
    @pl.when(pl.program_id(0) == 0)
    def _():
        s_ref[...] = jnp.zeros_like(s_ref)

    tb = r_ref.shape[0]
    ng = tb // GROUP
    pairs = range(HEAD_PAIRS)
    items = [(p, h) for p in pairs for h in range(2)]
    rowmod = lax.broadcasted_iota(jnp.int32, (tb, LANES), 0) & (GROUP - 1)
    lo = lax.broadcasted_iota(jnp.int32, (tb, LANES), 1) < RW_HEAD
    ti = lax.broadcasted_iota(jnp.int32, (tb, tb), 0)
    si = lax.broadcasted_iota(jnp.int32, (tb, tb), 1)
    same = (ti // GROUP) == (si // GROUP)
    strict = same & (si < ti)
    incl = same & (si <= ti)
    bdot = lambda x, y: jnp.dot(x, y, preferred_element_type=F32)
    ntdot = lambda x, y: lax.dot_general(x, y, NT_DIMS, preferred_element_type=F32)

    pre = []
    for p in pairs:
        sl = slice(p * LANES, (p + 1) * LANES)
        lw = lw_ref[:, sl]
        cs = _group_cumsum(lw, rowmod)
        cse = _group_last(cs, rowmod)
        pend = jnp.exp(cse - cs)
        pinv = jnp.exp(-cs)
        at_hi, at_lo = _split_bf16(a_ref[:, sl] * jnp.exp(cs - lw))
        bh_hi, bh_lo = _split_bf16(b_ref[:, sl] * pend)
        pre.append(dict(
            at_hi=at_hi, at_lo=at_lo, bh_hi=bh_hi, bh_lo=bh_lo,
            rt=(r_ref[:, sl] * jnp.exp(cs)).astype(BF16),
            bt=(b_ref[:, sl] * pinv).astype(BF16), kt=(k_ref[:, sl] * pinv).astype(BF16),
            kh=(k_ref[:, sl] * pend).astype(BF16), v=v_ref[:, sl], p8=jnp.exp(cse)))
    zero = jnp.zeros((tb, LANES), BF16)
    grams = []
    for p in pairs:
        d = pre[p]
        lhs = jnp.concatenate([jnp.where(lo, d["at_hi"], zero), jnp.where(lo, zero, d["at_hi"]),
                               jnp.where(lo, d["rt"], zero), jnp.where(lo, zero, d["rt"])], axis=0)
        grams.append(ntdot(lhs, jnp.concatenate([d["bt"], d["kt"]], axis=0)))
    mask = lambda h, x: jnp.where(lo, x, zero) if h == 0 else jnp.where(lo, zero, x)
    n1, mk, npow, tm1, vh, ath = {}, {}, {}, {}, {}, {}
    for p, h in items:
        g = grams[p]
        n1[p, h] = jnp.where(strict, g[h * tb:(h + 1) * tb, :tb], 0.0)
        mk[p, h] = jnp.where(strict, g[h * tb:(h + 1) * tb, tb:], 0.0)
        arb_ref[p, h] = jnp.where(incl, g[(2 + h) * tb:(3 + h) * tb, :tb], 0.0).astype(BF16)
        ark_ref[p, h] = jnp.where(incl, g[(2 + h) * tb:(3 + h) * tb, tb:], 0.0).astype(BF16)
        tm1[p, h] = n1[p, h]
        npow[p, h] = n1[p, h].astype(BF16)
        vh[p, h] = pre[p]["v"][:, h * RW_HEAD:(h + 1) * RW_HEAD]
        ath[p, h] = mask(h, pre[p]["at_hi"])
    sh = 2
    while sh < GROUP:
        sq = {it: bdot(npow[it], npow[it]) for it in items}
        for it in items:
            npow[it] = sq[it].astype(BF16)
        for it in items:
            tm1[it] = tm1[it] + sq[it] + bdot(tm1[it].astype(BF16), npow[it])
        sh *= 2
    tm1 = {it: tm1[it].astype(BF16) for it in items}
    tmk = {it: (mk[it] + bdot(tm1[it], mk[it].astype(BF16))).astype(BF16) for it in items}
    corr = {it: bdot(tm1[it], ath[it]).astype(BF16) for it in items}
    vt = {it: vh[it].T.astype(BF16) for it in items}
    vh = {it: vh[it].astype(BF16) for it in items}
    c1c = {it: ntdot(vt[it], tmk[it]) for it in items}
    for p, h in items:
        vv_ref[p, h] = vh[p, h]
    for p in pairs:
        d = pre[p]
        for g in range(ng):
            rows = slice(g * GROUP, (g + 1) * GROUP)
            both = lambda f: jnp.concatenate([f(0), f(1)], axis=0)
            w4_ref[p, g] = jnp.concatenate(
                [both(lambda h: ath[p, h][rows]), both(lambda h: mask(h, d["at_lo"])[rows]),
                 both(lambda h: corr[p, h][rows]), both(lambda h: ath[p, h][rows])], axis=1)
            rt_ref[p, g] = both(lambda h: mask(h, d["rt"])[rows])
            bhh_ref[p, g] = both(lambda h: mask(h, d["bh_hi"])[rows])
            bhl_ref[p, g] = both(lambda h: mask(h, d["bh_lo"])[rows])
            kh_ref[p, g] = both(lambda h: mask(h, d["kh"])[rows])
            vc_ref[p, g] = jnp.concatenate([vt[p, 0][:, rows], vt[p, 1][:, rows]], axis=1)
            c1c_ref[p, g] = jnp.concatenate([c1c[p, 0][:, rows], c1c[p, 1][:, rows]], axis=1)
            p8_ref[p, g] = d["p8"][g * GROUP:g * GROUP + SUBLANES]

    def group_step(g, carry):
        r0 = pl.multiple_of(g * GROUP, GROUP)
        s0 = [s_ref[p] for p in pairs]
        sp = [_split_bf16(s) for s in s0]
        s4 = [jnp.concatenate([hi, hi, hi, lo_], axis=1) for hi, lo_ in sp]
        zc = [ntdot(s4[p], w4_ref[p, g]) + c1c_ref[p, g] for p in pairs]
        vk = [bdot(vc_ref[p, g], kh_ref[p, g]) for p in pairs]
        zs = [_split_bf16(z) for z in zc]
        for p in pairs:
            z_hi, z_lo = zs[p]
            upd = (bdot(z_hi, bhh_ref[p, g]) + bdot(z_lo, bhh_ref[p, g])
                   + bdot(z_hi, bhl_ref[p, g]) + vk[p])
            s_ref[p] = s0[p] * p8_ref[p, g][0:1, :] + upd
        for p in pairs:
            zr = zc[p].T
            yr = ntdot(rt_ref[p, g], sp[p][0])
            for h in range(2):
                z_ref[p, h, pl.ds(r0, GROUP), :] = zr[h * GROUP:(h + 1) * GROUP].astype(BF16)
                ys_ref[p, h, pl.ds(r0, GROUP), :] = yr[h * GROUP:(h + 1) * GROUP]
        return carry

    lax.fori_loop(0, ng, group_step, 0)

    ys = {it: ys_ref[it] + bdot(arb_ref[it], z_ref[it]) + bdot(ark_ref[it], vv_ref[it])
          for it in items}
    for p in pairs:
        y_ref[:, p * LANES:(p + 1) * LANES] = jnp.concatenate([ys[p, 0], ys[p, 1]], axis=1)


def _rw_scan_chunked(r, lw, k, v, a, b):
    t = r.shape[0]
    ng = SCAN_T // GROUP
    g2 = 2 * GROUP
    blk = pl.BlockSpec((SCAN_T, RW_DIM), lambda i: (i, 0))
    per_head = lambda n, dt: pltpu.VMEM((HEAD_PAIRS, 2, SCAN_T, n), dt)
    per_group = lambda rows, n, dt: pltpu.VMEM((HEAD_PAIRS, ng, rows, n), dt)
    return pl.pallas_call(
        _rw_chunk_body,
        grid=(t // SCAN_T,),
        in_specs=[blk] * 6,
        out_specs=blk,
        out_shape=jax.ShapeDtypeStruct((t, RW_DIM), F32),
        scratch_shapes=[pltpu.VMEM((HEAD_PAIRS, RW_HEAD, LANES), F32),
                        per_group(g2, 4 * LANES, BF16), per_group(g2, LANES, BF16),
                        per_group(g2, LANES, BF16), per_group(g2, LANES, BF16),
                        per_group(g2, LANES, BF16), per_group(RW_HEAD, g2, BF16),
                        per_group(RW_HEAD, g2, F32),
                        per_group(SUBLANES, LANES, F32),
                        per_head(RW_HEAD, BF16), per_head(RW_HEAD, F32), per_head(RW_HEAD, BF16),
                        per_head(SCAN_T, BF16), per_head(SCAN_T, BF16)],
        compiler_params=_cparams(("arbitrary",)),
        name="rwkv_scan",
    )(r, lw, k, v, a, b)


def _rw_post_body(y_ref, r_ref, k_ref, v_ref, g_ref, lw_ref, lb_ref, rk_ref, bd_ref, o_ref):
    bd = bd_ref[...]
    y = y_ref[...]
    inv_n = 1.0 / RW_HEAD
    d = y - _group_sum(y, bd) * inv_n
    var = _group_sum(d * d, bd) * inv_n
    yn = d * lax.rsqrt(var + GN_EPS) * lw_ref[...] + lb_ref[...]
    bonus = _group_sum(r_ref[...] * k_ref[...] * rk_ref[...], bd) * v_ref[...]
    o_ref[...] = ((yn + bonus) * g_ref[...]).astype(o_ref.dtype)


def _rw_post(y, r, k, v, g, lnx_w, lnx_b, r_k, ones_bd):
    t = y.shape[0]
    tm = min(256, t)
    blk = pl.BlockSpec((tm, RW_DIM), lambda i: (i, 0))
    row = pl.BlockSpec((1, RW_DIM), lambda i: (0, 0))
    return pl.pallas_call(
        _rw_post_body,
        grid=(t // tm,),
        in_specs=[blk] * 5 + [row] * 3 + [pl.BlockSpec(ones_bd.shape, lambda i: (0, 0))],
        out_specs=blk,
        out_shape=jax.ShapeDtypeStruct((t, RW_DIM), BF16),
        compiler_params=_cparams(("arbitrary",)),
        name="rwkv_post",
    )(y, r, k, v, g, lnx_w, lnx_b, r_k, ones_bd)


def _rms(x, n):
    return x * lax.rsqrt(jnp.sum(x * x, axis=-1, keepdims=True) * (1.0 / n) + RMS_EPS)


def _mla_prep_body(p_ref, cos_ref, sin_ref, qn_ref, kvn_ref, wq_ref, wk_ref, wv_ref,
                   qnn_ref, qnr_ref, knn_ref, knr_ref, q_out, k_out, v_out):
    p = p_ref[...]
    cos = cos_ref[...]
    sin = sin_ref[...]
    scale = (QK_NOPE + QK_ROPE) ** -0.5 * LOG2_E

    def rope(x):
        return x * cos + pltpu.roll(x, ROPE_PAD // 2, axis=1) * sin

    cq = _rms(p[:, :Q_LORA], Q_LORA) * qn_ref[...]
    ckv = _rms(p[:, Q_LORA:Q_LORA + KV_LORA], KV_LORA) * kvn_ref[...]
    qf = _bdot(cq, wq_ref[...])
    kn = _bdot(ckv, wk_ref[...])
    v_out[...] = _bdot(ckv, wv_ref[...]).astype(v_out.dtype)
    kr = rope(_rms(p[:, Q_LORA + KV_LORA:], QK_ROPE) * knr_ref[...]).astype(k_out.dtype)
    for h in range(MLA_HEADS):
        o = h * QK_PAD
        qn = _rms(qf[:, o:o + QK_NOPE], QK_NOPE) * qnn_ref[...]
        qr = rope(_rms(qf[:, o + QK_NOPE:o + QK_PAD], QK_ROPE) * qnr_ref[...])
        q_out[:, o:o + QK_NOPE] = (qn * scale).astype(q_out.dtype)
        q_out[:, o + QK_NOPE:o + QK_PAD] = (qr * scale).astype(q_out.dtype)
        kh = _rms(kn[:, h * QK_NOPE:(h + 1) * QK_NOPE], QK_NOPE) * knn_ref[...]
        k_out[:, o:o + QK_NOPE] = kh.astype(k_out.dtype)
        k_out[:, o + QK_NOPE:o + QK_PAD] = kr


def _mla_prep(p_mla, cos, sin, q_norm, kv_norm, wq, wk, wv, qnn, qnr, knn, knr):
    t = p_mla.shape[0]
    tm = min(512, t)
    row = lambda a: pl.BlockSpec(a.shape, lambda i: (0, 0))
    blk = lambda n: pl.BlockSpec((tm, n), lambda i: (i, 0))
    return pl.pallas_call(
        _mla_prep_body,
        grid=(t // tm,),
        in_specs=[blk(N_MLA_PAD), blk(ROPE_PAD), blk(ROPE_PAD), row(q_norm), row(kv_norm),
                  row(wq), row(wk), row(wv), row(qnn), row(qnr), row(knn), row(knr)],
        out_specs=[blk(MLA_HEADS * QK_PAD), blk(MLA_HEADS * QK_PAD), blk(MLA_DIM)],
        out_shape=[jax.ShapeDtypeStruct((t, MLA_HEADS * QK_PAD), BF16),
                   jax.ShapeDtypeStruct((t, MLA_HEADS * QK_PAD), BF16),
                   jax.ShapeDtypeStruct((t, MLA_DIM), BF16)],
        compiler_params=_cparams(("arbitrary",)),
        name="mla_prep",
    )(p_mla, cos, sin, q_norm, kv_norm, wq, wk, wv, qnn, qnr, knn, knr)


ATT_Q = 1024
ATT_HALF = ATT_Q // 2
ATT_K = 1024


def _attn_update(q, k, v, m_ref, l_ref, acc_ref, half, col_shift):
    s = lax.dot_general(q, k, (((1,), (1,)), ((), ())), preferred_element_type=F32)
    if col_shift is not None:
        rows = lax.broadcasted_iota(jnp.int32, s.shape, 0)
        cols = lax.broadcasted_iota(jnp.int32, s.shape, 1)
        s = jnp.where(cols <= rows + col_shift, s, -jnp.inf)
    tiles = [s[:, c * LANES:(c + 1) * LANES] for c in range(s.shape[1] // LANES)]
    mx = functools.reduce(jnp.maximum, tiles)
    m_old = m_ref[half]
    m_new = jnp.maximum(m_old, jnp.max(mx, axis=1, keepdims=True))
    alpha = jnp.exp2(m_old - m_new)
    ps = [jnp.exp2(x - m_new) for x in tiles]
    l_ref[half] = alpha * l_ref[half] + functools.reduce(jnp.add, ps)
    p = jnp.concatenate([x.astype(v.dtype) for x in ps], axis=1)
    acc_ref[half] = alpha * acc_ref[half] + jnp.dot(p, v, preferred_element_type=F32)
    m_ref[half] = m_new


def _attn_body(q_ref, k_ref, v_ref, o_ref, m_ref, l_ref, acc_ref):
    qi = pl.program_id(1)
    m_ref[...] = jnp.full_like(m_ref, -jnp.inf)
    l_ref[...] = jnp.zeros_like(l_ref)
    acc_ref[...] = jnp.zeros_like(acc_ref)
    halves = [(0, slice(0, ATT_HALF)), (1, slice(ATT_HALF, ATT_Q))]

    def full_block(j, carry):
        r0 = pl.multiple_of(j * ATT_K, ATT_K)
        k = k_ref[pl.ds(r0, ATT_K), :]
        v = v_ref[pl.ds(r0, ATT_K), :]
        for half, rows in halves:
            _attn_update(q_ref[rows, :], k, v, m_ref, l_ref, acc_ref, half, None)
        return carry

    lax.fori_loop(0, qi * (ATT_Q // ATT_K), full_block, 0)

    d0 = pl.multiple_of(qi * ATT_Q, ATT_Q)
    _attn_update(q_ref[:ATT_HALF, :], k_ref[pl.ds(d0, ATT_HALF), :], v_ref[pl.ds(d0, ATT_HALF), :],
                 m_ref, l_ref, acc_ref, 0, 0)
    _attn_update(q_ref[ATT_HALF:, :], k_ref[pl.ds(d0, ATT_Q), :], v_ref[pl.ds(d0, ATT_Q), :],
                 m_ref, l_ref, acc_ref, 1, ATT_HALF)
    for half, rows in halves:
        denom = jnp.sum(l_ref[half], axis=1, keepdims=True)
        o_ref[rows, :] = (acc_ref[half] / denom).astype(o_ref.dtype)


def _attention(q, k, v):
    t = q.shape[0]
    assert t % ATT_Q == 0 and ATT_Q % ATT_K == 0
    stat = pltpu.VMEM((2, ATT_HALF, LANES), F32)
    return pl.pallas_call(
        _attn_body,
        grid=(MLA_HEADS, t // ATT_Q),
        in_specs=[pl.BlockSpec((ATT_Q, QK_PAD), lambda h, i: (i, h)),
                  pl.BlockSpec((t, QK_PAD), lambda h, i: (0, h)),
                  pl.BlockSpec((t, V_HEAD), lambda h, i: (0, h))],
        out_specs=pl.BlockSpec((ATT_Q, V_HEAD), lambda h, i: (i, h)),
        out_shape=jax.ShapeDtypeStruct((t, MLA_DIM), BF16),
        scratch_shapes=[stat, stat, pltpu.VMEM((2, ATT_HALF, V_HEAD), F32)],
        compiler_params=_cparams(("arbitrary", "arbitrary")),
        name="mla_attention",
    )(q, k, v)


def _sg_body(p_ref, lw_ref, lb_ref, ws_ref, bs_ref, o_ref):
    tm = p_ref.shape[0]
    u = _gelu_tanh(p_ref[:, :SG_DIM])
    gv = _gelu_tanh(p_ref[:, SG_DIM:])
    mu = jnp.mean(gv, axis=-1, keepdims=True)
    d = gv - mu
    var = jnp.mean(d * d, axis=-1, keepdims=True)
    vn = (d * lax.rsqrt(var + LN_EPS) * lw_ref[...] + lb_ref[...]).astype(BF16)
    rows = lax.broadcasted_iota(jnp.int32, (CHUNK, CHUNK), 0)
    cols = lax.broadcasted_iota(jnp.int32, (CHUNK, CHUNK), 1)
    tril = cols <= rows
    for g in range(SG_GROUPS):
        gs = slice(g * SG_GROUP_DIM, (g + 1) * SG_GROUP_DIM)
        wg = jnp.where(tril, ws_ref[g], 0.0).astype(BF16)
        for c in range(tm // CHUNK):
            cs = slice(c * CHUNK, (c + 1) * CHUNK)
            s = jnp.dot(wg, vn[cs, gs], preferred_element_type=F32) + bs_ref[:, gs]
            o_ref[cs, gs] = (u[cs, gs] * s).astype(o_ref.dtype)


def _spatial_gating(p_sg, ln_w, ln_b, ws, bs_full):
    t = p_sg.shape[0]
    tm = min(256, t)
    row = pl.BlockSpec((1, SG_DIM), lambda i: (0, 0))
    return pl.pallas_call(
        _sg_body,
        grid=(t // tm,),
        in_specs=[pl.BlockSpec((tm, 2 * SG_DIM), lambda i: (i, 0)), row, row,
                  pl.BlockSpec(ws.shape, lambda i: (0, 0, 0)),
                  pl.BlockSpec(bs_full.shape, lambda i: (0, 0))],
        out_specs=pl.BlockSpec((tm, SG_DIM), lambda i: (i, 0)),
        out_shape=jax.ShapeDtypeStruct((t, SG_DIM), BF16),
        compiler_params=_cparams(("arbitrary",)),
        name="spatial_gating",
    )(p_sg, ln_w, ln_b, ws, bs_full)


def _merge_body(ya_ref, yb_ref, yc_ref, wa_ref, wb_ref, wc_ref, ga_ref, gb_ref, gc_ref, o_ref):
    m = ga_ref[...].astype(F32) * jnp.dot(ya_ref[...], wa_ref[...], preferred_element_type=F32)
    m += gb_ref[...].astype(F32) * jnp.dot(yb_ref[...], wb_ref[...], preferred_element_type=F32)
    m += gc_ref[...].astype(F32) * jnp.dot(yc_ref[...], wc_ref[...], preferred_element_type=F32)
    o_ref[...] = m.astype(o_ref.dtype)


def _merge(ya, yb, yc, wa, wb, wc, gates):
    t = ya.shape[0]
    d = wa.shape[1]
    tm = min(512, t)
    tn = 1024
    nj = d // tn
    yblk = pl.BlockSpec((tm, ya.shape[1]), lambda i, j: (i, 0))
    wblk = pl.BlockSpec((wa.shape[0], tn), lambda i, j: (0, j))
    gblk = lambda b: pl.BlockSpec((tm, tn), lambda i, j: (i, b * nj + j))
    return pl.pallas_call(
        _merge_body,
        grid=(t // tm, nj),
        in_specs=[yblk, yblk, yblk, wblk, wblk, wblk, gblk(0), gblk(1), gblk(2)],
        out_specs=pl.BlockSpec((tm, tn), lambda i, j: (i, j)),
        out_shape=jax.ShapeDtypeStruct((t, d), BF16),
        compiler_params=_cparams(("arbitrary", "arbitrary")),
        name="branch_merge",
    )(ya, yb, yc, wa, wb, wc, gates, gates, gates)


CONV_CARRY = 8


def _ffn_up_body(h_ref, wg_ref, wv_ref, cwg_ref, cwv_ref, cbg_ref, cbv_ref, o_ref,
                 cg_ref, cv_ref, wg16_ref, wv16_ref):
    i = pl.program_id(1)

    @pl.when(i == 0)
    def _():
        cg_ref[...] = jnp.zeros_like(cg_ref)
        cv_ref[...] = jnp.zeros_like(cv_ref)
        wg16_ref[...] = wg_ref[...].astype(BF16)
        wv16_ref[...] = wv_ref[...].astype(BF16)

    h = h_ref[...]
    tm = h.shape[0]
    rowid = lax.broadcasted_iota(jnp.int32, (tm, 1), 0)

    def conv(w_ref, cw_ref, cb_ref, carry_ref):
        up = jnp.dot(h, w_ref[...], preferred_element_type=F32)
        c1 = carry_ref[CONV_CARRY - 1:CONV_CARRY, :]
        c2 = carry_ref[CONV_CARRY - 2:CONV_CARRY - 1, :]
        s1 = jnp.where(rowid == 0, c1, pltpu.roll(up, 1, axis=0))
        s2 = jnp.where(rowid == 0, c2, jnp.where(rowid == 1, c1, pltpu.roll(up, 2, axis=0)))
        carry_ref[...] = up[tm - CONV_CARRY:, :]
        return cb_ref[...] + cw_ref[0:1, :] * s2 + cw_ref[1:2, :] * s1 + cw_ref[2:3, :] * up

    gate = conv(wg16_ref, cwg_ref, cbg_ref, cg_ref)
    val = conv(wv16_ref, cwv_ref, cbv_ref, cv_ref)
    o_ref[...] = (gate * _sigmoid(gate) * val).astype(o_ref.dtype)


def _ffn_up(h, w_up, layer, conv_w, conv_b):
    t, d = h.shape
    tm = min(1024, t)
    tn = 512
    nj = D_FF // tn
    return pl.pallas_call(
        _ffn_up_body,
        grid=(nj, t // tm),
        in_specs=[pl.BlockSpec((tm, d), lambda j, i: (i, 0)),
                  pl.BlockSpec((None, d, tn), lambda j, i: (layer, 0, j)),
                  pl.BlockSpec((None, d, tn), lambda j, i: (layer, 0, nj + j)),
                  pl.BlockSpec((CONV_W, tn), lambda j, i: (0, j)),
                  pl.BlockSpec((CONV_W, tn), lambda j, i: (0, nj + j)),
                  pl.BlockSpec((1, tn), lambda j, i: (0, j)),
                  pl.BlockSpec((1, tn), lambda j, i: (0, nj + j))],
        out_specs=pl.BlockSpec((tm, tn), lambda j, i: (i, j)),
        out_shape=jax.ShapeDtypeStruct((t, D_FF), BF16),
        scratch_shapes=[pltpu.VMEM((CONV_CARRY, tn), F32), pltpu.VMEM((CONV_CARRY, tn), F32),
                        pltpu.VMEM((d, tn), BF16), pltpu.VMEM((d, tn), BF16)],
        compiler_params=_cparams(("arbitrary", "arbitrary")),
        name="ffn_up_conv",
    )(h, w_up, w_up, conv_w, conv_w, conv_b, conv_b)


def _pad_cols(a, n):
    return jnp.pad(a, ((0, 0), (0, n - a.shape[1])))


def _rw_cols(a):
    o = 3 * RW_DIM
    return jnp.concatenate([a[:, :o], _pad_cols(a[:, o:o + W_LORA], LORA_PAD),
                            _pad_cols(a[:, o + W_LORA:o + W_LORA + A_LORA], LORA_PAD),
                            a[:, o + W_LORA + A_LORA:N_RW]], axis=1)


def _pad_rows(a, n):
    return jnp.pad(a, ((0, n - a.shape[0]), (0, 0)))


def kernel(x, c, positions, ada_w, ada_b, norm_mix_g, norm_ffn_g, w_in, rw_mu, rw_w0, rw_w2, rw_a0, rw_a2, rw_g2, rw_kk, rw_ka, rw_rk, rw_lnx_w, rw_lnx_b, rw_v0, rw_v1, rw_v2, mla_q_norm, mla_kv_norm, mla_w_uq, mla_w_ukv, mla_qn_nope, mla_qn_rope, mla_kn_nope, mla_kn_rope, sg_ln_w, sg_ln_b, sg_ws, sg_b, w_br_a, w_br_b, w_br_c, w_out, ffn_up, ffn_conv, ffn_conv_b, ffn_down):
    b_, t, d = x.shape
    assert b_ == 1 and d == D_MODEL
    depth = w_in.shape[0]
    xs = x.reshape(t, d)

    mod = _ada_all(c, ada_w, ada_b)
    cos, sin = _rope_tables(positions, t)
    lane_head = jnp.arange(LANES) // RW_HEAD
    ones_bd = (jnp.tile(lane_head, 2)[:, None] == lane_head[None, :]).astype(BF16)
    row = lambda a: a.reshape(1, -1)

    v_first = None
    for l in range(depth):
        sh1, sc1, gt1, sh2, sc2, gt2 = [mod[l, :, i * d:(i + 1) * d] for i in range(6)]

        h = _norm_mod(xs, row(norm_mix_g[l]), sc1, sh1)
        wl = w_in[l]
        o_dq = N_RW
        o_kr = N_RW + Q_LORA + KV_LORA
        o_sg = o_kr + QK_ROPE
        o_gt = o_sg + 2 * SG_DIM
        w_rw = _rw_cols(wl).astype(BF16)
        w_mla = jnp.concatenate([wl[:, o_dq:o_kr], _rope_pad(wl[:, o_kr:o_sg])], axis=1).astype(BF16)
        p_rw = _matmul(h, w_rw, out_dtype=F32, tm=1024, tn=896, name="proj_rw")
        p_mla = _matmul(h, w_mla, out_dtype=F32, tm=1024, tn=N_MLA_PAD, name="proj_mla")
        p_sg = _matmul(h, w_in, layer=l, col0=o_sg, n=2 * SG_DIM, out_dtype=F32, tm=2048, tn=512,
                       name="proj_sg")
        gates = _matmul(h, w_in, layer=l, col0=o_gt, n=3 * d, out_dtype=BF16, tm=2048, tn=512,
                        epi="sigmoid", name="proj_gates")

        vres = None
        if l > 0:
            vres = (v_first, row(rw_v0[l - 1]), rw_v1[l - 1].astype(BF16),
                    rw_v2[l - 1].astype(BF16))
        r_, w_, k_, v_, a_, b2_, g_ = _rw_prep(
            p_rw, _rw_cols(row(rw_mu[l])), row(rw_w0[l]),
            _pad_rows(rw_w2[l], LORA_PAD).astype(BF16), row(rw_a0[l]),
            _pad_rows(rw_a2[l], LORA_PAD).astype(BF16), rw_g2[l].astype(BF16),
            row(rw_kk[l]), row(rw_ka[l]), ones_bd, vres)
        if l == 0:
            v_first = v_
        y_scan = _rw_scan_chunked(r_, w_, k_, v_, a_, b2_)
        y_a = _rw_post(y_scan, r_, k_, v_, g_, row(rw_lnx_w[l]), row(rw_lnx_b[l]),
                       row(rw_rk[l]), ones_bd)

        wq = mla_w_uq[l].reshape(Q_LORA, MLA_HEADS, QK_NOPE + QK_ROPE)
        wq = jnp.concatenate([wq[..., :QK_NOPE], _rope_pad(wq[..., QK_NOPE:])], axis=-1)
        wq = wq.reshape(Q_LORA, MLA_HEADS * QK_PAD).astype(BF16)
        wkv = mla_w_ukv[l].reshape(KV_LORA, MLA_HEADS, QK_NOPE + V_HEAD)
        wk = wkv[..., :QK_NOPE].reshape(KV_LORA, MLA_HEADS * QK_NOPE).astype(BF16)
        wv = wkv[..., QK_NOPE:].reshape(KV_LORA, MLA_DIM).astype(BF16)
        q, k, v = _mla_prep(p_mla, cos, sin, row(mla_q_norm[l]), row(mla_kv_norm[l]), wq, wk, wv,
                            row(mla_qn_nope[l]), row(_rope_pad(mla_qn_rope[l])),
                            row(mla_kn_nope[l]), row(_rope_pad(mla_kn_rope[l])))
        y_b = _attention(q, k, v)

        bs_full = jnp.repeat(sg_b[l].T, SG_GROUP_DIM, axis=1)
        y_c = _spatial_gating(p_sg, row(sg_ln_w[l]), row(sg_ln_b[l]), sg_ws[l], bs_full)

        merged = _merge(y_a, y_b, y_c, w_br_a[l].astype(BF16), w_br_b[l].astype(BF16),
                        w_br_c[l].astype(BF16), gates)
        xs = _matmul(merged, w_out[l].astype(BF16), out_dtype=F32, tm=1024, tn=1024,
                     epi="residual", res=xs, gate=gt1, name="out_proj")

        h = _norm_mod(xs, row(norm_ffn_g[l]), sc2, sh2)
        act = _ffn_up(h, ffn_up, l, ffn_conv[l], row(ffn_conv_b[l]))
        xs = _matmul(act, ffn_down[l].astype(BF16), out_dtype=F32, tm=1024, tn=1024,
                     tk=D_FF // 2, epi="residual", res=xs, gate=gt2, name="ffn_down")

    return xs.reshape(b_, t, d)
```
